```python
import jax, jax.numpy as jnp
from jax import lax
import numpy as np

D_MODEL = 1024
BATCH = 8
SEQ = 2048
DEPTH = 2

GRID_W = 64
CTX_LEN = 256
N_MIXERS = 2
POOL_WINDOWS = (2, 4, 8, 16)
N_POOL_GROUPS = 4
POOL_GROUP = D_MODEL // N_POOL_GROUPS
N_HEADS = D_MODEL // 128
Q_LORA = D_MODEL // 2
KV_LORA = D_MODEL // 4
QK_NOPE = 128
QK_ROPE = 64
V_HEAD = 128
ROPE_AXIS = QK_ROPE // 2
ROPE_BASE = 10000.0
ATTN_SCALE = (QK_NOPE + QK_ROPE) ** -0.5
Q_BLOCK = 128
N_EXPERTS = 32
TOP_K = 4
D_FF = D_MODEL
SWIGLU_LIMIT = 7.0
SWIGLU_ALPHA = 1.702
EXPERT_BLOCK = 256
EPS = 1e-6
N_POOL_LAYERS = (DEPTH + 1) // 2
N_MLA_LAYERS = DEPTH // 2

kernel_name = "hybrid_pool_mla_moe_prefix_trunk"


def rmsnorm(x, g):
    xf = x.astype(jnp.float32)
    y = xf * lax.rsqrt(jnp.mean(xf * xf, axis=-1, keepdims=True) + EPS)
    return (y * g.astype(jnp.float32)).astype(x.dtype)


def modulate(h, shift, scale):
    return h * (1 + scale) + shift


def centred_mean(h, w):
    L = h.shape[1]
    cs = jnp.cumsum(h.astype(jnp.float32), axis=1)
    cs = jnp.pad(cs, ((0, 0), (1, 0), (0, 0)))
    t = jnp.arange(L)
    lo = jnp.clip(t - w // 2, 0, L)
    hi = jnp.clip(t + w - w // 2, 0, L)
    s = cs[:, hi] - cs[:, lo]
    cnt = (hi - lo).astype(jnp.float32)
    return (s / cnt[None, :, None]).astype(h.dtype)


def pool_mixer(h, w, b, scale):
    B, L, D = h.shape
    hg = h.reshape(B, L, N_POOL_GROUPS, POOL_GROUP)
    d = jnp.stack([centred_mean(hg[:, :, g], POOL_WINDOWS[g]) - hg[:, :, g]
                   for g in range(N_POOL_GROUPS)], axis=2)
    y = jnp.einsum('blgc,gcd->blgd', d, w).reshape(B, L, D) + b
    return y * scale


def rotate(x, ang):
    x1, x2 = jnp.split(x, 2, axis=-1)
    cos = jnp.cos(ang)[None, :, None, :].astype(x.dtype)
    sin = jnp.sin(ang)[None, :, None, :].astype(x.dtype)
    return jnp.concatenate([x1 * cos - x2 * sin, x2 * cos + x1 * sin], axis=-1)


def axial_rope(x, ang_row, ang_col):
    xr, xc = jnp.split(x, 2, axis=-1)
    return jnp.concatenate([rotate(xr, ang_row), rotate(xc, ang_col)], axis=-1)


def mla_q(h, w_dq, q_norm_g, w_uq, ang):
    B, L, _ = h.shape
    q = (rmsnorm(h @ w_dq, q_norm_g) @ w_uq).reshape(B, L, N_HEADS, QK_NOPE + QK_ROPE)
    q_nope, q_pe = q[..., :QK_NOPE], q[..., QK_NOPE:]
    if ang is not None:
        q_pe = axial_rope(q_pe, *ang)
    return jnp.concatenate([q_nope, q_pe], axis=-1)


def mla_kv(h, w_dkv, kv_norm_g, w_ukv, ang):
    B, L, _ = h.shape
    kv_a = h @ w_dkv
    c_kv, k_pe = kv_a[..., :KV_LORA], kv_a[..., KV_LORA:][:, :, None, :]
    kv = (rmsnorm(c_kv, kv_norm_g) @ w_ukv).reshape(B, L, N_HEADS, QK_NOPE + V_HEAD)
    k_nope, v = kv[..., :QK_NOPE], kv[..., QK_NOPE:]
    if ang is not None:
        k_pe = axial_rope(k_pe, *ang)
    k = jnp.concatenate([k_nope, jnp.broadcast_to(k_pe, (B, L, N_HEADS, QK_ROPE))], axis=-1)
    return k, v


def attend(q, k, v):
    s = jnp.einsum('bqhd,bkhd->bhqk', q, k).astype(jnp.float32) * ATTN_SCALE
    p = jax.nn.softmax(s, axis=-1).astype(v.dtype)
    return jnp.einsum('bhqk,bkhd->bqhd', p, v)


def blocked_attention(q, k, v):
    B, L, H, Dk = q.shape
    nb = L // Q_BLOCK
    qb = jnp.moveaxis(q.reshape(B, nb, Q_BLOCK, H, Dk), 1, 0)
    ob = lax.map(lambda qq: attend(qq, k, v), qb)
    return jnp.moveaxis(ob, 0, 1).reshape(B, L, H * V_HEAD)


def moe_ffn(h, router_w, router_b, w_gu, b_gu, w_down, b_down):
    T, D = h.shape
    logits = (h @ router_w + router_b).astype(jnp.float32)
    top_logits, top_e = lax.top_k(logits, TOP_K)
    gates = jax.nn.softmax(top_logits, axis=-1).astype(h.dtype)
    n_assign = T * TOP_K
    flat_e = top_e.reshape(-1)
    order = jnp.argsort(flat_e)
    sorted_e = flat_e[order]
    counts = jnp.bincount(flat_e, length=N_EXPERTS)
    padded = (counts + EXPERT_BLOCK - 1) // EXPERT_BLOCK * EXPERT_BLOCK
    start = jnp.cumsum(counts) - counts
    pends = jnp.cumsum(padded)
    pstart = pends - padded
    slot = (pstart[sorted_e] + jnp.arange(n_assign) - start[sorted_e]).astype(jnp.int32)
    n_blocks = -(-n_assign // EXPERT_BLOCK) + N_EXPERTS
    n_slots = n_blocks * EXPERT_BLOCK
    slot_token = jnp.full((n_slots,), T, jnp.int32).at[slot].set((order // TOP_K).astype(jnp.int32))
    block_expert = jnp.clip(jnp.searchsorted(pends, jnp.arange(n_blocks) * EXPERT_BLOCK, side='right'),
                            0, N_EXPERTS - 1)
    h_pad = jnp.concatenate([h, jnp.zeros((1, D), h.dtype)], axis=0)
    xs = h_pad[slot_token].reshape(n_blocks, EXPERT_BLOCK, D)

    def expert_block(args):
        xb, e = args
        gu = xb @ w_gu[e] + b_gu[e]
        gate = jnp.minimum(gu[:, :D_FF], SWIGLU_LIMIT)
        up = jnp.clip(gu[:, D_FF:], -SWIGLU_LIMIT, SWIGLU_LIMIT)
        act = (up + 1) * (gate * jax.nn.sigmoid(SWIGLU_ALPHA * gate))
        return act @ w_down[e] + b_down[e]

    ys = lax.map(expert_block, (xs, block_expert)).reshape(n_slots, D)
    assign_slot = jnp.zeros((n_assign,), jnp.int32).at[order].set(slot)
    y = ys[assign_slot].reshape(T, TOP_K, D)
    return jnp.einsum('tk,tkd->td', gates, y)


def setup_inputs(seed: int = 0) -> dict:
    key = jax.random.key(seed)
    ks = jax.random.split(key, 32)
    D, f32 = D_MODEL, jnp.float32
    nrm = lambda k, shape, s: jax.random.normal(k, shape, f32) * s
    gain = lambda k, shape: 1.0 + 0.02 * jax.random.normal(k, shape, f32)
    return {
        "x": nrm(ks[0], (BATCH, SEQ, D), 1.0),
        "c": nrm(ks[1], (BATCH, D), 1.0),
        "ctx": nrm(ks[2], (BATCH, CTX_LEN, D), 1.0),
        "c_ctx": nrm(ks[3], (D,), 1.0),
        "ada_w": nrm(ks[4], (DEPTH, D, 6 * D), 0.5 * D ** -0.5),
        "ada_b": nrm(ks[5], (DEPTH, 6 * D), 0.02),
        "norm1_g": gain(ks[6], (DEPTH, D)),
        "norm2_g": gain(ks[7], (DEPTH, D)),
        "pool_w": nrm(ks[8], (N_POOL_LAYERS, N_POOL_GROUPS, POOL_GROUP, POOL_GROUP), POOL_GROUP ** -0.5),
        "pool_b": nrm(ks[9], (N_POOL_LAYERS, D), 0.02),
        "pool_scale": gain(ks[10], (N_POOL_LAYERS, D)),
        "w_dq": nrm(ks[11], (N_MLA_LAYERS, D, Q_LORA), D ** -0.5),
        "q_norm_g": gain(ks[12], (N_MLA_LAYERS, Q_LORA)),
        "w_uq": nrm(ks[13], (N_MLA_LAYERS, Q_LORA, N_HEADS * (QK_NOPE + QK_ROPE)), Q_LORA ** -0.5),
        "w_dkv": nrm(ks[14], (N_MLA_LAYERS, D, KV_LORA + QK_ROPE), D ** -0.5),
        "kv_norm_g": gain(ks[15], (N_MLA_LAYERS, KV_LORA)),
        "w_ukv": nrm(ks[16], (N_MLA_LAYERS, KV_LORA, N_HEADS * (QK_NOPE + V_HEAD)), KV_LORA ** -0.5),
        "w_o": nrm(ks[17], (N_MLA_LAYERS, N_HEADS * V_HEAD, D), (N_HEADS * V_HEAD) ** -0.5),
        "router_w": nrm(ks[18], (DEPTH, D, N_EXPERTS), D ** -0.5),
        "router_b": nrm(ks[19], (DEPTH, N_EXPERTS), 0.01),
        "w_gu": nrm(ks[20], (DEPTH, N_EXPERTS, D, 2 * D_FF), D ** -0.5),
        "b_gu": nrm(ks[21], (DEPTH, N_EXPERTS, 2 * D_FF), 0.02),
        "w_down": nrm(ks[22], (DEPTH, N_EXPERTS, D_FF, D), D_FF ** -0.5),
        "b_down": nrm(ks[23], (DEPTH, N_EXPERTS, D), 0.02),
        "final_g": gain(ks[24], (D,)),
    }


def reference(x, c, ctx, c_ctx, ada_w, ada_b, norm1_g, norm2_g, pool_w, pool_b, pool_scale,
              w_dq, q_norm_g, w_uq, w_dkv, kv_norm_g, w_ukv, w_o,
              router_w, router_b, w_gu, b_gu, w_down, b_down, final_g):
    B, L, D = x.shape
    C = ctx.shape[1]
    rows = L // GRID_W
    row_pos = jnp.repeat(jnp.arange(rows), GRID_W).astype(jnp.float32)
    col_pos = jnp.tile(jnp.arange(GRID_W), rows).astype(jnp.float32)
    inv_freq = ROPE_BASE ** (-jnp.arange(0, ROPE_AXIS, 2, dtype=jnp.float32) / ROPE_AXIS)
    ang = (row_pos[:, None] * inv_freq, col_pos[:, None] * inv_freq)

    xc = ctx
    sc_lat = jax.nn.silu(c)
    sc_ctx = jax.nn.silu(c_ctx)
    for i in range(DEPTH):
        is_last = i == DEPTH - 1
        mixer = i % N_MIXERS
        j = i // N_MIXERS
        mod = sc_lat @ ada_w[i] + ada_b[i]
        sh1, s1, g1, sh2, s2, g2 = jnp.split(mod[:, None, :], 6, axis=-1)
        ctx_needed = (not is_last) or mixer == 1
        if ctx_needed:
            modc = sc_ctx @ ada_w[i] + ada_b[i]
            sh1c, s1c, g1c, sh2c, s2c, g2c = jnp.split(modc, 6, axis=-1)
            hc = modulate(rmsnorm(xc, norm1_g[i]), sh1c, s1c)
        h = modulate(rmsnorm(x, norm1_g[i]), sh1, s1)

        if mixer == 0:
            x = x + g1 * pool_mixer(h, pool_w[j], pool_b[j], pool_scale[j])
            if not is_last:
                xc = xc + g1c * pool_mixer(hc, pool_w[j], pool_b[j], pool_scale[j])
        else:
            kc, vc = mla_kv(hc, w_dkv[j], kv_norm_g[j], w_ukv[j], None)
            kl, vl = mla_kv(h, w_dkv[j], kv_norm_g[j], w_ukv[j], ang)
            ql = mla_q(h, w_dq[j], q_norm_g[j], w_uq[j], ang)
            k_all = jnp.concatenate([kc, kl], axis=1)
            v_all = jnp.concatenate([vc, vl], axis=1)
            x = x + g1 * (blocked_attention(ql, k_all, v_all) @ w_o[j])
            if not is_last:
                qc = mla_q(hc, w_dq[j], q_norm_g[j], w_uq[j], None)
                yc = attend(qc, kc, vc).reshape(B, C, N_HEADS * V_HEAD)
                xc = xc + g1c * (yc @ w_o[j])

        h2 = modulate(rmsnorm(x, norm2_g[i]), sh2, s2)
        if is_last:
            y = moe_ffn(h2.reshape(B * L, D), router_w[i], router_b[i], w_gu[i], b_gu[i],
                        w_down[i], b_down[i])
            x = x + g2 * y.reshape(B, L, D)
        else:
            h2c = modulate(rmsnorm(xc, norm2_g[i]), sh2c, s2c)
            tokens = jnp.concatenate([h2.reshape(B * L, D), h2c.reshape(B * C, D)], axis=0)
            y = moe_ffn(tokens, router_w[i], router_b[i], w_gu[i], b_gu[i], w_down[i], b_down[i])
            x = x + g2 * y[:B * L].reshape(B, L, D)
            xc = xc + g2c * y[B * L:].reshape(B, C, D)

    return rmsnorm(x, final_g)
```

```python
import functools

import jax
import jax.numpy as jnp
from jax import lax
from jax.experimental import pallas as pl
from jax.experimental.pallas import tpu as pltpu

F32 = jnp.float32
BF16 = jnp.bfloat16

N_HEADS = 8
QK_NOPE = 128
QK_ROPE = 64
V_HEAD = 128
KV_LORA = 256
N_EXPERTS = 32
TOP_K = 4
POOL_WINDOWS = (2, 4, 8, 16)
GRID_W = 64
ROPE_BASE = 10000.0
ATTN_SCALE = (QK_NOPE + QK_ROPE) ** -0.5
SWIGLU_LIMIT = 7.0
SWIGLU_ALPHA = 1.702
EPS = 1e-6

LANES = 128
SUBLANES = 8
HEAD_PAD = 2 * LANES
VMEM_LIMIT = 56 * 1024 * 1024

SEQ_TILE = 256
ROUTE_TILE = 512
EXPERT_BLOCK = 256
Q_TILE = 512
MOD_COLS = 1536
HALO = 8


def _cparams(sem, vmem=VMEM_LIMIT):
    return pltpu.CompilerParams(dimension_semantics=sem, vmem_limit_bytes=vmem)


def _dot(a, b):
    return jnp.dot(a, b, preferred_element_type=F32)


def _dot3(a, b):
    ah = a.astype(BF16)
    al = (a - ah.astype(F32)).astype(BF16)
    bh = b.astype(BF16)
    bl = (b - bh.astype(F32)).astype(BF16)
    return _dot(ah, bh) + _dot(ah, bl) + _dot(al, bh)


def _rms(x, g):
    return x * lax.rsqrt(jnp.mean(x * x, axis=-1, keepdims=True) + EPS) * g


def _sigmoid(x):
    return 1.0 / (1.0 + jnp.exp(-x))


def _mod_kernel(c_ref, w_ref, b_ref, o_ref):
    a = c_ref[...]
    s = a * _sigmoid(a)
    o_ref[0] = _dot3(s, w_ref[0]) + b_ref[0]


def _modulation(cvec, ada_w, ada_b):
    depth, d, n = ada_w.shape
    rows = cvec.shape[0]
    return pl.pallas_call(
        _mod_kernel,
        grid=(depth, n // MOD_COLS),
        in_specs=[
            pl.BlockSpec((rows, d), lambda i, j: (0, 0)),
            pl.BlockSpec((1, d, MOD_COLS), lambda i, j: (i, 0, j)),
            pl.BlockSpec((1, 1, MOD_COLS), lambda i, j: (i, 0, j)),
        ],
        out_specs=pl.BlockSpec((1, rows, MOD_COLS), lambda i, j: (i, 0, j)),
        out_shape=jax.ShapeDtypeStruct((depth, rows, n), F32),
        compiler_params=_cparams(("arbitrary", "arbitrary")),
        name="modulation",
    )(cvec, ada_w, ada_b.reshape(depth, 1, n))


def _store_row_major(ref, val, rows):
    for c in range(val.shape[1] // LANES):
        ref[0, pl.ds(c, rows, stride=SUBLANES), :] = val[:, c * LANES:(c + 1) * LANES]


def _tail(x1, mod, n2_ref, rw_ref, rb_ref, x1_ref, h2_ref, lg_ref, rows):
    sh2, s2 = mod[3:4], mod[4:5]
    h2 = _rms(x1, n2_ref[...]) * (1.0 + s2) + sh2
    x1_ref[0] = x1
    _store_row_major(h2_ref, h2, rows)
    lg_ref[0] = _dot3(h2, rw_ref[...]) + rb_ref[...]


def _pool_kernel(x_ref, mod_ref, n1_ref, n2_ref, pw_ref, pb_ref, ps_ref, rw_ref, rb_ref,
                 x1_ref, h2_ref, lg_ref, *, rows, seq):
    j = pl.program_id(1)
    start = pl.multiple_of(j * rows, rows)
    prev0 = pl.multiple_of(jnp.maximum(start - HALO, 0), HALO)
    next0 = pl.multiple_of(jnp.minimum(start + rows, seq - HALO), HALO)
    xm = x_ref[0, pl.ds(start, rows), :]
    xe = jnp.concatenate(
        [x_ref[0, pl.ds(prev0, HALO), :], xm, x_ref[0, pl.ds(next0, HALO), :]], axis=0)
    mod = mod_ref[0]
    sh1, s1, g1 = mod[0:1], mod[1:2], mod[2:3]
    h = _rms(xe, n1_ref[...]) * (1.0 + s1) + sh1
    pos = start - HALO + lax.broadcasted_iota(jnp.int32, (rows + 2 * HALO, 1), 0)
    hz = jnp.where((pos >= 0) & (pos < seq), h, 0.0)
    tpos = start + lax.broadcasted_iota(jnp.int32, (rows, 1), 0)
    group = hz.shape[1] // len(POOL_WINDOWS)
    ys = []
    for g, w in enumerate(POOL_WINDOWS):
        half = w // 2
        hg = hz[:, g * group:(g + 1) * group]
        tot = hg[HALO - half:HALO - half + rows]
        for o in range(1 - half, half):
            tot = tot + hg[HALO + o:HALO + o + rows]
        cnt = (jnp.minimum(tpos + half, seq) - jnp.maximum(tpos - half, 0)).astype(F32)
        dlt = tot / cnt - hg[HALO:HALO + rows]
        ys.append(_dot(dlt.astype(BF16), pw_ref[g]))
    y = (jnp.concatenate(ys, axis=1) + pb_ref[...]) * ps_ref[...]
    x1 = xm + g1 * y
    _tail(x1, mod, n2_ref, rw_ref, rb_ref, x1_ref, h2_ref, lg_ref, rows)


def _pool_layer(x, mod8, n1, n2, pw, pb, ps, rw, rb):
    bsz, seq, d = x.shape
    rows = min(SEQ_TILE, seq)
    grid = (bsz, seq // rows)
    const2 = lambda b, j: (0, 0)
    return pl.pallas_call(
        functools.partial(_pool_kernel, rows=rows, seq=seq),
        grid=grid,
        in_specs=[
            pl.BlockSpec((1, seq, d), lambda b, j: (b, 0, 0)),
            pl.BlockSpec((1, SUBLANES, d), lambda b, j: (b, 0, 0)),
            pl.BlockSpec((1, d), const2),
            pl.BlockSpec((1, d), const2),
            pl.BlockSpec(pw.shape, lambda b, j: (0, 0, 0)),
            pl.BlockSpec((1, d), const2),
            pl.BlockSpec((1, d), const2),
            pl.BlockSpec(rw.shape, const2),
            pl.BlockSpec((1, N_EXPERTS), const2),
        ],
        out_specs=[
            pl.BlockSpec((1, rows, d), lambda b, j: (b, j, 0)),
            pl.BlockSpec((1, rows * SUBLANES, LANES), lambda b, j: (b, j, 0)),
            pl.BlockSpec((1, rows, N_EXPERTS), lambda b, j: (b, j, 0)),
        ],
        out_shape=[
            jax.ShapeDtypeStruct((bsz, seq, d), F32),
            jax.ShapeDtypeStruct((bsz, seq * SUBLANES, LANES), F32),
            jax.ShapeDtypeStruct((bsz, seq, N_EXPERTS), F32),
        ],
        compiler_params=_cparams(("arbitrary", "arbitrary")),
        name="pool_layer",
    )(x, mod8, n1, n2, pw, pb, ps, rw, rb)


def _route_kernel(lg_ref, te_ref, gt_ref, rk_ref, cnt_ref, carry_ref, *, rows):
    i = pl.program_id(0)

    @pl.when(i == 0)
    def _():
        carry_ref[...] = jnp.zeros_like(carry_ref)

    work = lg_ref[...]
    lane = lax.broadcasted_iota(jnp.int32, work.shape, 1).astype(F32)
    tops, idxs, hots = [], [], []
    for _ in range(TOP_K):
        m = jnp.max(work, axis=1, keepdims=True)
        idx = jnp.min(jnp.where(work == m, lane, float(N_EXPERTS)), axis=1, keepdims=True)
        hot = lane == idx
        work = jnp.where(hot, -jnp.inf, work)
        tops.append(m)
        idxs.append(idx)
        hots.append(hot)
    exps = [jnp.exp(t - tops[0]) for t in tops]
    den = exps[0] + exps[1] + exps[2] + exps[3]
    chosen = jnp.zeros(work.shape, F32)
    for hot in hots:
        chosen = chosen + hot.astype(F32)
    r = lax.broadcasted_iota(jnp.int32, (rows, rows), 0)
    c = lax.broadcasted_iota(jnp.int32, (rows, rows), 1)
    earlier = jnp.where(c < r, 1.0, 0.0).astype(BF16)
    before = _dot(earlier, chosen.astype(BF16)) + carry_ref[...]
    col = lax.broadcasted_iota(jnp.int32, (rows, TOP_K), 1)
    te = jnp.zeros((rows, TOP_K), F32)
    gt = jnp.zeros((rows, TOP_K), F32)
    rk = jnp.zeros((rows, TOP_K), F32)
    for k in range(TOP_K):
        rank_k = jnp.sum(jnp.where(hots[k], before, 0.0), axis=1, keepdims=True)
        te = jnp.where(col == k, idxs[k], te)
        gt = jnp.where(col == k, exps[k] / den, gt)
        rk = jnp.where(col == k, rank_k, rk)
    te_ref[...] = te.astype(jnp.int32)
    gt_ref[...] = gt
    rk_ref[...] = rk.astype(jnp.int32)
    carry_ref[...] = carry_ref[...] + jnp.sum(chosen, axis=0, keepdims=True)
    cnt_ref[...] = carry_ref[...].astype(jnp.int32)


def _route(logits):
    t = logits.shape[0]
    rows = min(ROUTE_TILE, t)
    tok = pl.BlockSpec((rows, TOP_K), lambda i: (i, 0))
    return pl.pallas_call(
        functools.partial(_route_kernel, rows=rows),
        grid=(t // rows,),
        in_specs=[pl.BlockSpec((rows, N_EXPERTS), lambda i: (i, 0))],
        out_specs=[tok, tok, tok, pl.BlockSpec((1, N_EXPERTS), lambda i: (0, 0))],
        out_shape=[
            jax.ShapeDtypeStruct((t, TOP_K), jnp.int32),
            jax.ShapeDtypeStruct((t, TOP_K), F32),
            jax.ShapeDtypeStruct((t, TOP_K), jnp.int32),
            jax.ShapeDtypeStruct((1, N_EXPERTS), jnp.int32),
        ],
        scratch_shapes=[pltpu.VMEM((1, N_EXPERTS), F32)],
        compiler_params=_cparams(("arbitrary",)),
        name="route",
    )(logits)


def _dispatch_kernel(ends_ref, nu_ref, slot_ref, *refs, rows, tiles, n_blocks):
    n_src = len(tiles)
    srcs, xs_ref, zeros, sem, zsem = refs[:n_src], refs[n_src], *refs[n_src + 1:]
    i = pl.program_id(0)
    blk = EXPERT_BLOCK

    @pl.when(i == 0)
    def _():
        zeros[...] = jnp.zeros_like(zeros)

        def fill(b):
            return pltpu.make_async_copy(zeros, xs_ref.at[pl.ds(b * blk, blk)], zsem)

        def expert_tails(go):
            for e in range(N_EXPERTS):
                lo = ends_ref[e - 1] if e else 0

                @pl.when(ends_ref[e] > lo)
                def _(e=e):
                    go(fill(ends_ref[e] // blk - 1))

        def unused(go):
            def body(b, carry):
                go(fill(b))
                return carry
            lax.fori_loop(nu_ref[0], n_blocks, body, 0)

        expert_tails(lambda cp: cp.start())
        unused(lambda cp: cp.start())
        expert_tails(lambda cp: cp.wait())
        unused(lambda cp: cp.wait())

    def copies(src, t0, t):
        return [pltpu.make_async_copy(src.at[t0 + t], xs_ref.at[slot_ref[t * TOP_K + k]], sem)
                for k in range(TOP_K)]

    first = 0
    for src, n in zip(srcs, tiles):
        @pl.when((i >= first) & (i < first + n))
        def _(src=src, first=first):
            t0 = (i - first) * rows

            def issue(t, carry):
                for cp in copies(src, t0, t):
                    cp.start()
                return carry

            def drain(t, carry):
                for cp in copies(src, t0, t):
                    cp.wait()
                return carry

            lax.fori_loop(0, rows, issue, 0)
            lax.fori_loop(0, rows, drain, 0)
        first += n


def _dispatch(ends, n_used, slot_flat, sources, n_blocks):
    rows = SEQ_TILE
    tiles = tuple(s.shape[0] // rows for s in sources)
    grid_spec = pltpu.PrefetchScalarGridSpec(
        num_scalar_prefetch=2,
        grid=(sum(tiles),),
        in_specs=[pl.BlockSpec((rows * TOP_K,), lambda i, *_: (i,), memory_space=pltpu.SMEM)]
        + [pl.BlockSpec(memory_space=pl.ANY)] * len(sources),
        out_specs=pl.BlockSpec(memory_space=pl.ANY),
        scratch_shapes=[pltpu.VMEM((EXPERT_BLOCK, SUBLANES, LANES), F32),
                        pltpu.SemaphoreType.DMA, pltpu.SemaphoreType.DMA],
    )
    return pl.pallas_call(
        functools.partial(_dispatch_kernel, rows=rows, tiles=tiles, n_blocks=n_blocks),
        grid_spec=grid_spec,
        out_shape=jax.ShapeDtypeStruct((n_blocks * EXPERT_BLOCK, SUBLANES, LANES), F32),
        compiler_params=_cparams(("arbitrary",)),
        name="dispatch",
    )(ends, n_used, slot_flat, *sources)


def _experts_kernel(be_ref, nu_ref, xs_ref, wgu_ref, bgu_ref, wd_ref, bd_ref, ys_ref,
                    wgu_bf, wd_bf, *, rows, dff):
    b = pl.program_id(0)
    prev = be_ref[jnp.maximum(b - 1, 0)]
    fresh = (b == 0) | (be_ref[b] != prev)
    live = b < nu_ref[0]

    @pl.when(live & fresh)
    def _():
        wgu_bf[...] = wgu_ref[0].astype(BF16)
        wd_bf[...] = wd_ref[0].astype(BF16)

    @pl.when(live)
    def _():
        chunks = [xs_ref[pl.ds(c, rows, stride=SUBLANES), :] for c in range(SUBLANES)]
        x = jnp.concatenate(chunks, axis=1).astype(BF16)
        gu = _dot(x, wgu_bf[...]) + bgu_ref[0]
        gate = jnp.minimum(gu[:, :dff], SWIGLU_LIMIT)
        up = jnp.clip(gu[:, dff:], -SWIGLU_LIMIT, SWIGLU_LIMIT)
        act = (up + 1.0) * (gate * _sigmoid(SWIGLU_ALPHA * gate))
        y = _dot(act.astype(BF16), wd_bf[...]) + bd_ref[0]
        for c in range(SUBLANES):
            ys_ref[pl.ds(c, rows, stride=SUBLANES), :] = y[:, c * LANES:(c + 1) * LANES]

    @pl.when(jnp.logical_not(live))
    def _():
        ys_ref[...] = jnp.zeros_like(ys_ref)


def _experts(block_expert, n_used, xs, w_gu, b_gu, w_down, b_down):
    n_slots = xs.shape[0]
    rows = EXPERT_BLOCK
    n_blocks = n_slots // rows
    e, d, dff2 = w_gu.shape
    dff = dff2 // 2
    blk = lambda b, be, nu: (jnp.minimum(b, nu[0] - 1), 0)
    grid_spec = pltpu.PrefetchScalarGridSpec(
        num_scalar_prefetch=2,
        grid=(n_blocks,),
        in_specs=[
            pl.BlockSpec((rows * SUBLANES, LANES), blk),
            pl.BlockSpec((1, d, dff2), lambda b, be, nu: (be[b], 0, 0)),
            pl.BlockSpec((1, 1, dff2), lambda b, be, nu: (be[b], 0, 0)),
            pl.BlockSpec((1, dff, d), lambda b, be, nu: (be[b], 0, 0)),
            pl.BlockSpec((1, 1, d), lambda b, be, nu: (be[b], 0, 0)),
        ],
        out_specs=pl.BlockSpec((rows * SUBLANES, LANES), lambda b, be, nu: (b, 0)),
        scratch_shapes=[pltpu.VMEM((d, dff2), BF16), pltpu.VMEM((dff, d), BF16)],
    )
    ys = pl.pallas_call(
        functools.partial(_experts_kernel, rows=rows, dff=dff),
        grid_spec=grid_spec,
        out_shape=jax.ShapeDtypeStruct((n_slots * SUBLANES, LANES), F32),
        compiler_params=_cparams(("arbitrary",)),
        name="experts",
    )(block_expert, n_used, xs.reshape(n_slots * SUBLANES, LANES), w_gu,
      b_gu.reshape(e, 1, dff2), w_down, b_down.reshape(e, 1, d))
    return ys.reshape(n_slots, SUBLANES, LANES)


def _combine_kernel(slot_ref, ys_ref, gt_ref, x1_ref, mod_ref, fg_ref, out_ref, buf, sem,
                    *, rows, final):
    def copies(t):
        return [pltpu.make_async_copy(ys_ref.at[slot_ref[t * TOP_K + k]],
                                      buf.at[k, pl.ds(t * SUBLANES, SUBLANES)], sem)
                for k in range(TOP_K)]

    def issue(t, carry):
        for cp in copies(t):
            cp.start()
        return carry

    def drain(t, carry):
        for cp in copies(t):
            cp.wait()
        return carry

    lax.fori_loop(0, rows, issue, 0)
    lax.fori_loop(0, rows, drain, 0)

    g2 = mod_ref[0][5:6]
    gates = gt_ref[...]
    x1 = x1_ref[0]
    outs = []
    for c in range(SUBLANES):
        acc = gates[:, 0:1] * buf[0, pl.ds(c, rows, stride=SUBLANES), :]
        for k in range(1, TOP_K):
            acc = acc + gates[:, k:k + 1] * buf[k, pl.ds(c, rows, stride=SUBLANES), :]
        sl = slice(c * LANES, (c + 1) * LANES)
        outs.append(x1[:, sl] + g2[:, sl] * acc)
    x2 = jnp.concatenate(outs, axis=1)
    if final:
        x2 = _rms(x2, fg_ref[...])
    out_ref[0] = x2


def _combine(slot_flat, gates, ys, x1, mod8, final_g, final):
    bsz, seq, d = x1.shape
    rows = min(SEQ_TILE, seq)
    per = seq // rows
    return pl.pallas_call(
        functools.partial(_combine_kernel, rows=rows, final=final),
        grid=(bsz, per),
        in_specs=[
            pl.BlockSpec((rows * TOP_K,), lambda b, j: (b * per + j,), memory_space=pltpu.SMEM),
            pl.BlockSpec(memory_space=pl.ANY),
            pl.BlockSpec((rows, TOP_K), lambda b, j: (b * per + j, 0)),
            pl.BlockSpec((1, rows, d), lambda b, j: (b, j, 0)),
            pl.BlockSpec((1, SUBLANES, d), lambda b, j: (b, 0, 0)),
            pl.BlockSpec((1, d), lambda b, j: (0, 0)),
        ],
        out_specs=pl.BlockSpec((1, rows, d), lambda b, j: (b, j, 0)),
        out_shape=jax.ShapeDtypeStruct((bsz, seq, d), F32),
        scratch_shapes=[pltpu.VMEM((TOP_K, rows * SUBLANES, LANES), F32), pltpu.SemaphoreType.DMA],
        compiler_params=_cparams(("arbitrary", "arbitrary")),
        name="combine",
    )(slot_flat, ys, gates, x1, mod8, final_g)


def _moe(streams, w_gu, b_gu, w_down, b_down, final_g, final):
    logits = jnp.concatenate([s[2].reshape(-1, N_EXPERTS) for s in streams], axis=0)
    t = logits.shape[0]
    top_e, gates, rank, counts = _route(logits)
    counts = counts[0]
    blk = EXPERT_BLOCK
    padded = (counts + blk - 1) // blk * blk
    ends = jnp.cumsum(padded)
    starts = ends - padded
    slot = (starts[top_e] + rank).astype(jnp.int32)
    n_blocks = -(-(t * TOP_K) // blk) + N_EXPERTS
    n_used = (ends[-1] // blk).astype(jnp.int32)
    bexp = jnp.searchsorted(ends, jnp.arange(n_blocks, dtype=jnp.int32) * blk, side="right")
    bexp = jnp.clip(bexp, 0, N_EXPERTS - 1).astype(jnp.int32)
    bexp = jnp.where(jnp.arange(n_blocks) < n_used, bexp, bexp[jnp.maximum(n_used - 1, 0)])
    slot_flat = slot.reshape(-1)
    sources = [s[1].reshape(-1, SUBLANES, LANES) for s in streams]
    n_used = n_used.reshape(1)
    xs = _dispatch(ends.astype(jnp.int32), n_used, slot_flat, sources, n_blocks)
    ys = _experts(bexp, n_used, xs, w_gu, b_gu, w_down, b_down)
    outs, off = [], 0
    for x1, _, _, mod8 in streams:
        n = x1.shape[0] * x1.shape[1]
        outs.append(_combine(slot_flat[off * TOP_K:(off + n) * TOP_K], gates[off:off + n], ys,
                             x1, mod8, final_g, final))
        off += n
    return outs


def _rope(x, cos, sin):
    return x * cos + pltpu.roll(x, LANES // 2, axis=1) * sin


def _proj_kernel(xc_ref, x_ref, modc_ref, modl_ref, n1_ref, cos_ref, sin_ref,
                 wdq_ref, qg_ref, wuq_ref, wdkv_ref, kvg_ref, wukv_ref,
                 q_ref, k_ref, v_ref, *, ctx_tiles):
    j = pl.program_id(1)
    is_ctx = j < ctx_tiles
    xin = jnp.where(is_ctx, xc_ref[0], x_ref[0])
    mod = jnp.where(is_ctx, modc_ref[0], modl_ref[0])
    sh1, s1 = mod[0:1], mod[1:2]
    hb = (_rms(xin, n1_ref[...]) * (1.0 + s1) + sh1).astype(BF16)
    cos = jnp.where(is_ctx, 1.0, cos_ref[...])
    sin = jnp.where(is_ctx, 0.0, sin_ref[...])

    kva = _dot(hb, wdkv_ref[...])
    kpe = _rope(kva[:, KV_LORA:], cos, sin).astype(BF16)
    kv = _dot(_rms(kva[:, :KV_LORA], kvg_ref[...]).astype(BF16), wukv_ref[...])
    nope_all = N_HEADS * QK_NOPE
    for h in range(N_HEADS):
        k_ref[0, :, h * HEAD_PAD:h * HEAD_PAD + QK_NOPE] = \
            kv[:, h * QK_NOPE:(h + 1) * QK_NOPE].astype(BF16)
        k_ref[0, :, h * HEAD_PAD + QK_NOPE:(h + 1) * HEAD_PAD] = kpe
    v_ref[0] = kv[:, nope_all:].astype(BF16)

    @pl.when(jnp.logical_not(is_ctx))
    def _():
        qa = _dot(hb, wdq_ref[...])
        q = _dot(_rms(qa, qg_ref[...]).astype(BF16), wuq_ref[...])
        for h in range(N_HEADS):
            lo = h * HEAD_PAD
            q_ref[0, :, lo:lo + QK_NOPE] = q[:, lo:lo + QK_NOPE].astype(BF16)
            q_ref[0, :, lo + QK_NOPE:lo + HEAD_PAD] = \
                _rope(q[:, lo + QK_NOPE:lo + HEAD_PAD], cos, sin).astype(BF16)


def _projections(xc, x, modc8, modl8, n1, cos_t, sin_t, wdq, qg, wuq, wdkv, kvg, wukv):
    bsz, seq, d = x.shape
    ctx = xc.shape[1]
    rows = SEQ_TILE
    ctx_tiles = ctx // rows
    lat = lambda b, j: (b, jnp.maximum(j - ctx_tiles, 0), 0)
    full = lambda a: pl.BlockSpec(a.shape, lambda b, j: (0,) * a.ndim)
    kw = N_HEADS * HEAD_PAD
    vw = N_HEADS * V_HEAD
    return pl.pallas_call(
        functools.partial(_proj_kernel, ctx_tiles=ctx_tiles),
        grid=(bsz, (ctx + seq) // rows),
        in_specs=[
            pl.BlockSpec((1, rows, d), lambda b, j: (b, jnp.minimum(j, ctx_tiles - 1), 0)),
            pl.BlockSpec((1, rows, d), lat),
            pl.BlockSpec((1, SUBLANES, d), lambda b, j: (b, 0, 0)),
            pl.BlockSpec((1, SUBLANES, d), lambda b, j: (b, 0, 0)),
            full(n1),
            pl.BlockSpec((rows, LANES), lambda b, j: (jnp.maximum(j - ctx_tiles, 0), 0)),
            pl.BlockSpec((rows, LANES), lambda b, j: (jnp.maximum(j - ctx_tiles, 0), 0)),
            full(wdq), full(qg), full(wuq), full(wdkv), full(kvg), full(wukv),
        ],
        out_specs=[
            pl.BlockSpec((1, rows, kw), lat),
            pl.BlockSpec((1, rows, kw), lambda b, j: (b, j, 0)),
            pl.BlockSpec((1, rows, vw), lambda b, j: (b, j, 0)),
        ],
        out_shape=[
            jax.ShapeDtypeStruct((bsz, seq, kw), BF16),
            jax.ShapeDtypeStruct((bsz, ctx + seq, kw), BF16),
            jax.ShapeDtypeStruct((bsz, ctx + seq, vw), BF16),
        ],
        compiler_params=_cparams(("arbitrary", "arbitrary")),
        name="projections",
    )(xc, x, modc8, modl8, n1, cos_t, sin_t, wdq, qg, wuq, wdkv, kvg, wukv)


def _attn_kernel(q_ref, k_ref, v_ref, o_ref):
    s = lax.dot_general(q_ref[0], k_ref[0], (((1,), (1,)), ((), ())),
                        preferred_element_type=F32) * ATTN_SCALE
    m = jnp.max(s, axis=-1, keepdims=True)
    p = jnp.exp(s - m)
    den = jnp.sum(p, axis=-1, keepdims=True)
    o_ref[0] = (_dot(p.astype(BF16), v_ref[0]) / den).astype(BF16)


def _attention(q, k, v):
    bsz, seq, _ = q.shape
    keys = k.shape[1]
    rows = min(Q_TILE, seq)
    return pl.pallas_call(
        _attn_kernel,
        grid=(bsz, N_HEADS, seq // rows),
        in_specs=[
            pl.BlockSpec((1, rows, HEAD_PAD), lambda b, h, i: (b, i, h)),
            pl.BlockSpec((1, keys, HEAD_PAD), lambda b, h, i: (b, 0, h)),
            pl.BlockSpec((1, keys, V_HEAD), lambda b, h, i: (b, 0, h)),
        ],
        out_specs=pl.BlockSpec((1, rows, V_HEAD), lambda b, h, i: (b, i, h)),
        out_shape=jax.ShapeDtypeStruct((bsz, seq, N_HEADS * V_HEAD), BF16),
        compiler_params=_cparams(("arbitrary", "arbitrary", "arbitrary")),
        name="attention",
    )(q, k, v)


def _oproj_kernel(o_ref, x_ref, mod_ref, wo_ref, n2_ref, rw_ref, rb_ref,
                  x1_ref, h2_ref, lg_ref, *, rows):
    mod = mod_ref[0]
    x1 = x_ref[0] + mod[2:3] * _dot(o_ref[0], wo_ref[...])
    _tail(x1, mod, n2_ref, rw_ref, rb_ref, x1_ref, h2_ref, lg_ref, rows)


def _out_projection(o, x, mod8, wo, n2, rw, rb):
    bsz, seq, d = x.shape
    rows = SEQ_TILE
    const2 = lambda b, j: (0, 0)
    return pl.pallas_call(
        functools.partial(_oproj_kernel, rows=rows),
        grid=(bsz, seq // rows),
        in_specs=[
            pl.BlockSpec((1, rows, o.shape[2]), lambda b, j: (b, j, 0)),
            pl.BlockSpec((1, rows, d), lambda b, j: (b, j, 0)),
            pl.BlockSpec((1, SUBLANES, d), lambda b, j: (b, 0, 0)),
            pl.BlockSpec(wo.shape, const2),
            pl.BlockSpec((1, d), const2),
            pl.BlockSpec(rw.shape, const2),
            pl.BlockSpec((1, N_EXPERTS), const2),
        ],
        out_specs=[
            pl.BlockSpec((1, rows, d), lambda b, j: (b, j, 0)),
            pl.BlockSpec((1, rows * SUBLANES, LANES), lambda b, j: (b, j, 0)),
            pl.BlockSpec((1, rows, N_EXPERTS), lambda b, j: (b, j, 0)),
        ],
        out_shape=[
            jax.ShapeDtypeStruct((bsz, seq, d), F32),
            jax.ShapeDtypeStruct((bsz, seq * SUBLANES, LANES), F32),
            jax.ShapeDtypeStruct((bsz, seq, N_EXPERTS), F32),
        ],
        compiler_params=_cparams(("arbitrary", "arbitrary")),
        name="out_projection",
    )(o, x, mod8, wo, n2, rw, rb)


def _rope_cols(base):
    q = QK_ROPE // 4
    x1 = list(range(base, base + q)) + list(range(base + 2 * q, base + 3 * q))
    x2 = list(range(base + q, base + 2 * q)) + list(range(base + 3 * q, base + 4 * q))
    pad = [-1] * (LANES // 2 - 2 * q)
    return x1 + pad + x2 + pad


def _take_cols(w, cols):
    wz = jnp.concatenate([w, jnp.zeros((w.shape[0], 1), w.dtype)], axis=1)
    idx = jnp.asarray([c if c >= 0 else w.shape[1] for c in cols], jnp.int32)
    return jnp.take(wz, idx, axis=1)


def _rope_tables(seq):
    q = QK_ROPE // 4
    pos = jnp.arange(seq)
    inv = ROPE_BASE ** (-jnp.arange(0, QK_ROPE // 2, 2, dtype=F32) / (QK_ROPE // 2))
    ang = jnp.concatenate([(pos // GRID_W).astype(F32)[:, None] * inv,
                           (pos % GRID_W).astype(F32)[:, None] * inv], axis=1)
    pad1 = jnp.ones((seq, LANES // 2 - 2 * q), F32)
    pad0 = jnp.zeros((seq, LANES // 2 - 2 * q), F32)
    cos = jnp.concatenate([jnp.cos(ang), pad1, jnp.cos(ang), pad1], axis=1)
    sin = jnp.concatenate([-jnp.sin(ang), pad0, jnp.sin(ang), pad0], axis=1)
    return cos, sin


def _mod8(mod_rows):
    bsz, n = mod_rows.shape
    d = n // 6
    m = mod_rows.reshape(bsz, 6, d)
    return jnp.concatenate([m, jnp.zeros((bsz, SUBLANES - 6, d), F32)], axis=1)


def kernel(x, c, ctx, c_ctx, ada_w, ada_b, norm1_g, norm2_g, pool_w, pool_b, pool_scale,
           w_dq, q_norm_g, w_uq, w_dkv, kv_norm_g, w_ukv, w_o,
           router_w, router_b, w_gu, b_gu, w_down, b_down, final_g):
    bsz, seq, d = x.shape
    assert seq % SEQ_TILE == 0 and ctx.shape[1] % SEQ_TILE == 0 and seq % GRID_W == 0
    row = lambda a: a.reshape(1, -1)

    mod_rows = 2 * SUBLANES
    cvec = jnp.concatenate([c, c_ctx[None], jnp.zeros((mod_rows - bsz - 1, d), F32)], axis=0)
    mod = _modulation(cvec, ada_w, ada_b)
    modl = [_mod8(mod[i, :bsz]) for i in range(2)]
    modc = [_mod8(jnp.broadcast_to(mod[i, bsz:bsz + 1], (bsz, 6 * d))) for i in range(2)]

    pw = pool_w[0].astype(BF16)
    args0 = (row(norm1_g[0]), row(norm2_g[0]), pw, row(pool_b[0]), row(pool_scale[0]),
             router_w[0], row(router_b[0]))
    x1, h2, lg = _pool_layer(x, modl[0], *args0)
    xc1, h2c, lgc = _pool_layer(ctx, modc[0], *args0)
    x, xc = _moe([(x1, h2, lg, modl[0]), (xc1, h2c, lgc, modc[0])],
                 w_gu[0], b_gu[0], w_down[0], b_down[0], row(final_g), False)

    head = QK_NOPE + QK_ROPE
    q_cols = []
    for h in range(N_HEADS):
        q_cols += list(range(h * head, h * head + QK_NOPE)) + _rope_cols(h * head + QK_NOPE)
    kv_cols = list(range(KV_LORA)) + _rope_cols(KV_LORA)
    hk = QK_NOPE + V_HEAD
    ukv_cols = [h * hk + i for h in range(N_HEADS) for i in range(QK_NOPE)] + \
               [h * hk + QK_NOPE + i for h in range(N_HEADS) for i in range(V_HEAD)]
    wuq = _take_cols(w_uq[0], q_cols).astype(BF16)
    wdkv = _take_cols(w_dkv[0], kv_cols).astype(BF16)
    wukv = _take_cols(w_ukv[0], ukv_cols).astype(BF16)
    cos_t, sin_t = _rope_tables(seq)
    q, k, v = _projections(xc, x, modc[1], modl[1], row(norm1_g[1]), cos_t, sin_t,
                           w_dq[0].astype(BF16), row(q_norm_g[0]), wuq, wdkv,
                           row(kv_norm_g[0]), wukv)
    o = _attention(q, k, v)
    x1, h2, lg = _out_projection(o, x, modl[1], w_o[0].astype(BF16), row(norm2_g[1]),
                                 router_w[1], row(router_b[1]))
    (out,) = _moe([(x1, h2, lg, modl[1])], w_gu[1], b_gu[1], w_down[1], b_down[1],
                  row(final_g), True)
    return out
```

```python
import functools

import jax
import jax.numpy as jnp
from jax import lax
from jax.experimental import pallas as pl
from jax.experimental.pallas import tpu as pltpu

F32 = jnp.float32
BF16 = jnp.bfloat16

N_HEADS = 8
QK_NOPE = 128
QK_ROPE = 64
V_HEAD = 128
KV_LORA = 256
N_EXPERTS = 32
TOP_K = 4
POOL_WINDOWS = (2, 4, 8, 16)
GRID_W = 64
ROPE_BASE = 10000.0
ATTN_SCALE = (QK_NOPE + QK_ROPE) ** -0.5
SWIGLU_LIMIT = 7.0
SWIGLU_ALPHA = 1.702
EPS = 1e-6

LANES = 128
SUBLANES = 8
HEAD_PAD = 2 * LANES
VMEM_LIMIT = 56 * 1024 * 1024

SEQ_TILE = 256
ROUTE_TILE = 512
EXPERT_BLOCK = 256
Q_TILE = 512
MOD_COLS = 1536
HALO = 8


def _cparams(sem, vmem=VMEM_LIMIT):
    return pltpu.CompilerParams(dimension_semantics=sem, vmem_limit_bytes=vmem)


def _dot(a, b):
    return jnp.dot(a, b, preferred_element_type=F32)


def _dot3(a, b):
    ah = a.astype(BF16)
    al = (a - ah.astype(F32)).astype(BF16)
    bh = b.astype(BF16)
    bl = (b - bh.astype(F32)).astype(BF16)
    return _dot(ah, bh) + _dot(ah, bl) + _dot(al, bh)


def _rms(x, g):
    return x * lax.rsqrt(jnp.mean(x * x, axis=-1, keepdims=True) + EPS) * g


def _sigmoid(x):
    return 1.0 / (1.0 + jnp.exp(-x))


def _mod_kernel(c_ref, w_ref, b_ref, o_ref):
    a = c_ref[...]
    s = a * _sigmoid(a)
    o_ref[0] = _dot3(s, w_ref[0]) + b_ref[0]


def _modulation(cvec, ada_w, ada_b):
    depth, d, n = ada_w.shape
    rows = cvec.shape[0]
    return pl.pallas_call(
        _mod_kernel,
        grid=(depth, n // MOD_COLS),
        in_specs=[
            pl.BlockSpec((rows, d), lambda i, j: (0, 0)),
            pl.BlockSpec((1, d, MOD_COLS), lambda i, j: (i, 0, j)),
            pl.BlockSpec((1, 1, MOD_COLS), lambda i, j: (i, 0, j)),
        ],
        out_specs=pl.BlockSpec((1, rows, MOD_COLS), lambda i, j: (i, 0, j)),
        out_shape=jax.ShapeDtypeStruct((depth, rows, n), F32),
        compiler_params=_cparams(("arbitrary", "arbitrary")),
        name="modulation",
    )(cvec, ada_w, ada_b.reshape(depth, 1, n))


def _store_row_major(ref, val, rows):
    for c in range(val.shape[1] // LANES):
        ref[0, pl.ds(c, rows, stride=SUBLANES), :] = val[:, c * LANES:(c + 1) * LANES]


def _tail(x1, mod, n2_ref, rw_ref, rb_ref, x1_ref, h2_ref, lg_ref, rows):
    sh2, s2 = mod[3:4], mod[4:5]
    h2 = _rms(x1, n2_ref[...]) * (1.0 + s2) + sh2
    x1_ref[0] = x1
    _store_row_major(h2_ref, h2, rows)
    lg_ref[0] = _dot3(h2, rw_ref[...]) + rb_ref[...]


def _pool_kernel(x_ref, mod_ref, n1_ref, n2_ref, pw_ref, pb_ref, ps_ref, rw_ref, rb_ref,
                 x1_ref, h2_ref, lg_ref, *, rows, seq):
    j = pl.program_id(1)
    start = pl.multiple_of(j * rows, rows)
    prev0 = pl.multiple_of(jnp.maximum(start - HALO, 0), HALO)
    next0 = pl.multiple_of(jnp.minimum(start + rows, seq - HALO), HALO)
    xm = x_ref[0, pl.ds(start, rows), :]
    xe = jnp.concatenate(
        [x_ref[0, pl.ds(prev0, HALO), :], xm, x_ref[0, pl.ds(next0, HALO), :]], axis=0)
    mod = mod_ref[0]
    sh1, s1, g1 = mod[0:1], mod[1:2], mod[2:3]
    h = _rms(xe, n1_ref[...]) * (1.0 + s1) + sh1
    pos = start - HALO + lax.broadcasted_iota(jnp.int32, (rows + 2 * HALO, 1), 0)
    hz = jnp.where((pos >= 0) & (pos < seq), h, 0.0)
    tpos = start + lax.broadcasted_iota(jnp.int32, (rows, 1), 0)
    group = hz.shape[1] // len(POOL_WINDOWS)
    ys = []
    for g, w in enumerate(POOL_WINDOWS):
        half = w // 2
        hg = hz[:, g * group:(g + 1) * group]
        tot = hg[HALO - half:HALO - half + rows]
        for o in range(1 - half, half):
            tot = tot + hg[HALO + o:HALO + o + rows]
        cnt = (jnp.minimum(tpos + half, seq) - jnp.maximum(tpos - half, 0)).astype(F32)
        dlt = tot / cnt - hg[HALO:HALO + rows]
        ys.append(_dot(dlt.astype(BF16), pw_ref[g]))
    y = (jnp.concatenate(ys, axis=1) + pb_ref[...]) * ps_ref[...]
    x1 = xm + g1 * y
    _tail(x1, mod, n2_ref, rw_ref, rb_ref, x1_ref, h2_ref, lg_ref, rows)


def _pool_layer(x, mod8, n1, n2, pw, pb, ps, rw, rb):
    bsz, seq, d = x.shape
    rows = min(SEQ_TILE, seq)
    grid = (bsz, seq // rows)
    const2 = lambda b, j: (0, 0)
    return pl.pallas_call(
        functools.partial(_pool_kernel, rows=rows, seq=seq),
        grid=grid,
        in_specs=[
            pl.BlockSpec((1, seq, d), lambda b, j: (b, 0, 0)),
            pl.BlockSpec((1, SUBLANES, d), lambda b, j: (b, 0, 0)),
            pl.BlockSpec((1, d), const2),
            pl.BlockSpec((1, d), const2),
            pl.BlockSpec(pw.shape, lambda b, j: (0, 0, 0)),
            pl.BlockSpec((1, d), const2),
            pl.BlockSpec((1, d), const2),
            pl.BlockSpec(rw.shape, const2),
            pl.BlockSpec((1, LANES), const2),
        ],
        out_specs=[
            pl.BlockSpec((1, rows, d), lambda b, j: (b, j, 0)),
            pl.BlockSpec((1, rows * SUBLANES, LANES), lambda b, j: (b, j, 0)),
            pl.BlockSpec((1, rows, LANES), lambda b, j: (b, j, 0)),
        ],
        out_shape=[
            jax.ShapeDtypeStruct((bsz, seq, d), F32),
            jax.ShapeDtypeStruct((bsz, seq * SUBLANES, LANES), F32),
            jax.ShapeDtypeStruct((bsz, seq, LANES), F32),
        ],
        compiler_params=_cparams(("arbitrary", "arbitrary")),
        name="pool_layer",
    )(x, mod8, n1, n2, pw, pb, ps, rw, rb)


def _route_kernel(lg_ref, slot_ref, gt_ref, ends_ref, carry_ref, starts_ref, *, rows):
    p = pl.program_id(0)
    i = pl.program_id(1)
    work = lg_ref[...]
    lane = lax.broadcasted_iota(jnp.int32, work.shape, 1).astype(F32)
    tops, hots = [], []
    for _ in range(TOP_K):
        m = jnp.max(work, axis=1, keepdims=True)
        idx = jnp.min(jnp.where(work == m, lane, float(LANES)), axis=1, keepdims=True)
        hot = lane == idx
        work = jnp.where(hot, -jnp.inf, work)
        tops.append(m)
        hots.append(hot)
    chosen = jnp.zeros(work.shape, F32)
    for hot in hots:
        chosen = chosen + jnp.where(hot, 1.0, 0.0)
    picked = jnp.broadcast_to(jnp.sum(chosen, axis=0, keepdims=True), carry_ref.shape)

    @pl.when((p == 0) & (i == 0))
    def _():
        carry_ref[...] = jnp.zeros_like(carry_ref)

    @pl.when(p == 0)
    def _():
        carry_ref[...] = carry_ref[...] + picked

    @pl.when((p == 1) & (i == 0))
    def _():
        blocks = jnp.floor((carry_ref[...] + (EXPERT_BLOCK - 1.0)) * (1.0 / EXPERT_BLOCK))
        incl = blocks
        shift = 1
        while shift < N_EXPERTS:
            incl = incl + pltpu.roll(incl, shift, axis=1)
            shift *= 2
        ends = incl * EXPERT_BLOCK
        starts_ref[...] = ends - blocks * EXPERT_BLOCK
        ends_ref[...] = ends[0:1].astype(jnp.int32)
        carry_ref[...] = jnp.zeros_like(carry_ref)

    @pl.when(p == 1)
    def _():
        r = lax.broadcasted_iota(jnp.int32, (rows, rows), 0)
        c = lax.broadcasted_iota(jnp.int32, (rows, rows), 1)
        earlier = jnp.where(c < r, 1.0, 0.0).astype(BF16)
        base = carry_ref[0:1] + starts_ref[0:1]
        before = _dot(earlier, chosen.astype(BF16)) + base
        exps = [jnp.exp(t - tops[0]) for t in tops]
        den = exps[0] + exps[1] + exps[2] + exps[3]
        col = lax.broadcasted_iota(jnp.int32, (rows, TOP_K), 1)
        gt = jnp.zeros((rows, TOP_K), F32)
        sl = jnp.zeros((rows, TOP_K), F32)
        for k in range(TOP_K):
            slot_k = jnp.sum(jnp.where(hots[k], before, 0.0), axis=1, keepdims=True)
            gt = jnp.where(col == k, exps[k] / den, gt)
            sl = jnp.where(col == k, slot_k, sl)
        slot_ref[...] = sl.astype(jnp.int32)
        gt_ref[...] = gt
        carry_ref[...] = carry_ref[...] + picked


def _route(logits):
    t = logits.shape[0]
    rows = min(ROUTE_TILE, t)
    tok = pl.BlockSpec((rows, TOP_K), lambda p, i: (i * p, 0))
    return pl.pallas_call(
        functools.partial(_route_kernel, rows=rows),
        grid=(2, t // rows),
        in_specs=[pl.BlockSpec((rows, LANES), lambda p, i: (i, 0))],
        out_specs=[tok, tok, pl.BlockSpec((1, LANES), lambda p, i: (0, 0))],
        out_shape=[
            jax.ShapeDtypeStruct((t, TOP_K), jnp.int32),
            jax.ShapeDtypeStruct((t, TOP_K), F32),
            jax.ShapeDtypeStruct((1, LANES), jnp.int32),
        ],
        scratch_shapes=[pltpu.VMEM((SUBLANES, LANES), F32), pltpu.VMEM((SUBLANES, LANES), F32)],
        compiler_params=_cparams(("arbitrary", "arbitrary")),
        name="route",
    )(logits)


def _dispatch_kernel(ends_ref, nu_ref, slot_ref, *refs, rows, tiles, n_blocks):
    n_src = len(tiles)
    srcs, xs_ref, zeros, sem, zsem = refs[:n_src], refs[n_src], *refs[n_src + 1:]
    i = pl.program_id(0)
    blk = EXPERT_BLOCK

    @pl.when(i == 0)
    def _():
        zeros[...] = jnp.zeros_like(zeros)

        def fill(b):
            return pltpu.make_async_copy(zeros, xs_ref.at[pl.ds(b * blk, blk)], zsem)

        def expert_tails(go):
            for e in range(N_EXPERTS):
                lo = ends_ref[e - 1] if e else 0

                @pl.when(ends_ref[e] > lo)
                def _(e=e):
                    go(fill(ends_ref[e] // blk - 1))

        def unused(go):
            def body(b, carry):
                go(fill(b))
                return carry
            lax.fori_loop(nu_ref[0], n_blocks, body, 0)

        expert_tails(lambda cp: cp.start())
        unused(lambda cp: cp.start())
        expert_tails(lambda cp: cp.wait())
        unused(lambda cp: cp.wait())

    def copies(src, t):
        row = src.at[pl.ds(pl.multiple_of(t * SUBLANES, SUBLANES), SUBLANES)]
        return [pltpu.make_async_copy(row, xs_ref.at[slot_ref[t * TOP_K + k]], sem)
                for k in range(TOP_K)]

    first = 0
    for src, n in zip(srcs, tiles):
        @pl.when((i >= first) & (i < first + n))
        def _(src=src):
            def issue(t, carry):
                for cp in copies(src, t):
                    cp.start()
                return carry

            def drain(t, carry):
                for cp in copies(src, t):
                    cp.wait()
                return carry

            lax.fori_loop(0, rows, issue, 0)
            lax.fori_loop(0, rows, drain, 0)
        first += n


def _dispatch(ends, n_used, slot_flat, sources, n_blocks):
    rows = SEQ_TILE
    tiles = tuple(s.shape[0] // (rows * SUBLANES) for s in sources)
    firsts = [sum(tiles[:k]) for k in range(len(tiles))]

    def src_spec(first, n):
        return pl.BlockSpec((rows * SUBLANES, LANES),
                            lambda i, *_: (jnp.clip(i - first, 0, n - 1), 0))

    grid_spec = pltpu.PrefetchScalarGridSpec(
        num_scalar_prefetch=2,
        grid=(sum(tiles),),
        in_specs=[pl.BlockSpec((rows * TOP_K,), lambda i, *_: (i,), memory_space=pltpu.SMEM)]
        + [src_spec(f, n) for f, n in zip(firsts, tiles)],
        out_specs=pl.BlockSpec(memory_space=pl.ANY),
        scratch_shapes=[pltpu.VMEM((EXPERT_BLOCK, SUBLANES, LANES), F32),
                        pltpu.SemaphoreType.DMA, pltpu.SemaphoreType.DMA],
    )
    return pl.pallas_call(
        functools.partial(_dispatch_kernel, rows=rows, tiles=tiles, n_blocks=n_blocks),
        grid_spec=grid_spec,
        out_shape=jax.ShapeDtypeStruct((n_blocks * EXPERT_BLOCK, SUBLANES, LANES), F32),
        compiler_params=_cparams(("arbitrary",)),
        name="dispatch",
    )(ends, n_used, slot_flat, *sources)


def _experts_kernel(be_ref, nu_ref, xs_ref, wgu_ref, bgu_ref, wd_ref, bd_ref, ys_ref,
                    wgu_bf, wd_bf, *, rows, dff):
    b = pl.program_id(0)
    prev = be_ref[jnp.maximum(b - 1, 0)]
    fresh = (b == 0) | (be_ref[b] != prev)
    live = b < nu_ref[0]

    @pl.when(live & fresh)
    def _():
        wgu_bf[...] = wgu_ref[0, 0].astype(BF16)
        wd_bf[...] = wd_ref[0, 0].astype(BF16)

    @pl.when(live)
    def _():
        chunks = [xs_ref[pl.ds(c, rows, stride=SUBLANES), :] for c in range(SUBLANES)]
        x = jnp.concatenate(chunks, axis=1).astype(BF16)
        gu = _dot(x, wgu_bf[...]) + bgu_ref[0, 0]
        gate = jnp.minimum(gu[:, :dff], SWIGLU_LIMIT)
        up = jnp.clip(gu[:, dff:], -SWIGLU_LIMIT, SWIGLU_LIMIT)
        act = (up + 1.0) * (gate * _sigmoid(SWIGLU_ALPHA * gate))
        y = _dot(act.astype(BF16), wd_bf[...]) + bd_ref[0, 0]
        for c in range(SUBLANES):
            ys_ref[pl.ds(c, rows, stride=SUBLANES), :] = y[:, c * LANES:(c + 1) * LANES]

    @pl.when(jnp.logical_not(live))
    def _():
        ys_ref[...] = jnp.zeros_like(ys_ref)


def _experts(layer, block_expert, n_used, xs, w_gu, b_gu, w_down, b_down):
    n_slots = xs.shape[0]
    rows = EXPERT_BLOCK
    n_blocks = n_slots // rows
    depth, e, d, dff2 = w_gu.shape
    dff = dff2 // 2
    blk = lambda b, be, nu: (jnp.minimum(b, nu[0] - 1), 0)
    per_expert = lambda b, be, nu: (layer, be[b], 0, 0)
    grid_spec = pltpu.PrefetchScalarGridSpec(
        num_scalar_prefetch=2,
        grid=(n_blocks,),
        in_specs=[
            pl.BlockSpec((rows * SUBLANES, LANES), blk),
            pl.BlockSpec((1, 1, d, dff2), per_expert),
            pl.BlockSpec((1, 1, 1, dff2), per_expert),
            pl.BlockSpec((1, 1, dff, d), per_expert),
            pl.BlockSpec((1, 1, 1, d), per_expert),
        ],
        out_specs=pl.BlockSpec((rows * SUBLANES, LANES), lambda b, be, nu: (b, 0)),
        scratch_shapes=[pltpu.VMEM((d, dff2), BF16), pltpu.VMEM((dff, d), BF16)],
    )
    ys = pl.pallas_call(
        functools.partial(_experts_kernel, rows=rows, dff=dff),
        grid_spec=grid_spec,
        out_shape=jax.ShapeDtypeStruct((n_slots * SUBLANES, LANES), F32),
        compiler_params=_cparams(("arbitrary",)),
        name="experts",
    )(block_expert, n_used, xs.reshape(n_slots * SUBLANES, LANES), w_gu,
      b_gu.reshape(depth, e, 1, dff2), w_down, b_down.reshape(depth, e, 1, d))
    return ys.reshape(n_slots, SUBLANES, LANES)


def _combine_kernel(slot_ref, ys_ref, gt_ref, x1_ref, mod_ref, fg_ref, out_ref, buf, sem,
                    *, rows, final):
    def copies(t):
        return [pltpu.make_async_copy(ys_ref.at[slot_ref[t * TOP_K + k]],
                                      buf.at[k, pl.ds(t * SUBLANES, SUBLANES)], sem)
                for k in range(TOP_K)]

    def issue(t, carry):
        for cp in copies(t):
            cp.start()
        return carry

    def drain(t, carry):
        for cp in copies(t):
            cp.wait()
        return carry

    lax.fori_loop(0, rows, issue, 0)
    lax.fori_loop(0, rows, drain, 0)

    g2 = mod_ref[0][5:6]
    gates = gt_ref[...]
    x1 = x1_ref[0]
    outs = []
    for c in range(SUBLANES):
        acc = gates[:, 0:1] * buf[0, pl.ds(c, rows, stride=SUBLANES), :]
        for k in range(1, TOP_K):
            acc = acc + gates[:, k:k + 1] * buf[k, pl.ds(c, rows, stride=SUBLANES), :]
        sl = slice(c * LANES, (c + 1) * LANES)
        outs.append(x1[:, sl] + g2[:, sl] * acc)
    x2 = jnp.concatenate(outs, axis=1)
    if final:
        x2 = _rms(x2, fg_ref[...])
    out_ref[0] = x2


def _combine(slot_flat, gates, ys, x1, mod8, final_g, final):
    bsz, seq, d = x1.shape
    rows = min(SEQ_TILE, seq)
    per = seq // rows
    return pl.pallas_call(
        functools.partial(_combine_kernel, rows=rows, final=final),
        grid=(bsz, per),
        in_specs=[
            pl.BlockSpec((rows * TOP_K,), lambda b, j: (b * per + j,), memory_space=pltpu.SMEM),
            pl.BlockSpec(memory_space=pl.ANY),
            pl.BlockSpec((rows, TOP_K), lambda b, j: (b * per + j, 0)),
            pl.BlockSpec((1, rows, d), lambda b, j: (b, j, 0)),
            pl.BlockSpec((1, SUBLANES, d), lambda b, j: (b, 0, 0)),
            pl.BlockSpec((1, d), lambda b, j: (0, 0)),
        ],
        out_specs=pl.BlockSpec((1, rows, d), lambda b, j: (b, j, 0)),
        out_shape=jax.ShapeDtypeStruct((bsz, seq, d), F32),
        scratch_shapes=[pltpu.VMEM((TOP_K, rows * SUBLANES, LANES), F32), pltpu.SemaphoreType.DMA],
        compiler_params=_cparams(("arbitrary", "arbitrary")),
        name="combine",
    )(slot_flat, ys, gates, x1, mod8, final_g)


def _moe(layer, streams, w_gu, b_gu, w_down, b_down, final_g, final):
    logits = jnp.concatenate([s[2].reshape(-1, LANES) for s in streams], axis=0)
    t = logits.shape[0]
    slot, gates, ends = _route(logits)
    ends = ends[0, :N_EXPERTS]
    blk = EXPERT_BLOCK
    n_blocks = -(-(t * TOP_K) // blk) + N_EXPERTS
    n_used = ends[N_EXPERTS - 1] // blk
    block_start = jnp.arange(n_blocks, dtype=jnp.int32) * blk
    bexp = jnp.sum((ends[None, :] <= block_start[:, None]).astype(jnp.int32), axis=1)
    last = jnp.sum((ends <= (n_used - 1) * blk).astype(jnp.int32))
    bexp = jnp.where(jnp.arange(n_blocks) < n_used, bexp, last)
    slot_flat = slot.reshape(-1)
    sources = [s[1].reshape(-1, LANES) for s in streams]
    n_used = n_used.reshape(1)
    xs = _dispatch(ends, n_used, slot_flat, sources, n_blocks)
    ys = _experts(layer, bexp, n_used, xs, w_gu, b_gu, w_down, b_down)
    outs, off = [], 0
    for x1, _, _, mod8 in streams:
        n = x1.shape[0] * x1.shape[1]
        outs.append(_combine(slot_flat[off * TOP_K:(off + n) * TOP_K], gates[off:off + n], ys,
                             x1, mod8, final_g, final))
        off += n
    return outs


def _rope(x, cos, sin):
    return x * cos + pltpu.roll(x, LANES // 2, axis=1) * sin


def _proj_kernel(xc_ref, x_ref, modc_ref, modl_ref, n1_ref, cos_ref, sin_ref,
                 wdq_ref, qg_ref, wuq_ref, wdkv_ref, kvg_ref, wukv_ref,
                 q_ref, k_ref, v_ref, *, ctx_tiles):
    j = pl.program_id(1)
    is_ctx = j < ctx_tiles
    xin = jnp.where(is_ctx, xc_ref[0], x_ref[0])
    mod = jnp.where(is_ctx, modc_ref[0], modl_ref[0])
    sh1, s1 = mod[0:1], mod[1:2]
    hb = (_rms(xin, n1_ref[...]) * (1.0 + s1) + sh1).astype(BF16)
    cos = jnp.where(is_ctx, 1.0, cos_ref[...])
    sin = jnp.where(is_ctx, 0.0, sin_ref[...])

    kva = _dot(hb, wdkv_ref[...])
    kpe = _rope(kva[:, KV_LORA:], cos, sin).astype(BF16)
    kv = _dot(_rms(kva[:, :KV_LORA], kvg_ref[...]).astype(BF16), wukv_ref[...])
    nope_all = N_HEADS * QK_NOPE
    for h in range(N_HEADS):
        k_ref[0, :, h * HEAD_PAD:h * HEAD_PAD + QK_NOPE] = \
            kv[:, h * QK_NOPE:(h + 1) * QK_NOPE].astype(BF16)
        k_ref[0, :, h * HEAD_PAD + QK_NOPE:(h + 1) * HEAD_PAD] = kpe
    v_ref[0] = kv[:, nope_all:].astype(BF16)

    @pl.when(jnp.logical_not(is_ctx))
    def _():
        qa = _dot(hb, wdq_ref[...])
        q = _dot(_rms(qa, qg_ref[...]).astype(BF16), wuq_ref[...])
        for h in range(N_HEADS):
            lo = h * HEAD_PAD
            q_ref[0, :, lo:lo + QK_NOPE] = q[:, lo:lo + QK_NOPE].astype(BF16)
            q_ref[0, :, lo + QK_NOPE:lo + HEAD_PAD] = \
                _rope(q[:, lo + QK_NOPE:lo + HEAD_PAD], cos, sin).astype(BF16)


def _projections(xc, x, modc8, modl8, n1, cos_t, sin_t, wdq, qg, wuq, wdkv, kvg, wukv):
    bsz, seq, d = x.shape
    ctx = xc.shape[1]
    rows = SEQ_TILE
    ctx_tiles = ctx // rows
    lat = lambda b, j: (b, jnp.maximum(j - ctx_tiles, 0), 0)
    full = lambda a: pl.BlockSpec(a.shape, lambda b, j: (0,) * a.ndim)
    kw = N_HEADS * HEAD_PAD
    vw = N_HEADS * V_HEAD
    return pl.pallas_call(
        functools.partial(_proj_kernel, ctx_tiles=ctx_tiles),
        grid=(bsz, (ctx + seq) // rows),
        in_specs=[
            pl.BlockSpec((1, rows, d), lambda b, j: (b, jnp.minimum(j, ctx_tiles - 1), 0)),
            pl.BlockSpec((1, rows, d), lat),
            pl.BlockSpec((1, SUBLANES, d), lambda b, j: (b, 0, 0)),
            pl.BlockSpec((1, SUBLANES, d), lambda b, j: (b, 0, 0)),
            full(n1),
            pl.BlockSpec((rows, LANES), lambda b, j: (jnp.maximum(j - ctx_tiles, 0), 0)),
            pl.BlockSpec((rows, LANES), lambda b, j: (jnp.maximum(j - ctx_tiles, 0), 0)),
            full(wdq), full(qg), full(wuq), full(wdkv), full(kvg), full(wukv),
        ],
        out_specs=[
            pl.BlockSpec((1, rows, kw), lat),
            pl.BlockSpec((1, rows, kw), lambda b, j: (b, j, 0)),
            pl.BlockSpec((1, rows, vw), lambda b, j: (b, j, 0)),
        ],
        out_shape=[
            jax.ShapeDtypeStruct((bsz, seq, kw), BF16),
            jax.ShapeDtypeStruct((bsz, ctx + seq, kw), BF16),
            jax.ShapeDtypeStruct((bsz, ctx + seq, vw), BF16),
        ],
        compiler_params=_cparams(("arbitrary", "arbitrary")),
        name="projections",
    )(xc, x, modc8, modl8, n1, cos_t, sin_t, wdq, qg, wuq, wdkv, kvg, wukv)


def _attn_kernel(q_ref, k_ref, v_ref, o_ref):
    s = lax.dot_general(q_ref[0], k_ref[0], (((1,), (1,)), ((), ())),
                        preferred_element_type=F32) * ATTN_SCALE
    m = jnp.max(s, axis=-1, keepdims=True)
    p = jnp.exp(s - m)
    den = jnp.sum(p, axis=-1, keepdims=True)
    o_ref[0] = (_dot(p.astype(BF16), v_ref[0]) / den).astype(BF16)


def _attention(q, k, v):
    bsz, seq, _ = q.shape
    keys = k.shape[1]
    rows = min(Q_TILE, seq)
    return pl.pallas_call(
        _attn_kernel,
        grid=(bsz, N_HEADS, seq // rows),
        in_specs=[
            pl.BlockSpec((1, rows, HEAD_PAD), lambda b, h, i: (b, i, h)),
            pl.BlockSpec((1, keys, HEAD_PAD), lambda b, h, i: (b, 0, h)),
            pl.BlockSpec((1, keys, V_HEAD), lambda b, h, i: (b, 0, h)),
        ],
        out_specs=pl.BlockSpec((1, rows, V_HEAD), lambda b, h, i: (b, i, h)),
        out_shape=jax.ShapeDtypeStruct((bsz, seq, N_HEADS * V_HEAD), BF16),
        compiler_params=_cparams(("arbitrary", "arbitrary", "arbitrary")),
        name="attention",
    )(q, k, v)


def _oproj_kernel(o_ref, x_ref, mod_ref, wo_ref, n2_ref, rw_ref, rb_ref,
                  x1_ref, h2_ref, lg_ref, *, rows):
    mod = mod_ref[0]
    x1 = x_ref[0] + mod[2:3] * _dot(o_ref[0], wo_ref[...])
    _tail(x1, mod, n2_ref, rw_ref, rb_ref, x1_ref, h2_ref, lg_ref, rows)


def _out_projection(o, x, mod8, wo, n2, rw, rb):
    bsz, seq, d = x.shape
    rows = SEQ_TILE
    const2 = lambda b, j: (0, 0)
    return pl.pallas_call(
        functools.partial(_oproj_kernel, rows=rows),
        grid=(bsz, seq // rows),
        in_specs=[
            pl.BlockSpec((1, rows, o.shape[2]), lambda b, j: (b, j, 0)),
            pl.BlockSpec((1, rows, d), lambda b, j: (b, j, 0)),
            pl.BlockSpec((1, SUBLANES, d), lambda b, j: (b, 0, 0)),
            pl.BlockSpec(wo.shape, const2),
            pl.BlockSpec((1, d), const2),
            pl.BlockSpec(rw.shape, const2),
            pl.BlockSpec((1, LANES), const2),
        ],
        out_specs=[
            pl.BlockSpec((1, rows, d), lambda b, j: (b, j, 0)),
            pl.BlockSpec((1, rows * SUBLANES, LANES), lambda b, j: (b, j, 0)),
            pl.BlockSpec((1, rows, LANES), lambda b, j: (b, j, 0)),
        ],
        out_shape=[
            jax.ShapeDtypeStruct((bsz, seq, d), F32),
            jax.ShapeDtypeStruct((bsz, seq * SUBLANES, LANES), F32),
            jax.ShapeDtypeStruct((bsz, seq, LANES), F32),
        ],
        compiler_params=_cparams(("arbitrary", "arbitrary")),
        name="out_projection",
    )(o, x, mod8, wo, n2, rw, rb)


def _rope_cols(base):
    q = QK_ROPE // 4
    x1 = list(range(base, base + q)) + list(range(base + 2 * q, base + 3 * q))
    x2 = list(range(base + q, base + 2 * q)) + list(range(base + 3 * q, base + 4 * q))
    pad = [-1] * (LANES // 2 - 2 * q)
    return x1 + pad + x2 + pad


def _take_cols(w, cols):
    wz = jnp.concatenate([w, jnp.zeros((w.shape[0], 1), w.dtype)], axis=1)
    idx = jnp.asarray([c if c >= 0 else w.shape[1] for c in cols], jnp.int32)
    return jnp.take(wz, idx, axis=1)


def _rope_tables(seq):
    q = QK_ROPE // 4
    pos = jnp.arange(seq)
    inv = ROPE_BASE ** (-jnp.arange(0, QK_ROPE // 2, 2, dtype=F32) / (QK_ROPE // 2))
    ang = jnp.concatenate([(pos // GRID_W).astype(F32)[:, None] * inv,
                           (pos % GRID_W).astype(F32)[:, None] * inv], axis=1)
    pad1 = jnp.ones((seq, LANES // 2 - 2 * q), F32)
    pad0 = jnp.zeros((seq, LANES // 2 - 2 * q), F32)
    cos = jnp.concatenate([jnp.cos(ang), pad1, jnp.cos(ang), pad1], axis=1)
    sin = jnp.concatenate([-jnp.sin(ang), pad0, jnp.sin(ang), pad0], axis=1)
    return cos, sin


def _mod8(mod_rows):
    bsz, n = mod_rows.shape
    d = n // 6
    m = mod_rows.reshape(bsz, 6, d)
    return jnp.concatenate([m, jnp.zeros((bsz, SUBLANES - 6, d), F32)], axis=1)


def kernel(x, c, ctx, c_ctx, ada_w, ada_b, norm1_g, norm2_g, pool_w, pool_b, pool_scale,
           w_dq, q_norm_g, w_uq, w_dkv, kv_norm_g, w_ukv, w_o,
           router_w, router_b, w_gu, b_gu, w_down, b_down, final_g):
    bsz, seq, d = x.shape
    assert seq % SEQ_TILE == 0 and ctx.shape[1] % SEQ_TILE == 0 and seq % GRID_W == 0
    row = lambda a: a.reshape(1, -1)

    mod_rows = 2 * SUBLANES
    cvec = jnp.concatenate([c, c_ctx[None], jnp.zeros((mod_rows - bsz - 1, d), F32)], axis=0)
    mod = _modulation(cvec, ada_w, ada_b)
    modl = [_mod8(mod[i, :bsz]) for i in range(2)]
    modc = [_mod8(jnp.broadcast_to(mod[i, bsz:bsz + 1], (bsz, 6 * d))) for i in range(2)]

    n_exp = router_w.shape[2]
    rw = jnp.pad(router_w, ((0, 0), (0, 0), (0, LANES - n_exp)))
    rb = jnp.pad(router_b, ((0, 0), (0, LANES - n_exp)), constant_values=-jnp.inf)

    pw = pool_w[0].astype(BF16)
    args0 = (row(norm1_g[0]), row(norm2_g[0]), pw, row(pool_b[0]), row(pool_scale[0]),
             rw[0], row(rb[0]))
    x1, h2, lg = _pool_layer(x, modl[0], *args0)
    xc1, h2c, lgc = _pool_layer(ctx, modc[0], *args0)
    x, xc = _moe(0, [(x1, h2, lg, modl[0]), (xc1, h2c, lgc, modc[0])],
                 w_gu, b_gu, w_down, b_down, row(final_g), False)

    head = QK_NOPE + QK_ROPE
    q_cols = []
    for h in range(N_HEADS):
        q_cols += list(range(h * head, h * head + QK_NOPE)) + _rope_cols(h * head + QK_NOPE)
    kv_cols = list(range(KV_LORA)) + _rope_cols(KV_LORA)
    hk = QK_NOPE + V_HEAD
    ukv_cols = [h * hk + i for h in range(N_HEADS) for i in range(QK_NOPE)] + \
               [h * hk + QK_NOPE + i for h in range(N_HEADS) for i in range(V_HEAD)]
    wuq = _take_cols(w_uq[0], q_cols).astype(BF16)
    wdkv = _take_cols(w_dkv[0], kv_cols).astype(BF16)
    wukv = _take_cols(w_ukv[0], ukv_cols).astype(BF16)
    cos_t, sin_t = _rope_tables(seq)
    q, k, v = _projections(xc, x, modc[1], modl[1], row(norm1_g[1]), cos_t, sin_t,
                           w_dq[0].astype(BF16), row(q_norm_g[0]), wuq, wdkv,
                           row(kv_norm_g[0]), wukv)
    o = _attention(q, k, v)
    x1, h2, lg = _out_projection(o, x, modl[1], w_o[0].astype(BF16), row(norm2_g[1]),
                                 rw[1], row(rb[1]))
    (out,) = _moe(1, [(x1, h2, lg, modl[1])], w_gu, b_gu, w_down, b_down, row(final_g), True)
    return out
```

```python
import functools

import jax
import jax.numpy as jnp
from jax import lax
from jax.experimental import pallas as pl
from jax.experimental.pallas import tpu as pltpu

F32 = jnp.float32
BF16 = jnp.bfloat16

N_HEADS = 8
QK_NOPE = 128
QK_ROPE = 64
V_HEAD = 128
KV_LORA = 256
N_EXPERTS = 32
TOP_K = 4
POOL_WINDOWS = (2, 4, 8, 16)
GRID_W = 64
ROPE_BASE = 10000.0
ATTN_SCALE = (QK_NOPE + QK_ROPE) ** -0.5
Q_SCALE = ATTN_SCALE * 1.4426950408889634
SWIGLU_LIMIT = 7.0
SWIGLU_ALPHA = 1.702
EPS = 1e-6

LANES = 128
SUBLANES = 8
HEAD_PAD = 2 * LANES
VMEM_LIMIT = 56 * 1024 * 1024

SEQ_TILE = 256
EXPERT_BLOCK = 256
Q_TILE = 512
KEY_CHUNK = 768
MOD_COLS = 1536
HALO = 8
SORT_ROWS = -(-(TOP_K * SEQ_TILE + N_EXPERTS * (SUBLANES - 1)) // (2 * LANES)) * (2 * LANES)


def _cparams(sem, vmem=VMEM_LIMIT):
    return pltpu.CompilerParams(dimension_semantics=sem, vmem_limit_bytes=vmem)


def _dot(a, b):
    return jnp.dot(a, b, preferred_element_type=F32)


def _dot3(a, b):
    ah = a.astype(BF16)
    al = (a - ah.astype(F32)).astype(BF16)
    bh = b.astype(BF16)
    bl = (b - bh.astype(F32)).astype(BF16)
    return _dot(ah, bh) + _dot(ah, bl) + _dot(al, bh)


def _rms(x, g):
    return x * lax.rsqrt(jnp.mean(x * x, axis=-1, keepdims=True) + EPS) * g


def _sigmoid(x):
    return 1.0 / (1.0 + jnp.exp(-x))


def _mod_kernel(c_ref, w_ref, b_ref, o_ref):
    a = c_ref[...]
    s = a * _sigmoid(a)
    o_ref[0] = _dot3(s, w_ref[0]) + b_ref[0]


def _modulation(cvec, ada_w, ada_b):
    depth, d, n = ada_w.shape
    rows = cvec.shape[0]
    return pl.pallas_call(
        _mod_kernel,
        grid=(depth, n // MOD_COLS),
        in_specs=[
            pl.BlockSpec((rows, d), lambda i, j: (0, 0)),
            pl.BlockSpec((1, d, MOD_COLS), lambda i, j: (i, 0, j)),
            pl.BlockSpec((1, 1, MOD_COLS), lambda i, j: (i, 0, j)),
        ],
        out_specs=pl.BlockSpec((1, rows, MOD_COLS), lambda i, j: (i, 0, j)),
        out_shape=jax.ShapeDtypeStruct((depth, rows, n), F32),
        compiler_params=_cparams(("arbitrary", "arbitrary")),
        name="modulation",
    )(cvec, ada_w, ada_b.reshape(depth, 1, n))


def _tail(x1, mod, n2_ref, rw_ref, rb_ref, x1_ref, h2_ref, lg_ref):
    sh2, s2 = mod[3:4], mod[4:5]
    h2 = _rms(x1, n2_ref[...]) * (1.0 + s2) + sh2
    x1_ref[0] = x1
    h2_ref[0] = h2.astype(BF16)
    lg_ref[0] = _dot3(h2, rw_ref[...]) + rb_ref[...]


def _pool_kernel(x_ref, mod_ref, n1_ref, n2_ref, pw_ref, pb_ref, ps_ref, rw_ref, rb_ref,
                 x1_ref, h2_ref, lg_ref, *, rows, seq):
    j = pl.program_id(1)
    start = pl.multiple_of(j * rows, rows)
    prev0 = pl.multiple_of(jnp.maximum(start - HALO, 0), HALO)
    next0 = pl.multiple_of(jnp.minimum(start + rows, seq - HALO), HALO)
    xm = x_ref[0, pl.ds(start, rows), :]
    xe = jnp.concatenate(
        [x_ref[0, pl.ds(prev0, HALO), :], xm, x_ref[0, pl.ds(next0, HALO), :]], axis=0)
    mod = mod_ref[0]
    sh1, s1, g1 = mod[0:1], mod[1:2], mod[2:3]
    h = _rms(xe, n1_ref[...]) * (1.0 + s1) + sh1
    pos = start - HALO + lax.broadcasted_iota(jnp.int32, (rows + 2 * HALO, 1), 0)
    hz = jnp.where((pos >= 0) & (pos < seq), h, 0.0)
    tpos = start + lax.broadcasted_iota(jnp.int32, (rows, 1), 0)
    group = hz.shape[1] // len(POOL_WINDOWS)
    ys = []
    for g, w in enumerate(POOL_WINDOWS):
        half = w // 2
        hg = hz[:, g * group:(g + 1) * group]
        tot = hg[HALO - half:HALO - half + rows]
        for o in range(1 - half, half):
            tot = tot + hg[HALO + o:HALO + o + rows]
        cnt = (jnp.minimum(tpos + half, seq) - jnp.maximum(tpos - half, 0)).astype(F32)
        dlt = tot / cnt - hg[HALO:HALO + rows]
        ys.append(_dot(dlt.astype(BF16), pw_ref[g]))
    y = (jnp.concatenate(ys, axis=1) + pb_ref[...]) * ps_ref[...]
    x1 = xm + g1 * y
    _tail(x1, mod, n2_ref, rw_ref, rb_ref, x1_ref, h2_ref, lg_ref)


def _pool_layer(x, mod8, n1, n2, pw, pb, ps, rw, rb):
    bsz, seq, d = x.shape
    rows = min(SEQ_TILE, seq)
    grid = (bsz, seq // rows)
    const2 = lambda b, j: (0, 0)
    return pl.pallas_call(
        functools.partial(_pool_kernel, rows=rows, seq=seq),
        grid=grid,
        in_specs=[
            pl.BlockSpec((1, seq, d), lambda b, j: (b, 0, 0)),
            pl.BlockSpec((1, SUBLANES, d), lambda b, j: (b, 0, 0)),
            pl.BlockSpec((1, d), const2),
            pl.BlockSpec((1, d), const2),
            pl.BlockSpec(pw.shape, lambda b, j: (0, 0, 0)),
            pl.BlockSpec((1, d), const2),
            pl.BlockSpec((1, d), const2),
            pl.BlockSpec(rw.shape, const2),
            pl.BlockSpec((1, LANES), const2),
        ],
        out_specs=[
            pl.BlockSpec((1, rows, d), lambda b, j: (b, j, 0)),
            pl.BlockSpec((1, rows, d), lambda b, j: (b, j, 0)),
            pl.BlockSpec((1, rows, LANES), lambda b, j: (b, j, 0)),
        ],
        out_shape=[
            jax.ShapeDtypeStruct((bsz, seq, d), F32),
            jax.ShapeDtypeStruct((bsz, seq, d), BF16),
            jax.ShapeDtypeStruct((bsz, seq, LANES), F32),
        ],
        compiler_params=_cparams(("arbitrary", "arbitrary")),
        name="pool_layer",
    )(x, mod8, n1, n2, pw, pb, ps, rw, rb)


def _route_kernel(lg_ref, pos_ref, post_ref, gt_ref, seg_ref, ends_ref, carry_ref, starts_ref,
                  *, rows):
    p = pl.program_id(0)
    i = pl.program_id(1)
    work = lg_ref[...]
    lane = lax.broadcasted_iota(jnp.int32, work.shape, 1).astype(F32)
    tops, hots = [], []
    for _ in range(TOP_K):
        m = jnp.max(work, axis=1, keepdims=True)
        idx = jnp.min(jnp.where(work == m, lane, float(LANES)), axis=1, keepdims=True)
        hot = lane == idx
        work = jnp.where(hot, -jnp.inf, work)
        tops.append(m)
        hots.append(hot)
    chosen = jnp.zeros(work.shape, F32)
    for hot in hots:
        chosen = chosen + jnp.where(hot, 1.0, 0.0)
    count = jnp.broadcast_to(jnp.sum(chosen, axis=0, keepdims=True), carry_ref.shape)
    seg_rows = jnp.floor((count + (SUBLANES - 1.0)) * (1.0 / SUBLANES)) * SUBLANES

    def lane_prefix(v):
        shift = 1
        while shift < N_EXPERTS:
            v = v + pltpu.roll(v, shift, axis=1)
            shift *= 2
        return v

    @pl.when((p == 0) & (i == 0))
    def _():
        carry_ref[...] = jnp.zeros_like(carry_ref)

    @pl.when(p == 0)
    def _():
        carry_ref[...] = carry_ref[...] + seg_rows

    @pl.when((p == 1) & (i == 0))
    def _():
        blocks = jnp.floor((carry_ref[...] + (EXPERT_BLOCK - 1.0)) * (1.0 / EXPERT_BLOCK))
        ends = lane_prefix(blocks) * EXPERT_BLOCK
        starts_ref[...] = ends - blocks * EXPERT_BLOCK
        ends_ref[...] = ends[0:1].astype(jnp.int32)
        carry_ref[...] = jnp.zeros_like(carry_ref)

    @pl.when(p == 1)
    def _():
        r = lax.broadcasted_iota(jnp.int32, (rows, rows), 0)
        c = lax.broadcasted_iota(jnp.int32, (rows, rows), 1)
        earlier = jnp.where(c < r, 1.0, 0.0).astype(BF16)
        local = (lane_prefix(seg_rows) - seg_rows)[0:1]
        before = _dot(earlier, chosen.astype(BF16)) + local
        exps = [jnp.exp(t - tops[0]) for t in tops]
        den = exps[0] + exps[1] + exps[2] + exps[3]
        col = lax.broadcasted_iota(jnp.int32, (rows, TOP_K), 1)
        wide = lax.broadcasted_iota(jnp.int32, (rows, LANES), 1)
        gt = jnp.zeros((rows, TOP_K), F32)
        ps = jnp.zeros((rows, TOP_K), F32)
        ps_wide = jnp.zeros((rows, LANES), F32)
        for k in range(TOP_K):
            pos_k = jnp.sum(jnp.where(hots[k], before, 0.0), axis=1, keepdims=True)
            gt = jnp.where(col == k, exps[k] / den, gt)
            ps = jnp.where(col == k, pos_k, ps)
            ps_wide = jnp.where(wide == k, pos_k, ps_wide)
        pos_ref[...] = ps.astype(jnp.int32)
        post_ref[...] = jnp.transpose(ps_wide)[0:SUBLANES].astype(jnp.int32)
        gt_ref[...] = gt
        sub = lax.broadcasted_iota(jnp.int32, carry_ref.shape, 0)
        dst = starts_ref[...] + carry_ref[...]
        seg_ref[0] = jnp.where(sub == 0, dst, jnp.where(sub == 1, seg_rows, 0.0)).astype(jnp.int32)
        carry_ref[...] = carry_ref[...] + seg_rows


def _route(logits):
    t = logits.shape[0]
    rows = SEQ_TILE
    tiles = t // rows
    tok = pl.BlockSpec((rows, TOP_K), lambda p, i: (i * p, 0))
    return pl.pallas_call(
        functools.partial(_route_kernel, rows=rows),
        grid=(2, tiles),
        in_specs=[pl.BlockSpec((rows, LANES), lambda p, i: (i, 0))],
        out_specs=[tok,
                   pl.BlockSpec((SUBLANES, rows), lambda p, i: (0, i * p)),
                   tok,
                   pl.BlockSpec((1, SUBLANES, LANES), lambda p, i: (i * p, 0, 0)),
                   pl.BlockSpec((1, LANES), lambda p, i: (0, 0))],
        out_shape=[
            jax.ShapeDtypeStruct((t, TOP_K), jnp.int32),
            jax.ShapeDtypeStruct((SUBLANES, t), jnp.int32),
            jax.ShapeDtypeStruct((t, TOP_K), F32),
            jax.ShapeDtypeStruct((tiles, SUBLANES, LANES), jnp.int32),
            jax.ShapeDtypeStruct((1, LANES), jnp.int32),
        ],
        scratch_shapes=[pltpu.VMEM((SUBLANES, LANES), F32), pltpu.VMEM((SUBLANES, LANES), F32)],
        compiler_params=_cparams(("arbitrary", "arbitrary")),
        name="route",
    )(logits)


def _segment_copies(go, dst_ref, cnt_ref, tile, make):
    def per_expert(e, local):
        n = cnt_ref[tile * N_EXPERTS + e]
        d = dst_ref[tile * N_EXPERTS + e]

        def chunk(c, carry):
            off = c * SUBLANES
            go(make(pl.multiple_of(local + off, SUBLANES), pl.multiple_of(d + off, SUBLANES)))
            return carry

        lax.fori_loop(0, n // SUBLANES, chunk, 0)
        return local + n

    lax.fori_loop(0, N_EXPERTS, per_expert, 0)


def _dispatch_kernel(dst_ref, cnt_ref, ends_ref, nu_ref, post_ref, *refs, tiles, n_blocks):
    n_src = len(tiles)
    srcs, (xs_ref, sorted_ref, zeros, sem, zsem) = refs[:n_src], refs[n_src:]
    i = pl.program_id(0)
    blk = EXPERT_BLOCK

    @pl.when(i == 0)
    def _():
        zeros[...] = jnp.zeros_like(zeros)

        def fill(b):
            return pltpu.make_async_copy(zeros, xs_ref.at[pl.ds(b * blk, blk)], zsem)

        def expert_tails(go):
            for e in range(N_EXPERTS):
                lo = ends_ref[e - 1] if e else 0

                @pl.when(ends_ref[e] > lo)
                def _(e=e):
                    go(fill(ends_ref[e] // blk - 1))

        def unused(go):
            def body(b, carry):
                go(fill(b))
                return carry
            lax.fori_loop(nu_ref[0], n_blocks, body, 0)

        expert_tails(lambda cp: cp.start())
        unused(lambda cp: cp.start())
        expert_tails(lambda cp: cp.wait())
        unused(lambda cp: cp.wait())

    h = srcs[0][...]
    first = tiles[0]
    for src, n in zip(srcs[1:], tiles[1:]):
        h = jnp.where(i >= first, src[...], h)
        first += n

    pos = post_ref[...]
    j = lax.broadcasted_iota(jnp.int32, (SORT_ROWS, h.shape[0]), 0)
    onehot = jnp.zeros(j.shape, F32)
    for k in range(TOP_K):
        onehot = jnp.where(j == pos[k:k + 1], 1.0, onehot)
    sorted_ref[...] = _dot(onehot.astype(BF16), h)

    def make(local, glob):
        return pltpu.make_async_copy(sorted_ref.at[pl.ds(local, SUBLANES)],
                                     xs_ref.at[pl.ds(glob, SUBLANES)], sem)

    _segment_copies(lambda cp: cp.start(), dst_ref, cnt_ref, i, make)
    _segment_copies(lambda cp: cp.wait(), dst_ref, cnt_ref, i, make)


def _dispatch(seg_dst, seg_cnt, ends, n_used, pos_t, sources, n_blocks):
    rows = SEQ_TILE
    d = sources[0].shape[1]
    tiles = tuple(s.shape[0] // rows for s in sources)
    firsts = [sum(tiles[:k]) for k in range(len(tiles))]

    def src_spec(first, n):
        return pl.BlockSpec((rows, d), lambda i, *_: (jnp.clip(i - first, 0, n - 1), 0))

    grid_spec = pltpu.PrefetchScalarGridSpec(
        num_scalar_prefetch=4,
        grid=(sum(tiles),),
        in_specs=[pl.BlockSpec((SUBLANES, rows), lambda i, *_: (0, i))]
        + [src_spec(f, n) for f, n in zip(firsts, tiles)],
        out_specs=pl.BlockSpec(memory_space=pl.ANY),
        scratch_shapes=[pltpu.VMEM((SORT_ROWS, d), F32), pltpu.VMEM((EXPERT_BLOCK, d), F32),
                        pltpu.SemaphoreType.DMA, pltpu.SemaphoreType.DMA],
    )
    return pl.pallas_call(
        functools.partial(_dispatch_kernel, tiles=tiles, n_blocks=n_blocks),
        grid_spec=grid_spec,
        out_shape=jax.ShapeDtypeStruct((n_blocks * EXPERT_BLOCK, d), F32),
        compiler_params=_cparams(("arbitrary",)),
        name="dispatch",
    )(seg_dst, seg_cnt, ends, n_used, pos_t, *sources)


def _experts_kernel(be_ref, nu_ref, xs_ref, wgu_ref, bgu_ref, wd_ref, bd_ref, ys_ref,
                    wgu_bf, wd_bf, *, rows, dff):
    b = pl.program_id(0)
    prev = be_ref[jnp.maximum(b - 1, 0)]
    fresh = (b == 0) | (be_ref[b] != prev)
    live = b < nu_ref[0]

    @pl.when(live & fresh)
    def _():
        wgu_bf[...] = wgu_ref[0, 0].astype(BF16)
        wd_bf[...] = wd_ref[0, 0].astype(BF16)

    @pl.when(live)
    def _():
        gu = _dot(xs_ref[...].astype(BF16), wgu_bf[...]) + bgu_ref[0, 0]
        gate = jnp.minimum(gu[:, :dff], SWIGLU_LIMIT)
        up = jnp.clip(gu[:, dff:], -SWIGLU_LIMIT, SWIGLU_LIMIT)
        act = (up + 1.0) * (gate * _sigmoid(SWIGLU_ALPHA * gate))
        ys_ref[...] = _dot(act.astype(BF16), wd_bf[...]) + bd_ref[0, 0]

    @pl.when(jnp.logical_not(live))
    def _():
        ys_ref[...] = jnp.zeros_like(ys_ref)


def _experts(layer, block_expert, n_used, xs, w_gu, b_gu, w_down, b_down):
    n_rows = xs.shape[0]
    rows = EXPERT_BLOCK
    n_blocks = n_rows // rows
    depth, e, d, dff2 = w_gu.shape
    dff = dff2 // 2
    blk = lambda b, be, nu: (jnp.minimum(b, nu[0] - 1), 0)
    per_expert = lambda b, be, nu: (layer, be[b], 0, 0)
    grid_spec = pltpu.PrefetchScalarGridSpec(
        num_scalar_prefetch=2,
        grid=(n_blocks,),
        in_specs=[
            pl.BlockSpec((rows, d), blk),
            pl.BlockSpec((1, 1, d, dff2), per_expert),
            pl.BlockSpec((1, 1, 1, dff2), per_expert),
            pl.BlockSpec((1, 1, dff, d), per_expert),
            pl.BlockSpec((1, 1, 1, d), per_expert),
        ],
        out_specs=pl.BlockSpec((rows, d), lambda b, be, nu: (b, 0)),
        scratch_shapes=[pltpu.VMEM((d, dff2), BF16), pltpu.VMEM((dff, d), BF16)],
    )
    return pl.pallas_call(
        functools.partial(_experts_kernel, rows=rows, dff=dff),
        grid_spec=grid_spec,
        out_shape=jax.ShapeDtypeStruct((n_rows, d), F32),
        compiler_params=_cparams(("arbitrary",)),
        name="experts",
    )(block_expert, n_used, xs, w_gu, b_gu.reshape(depth, e, 1, dff2), w_down,
      b_down.reshape(depth, e, 1, d))


def _combine_kernel(dst_ref, cnt_ref, ys_ref, pos_ref, gt_ref, x1_ref, mod_ref, fg_ref, out_ref,
                    sorted_ref, sem, *, per, tile0, final):
    b = pl.program_id(0)
    j = pl.program_id(1)

    @pl.when((b == 0) & (j == 0))
    def _():
        sorted_ref[...] = jnp.zeros_like(sorted_ref)

    def make(local, glob):
        return pltpu.make_async_copy(ys_ref.at[pl.ds(glob, SUBLANES)],
                                     sorted_ref.at[pl.ds(local, SUBLANES)], sem)

    tile = tile0 + b * per + j
    _segment_copies(lambda cp: cp.start(), dst_ref, cnt_ref, tile, make)
    _segment_copies(lambda cp: cp.wait(), dst_ref, cnt_ref, tile, make)

    pos = pos_ref[...]
    gates = gt_ref[...]
    lane = lax.broadcasted_iota(jnp.int32, (pos.shape[0], SORT_ROWS), 1)
    weights = jnp.zeros(lane.shape, F32)
    for k in range(TOP_K):
        weights = jnp.where(lane == pos[:, k:k + 1], gates[:, k:k + 1], weights)
    y = _dot(weights.astype(BF16), sorted_ref[...].astype(BF16))
    x2 = x1_ref[0] + mod_ref[0][5:6] * y
    if final:
        x2 = _rms(x2, fg_ref[...])
    out_ref[0] = x2


def _combine(seg_dst, seg_cnt, tile0, pos, gates, ys, x1, mod8, final_g, final):
    bsz, seq, d = x1.shape
    rows = SEQ_TILE
    per = seq // rows
    tok = pl.BlockSpec((rows, TOP_K), lambda b, j, *_: (tile0 + b * per + j, 0))
    grid_spec = pltpu.PrefetchScalarGridSpec(
        num_scalar_prefetch=2,
        grid=(bsz, per),
        in_specs=[
            pl.BlockSpec(memory_space=pl.ANY),
            tok, tok,
            pl.BlockSpec((1, rows, d), lambda b, j, *_: (b, j, 0)),
            pl.BlockSpec((1, SUBLANES, d), lambda b, j, *_: (b, 0, 0)),
            pl.BlockSpec((1, d), lambda b, j, *_: (0, 0)),
        ],
        out_specs=pl.BlockSpec((1, rows, d), lambda b, j, *_: (b, j, 0)),
        scratch_shapes=[pltpu.VMEM((SORT_ROWS, d), F32), pltpu.SemaphoreType.DMA],
    )
    return pl.pallas_call(
        functools.partial(_combine_kernel, per=per, tile0=tile0, final=final),
        grid_spec=grid_spec,
        out_shape=jax.ShapeDtypeStruct((bsz, seq, d), F32),
        compiler_params=_cparams(("arbitrary", "arbitrary")),
        name="combine",
    )(seg_dst, seg_cnt, ys, pos, gates, x1, mod8, final_g)


def _moe(layer, streams, w_gu, b_gu, w_down, b_down, final_g, final):
    logits = jnp.concatenate([s[2].reshape(-1, LANES) for s in streams], axis=0)
    t = logits.shape[0]
    d = streams[0][0].shape[2]
    pos, pos_t, gates, seg, ends = _route(logits)
    tiles = seg.shape[0]
    seg_dst = seg[:, 0, :N_EXPERTS].reshape(-1)
    seg_cnt = seg[:, 1, :N_EXPERTS].reshape(-1)
    ends = ends[0, :N_EXPERTS]
    blk = EXPERT_BLOCK
    max_rows = t * TOP_K + tiles * N_EXPERTS * (SUBLANES - 1)
    n_blocks = -(-max_rows // blk) + N_EXPERTS
    n_used = ends[N_EXPERTS - 1] // blk
    block_start = jnp.arange(n_blocks, dtype=jnp.int32) * blk
    bexp = jnp.sum((ends[None, :] <= block_start[:, None]).astype(jnp.int32), axis=1)
    last = jnp.sum((ends <= (n_used - 1) * blk).astype(jnp.int32))
    bexp = jnp.where(jnp.arange(n_blocks) < n_used, bexp, last)
    sources = [s[1].reshape(-1, d) for s in streams]
    n_used = n_used.reshape(1)
    xs = _dispatch(seg_dst, seg_cnt, ends, n_used, pos_t, sources, n_blocks)
    ys = _experts(layer, bexp, n_used, xs, w_gu, b_gu, w_down, b_down)
    outs, tile0 = [], 0
    for x1, _, _, mod8 in streams:
        outs.append(_combine(seg_dst, seg_cnt, tile0, pos, gates, ys, x1, mod8, final_g, final))
        tile0 += x1.shape[0] * x1.shape[1] // SEQ_TILE
    return outs


def _rope(x, cos, sin):
    return x * cos + pltpu.roll(x, LANES // 2, axis=1) * sin


def _proj_kernel(xc_ref, x_ref, modc_ref, modl_ref, n1_ref, cos_ref, sin_ref,
                 wdq_ref, qg_ref, wuq_ref, wdkv_ref, kvg_ref, wukv_ref,
                 q_ref, k_ref, v_ref, *, ctx_tiles):
    j = pl.program_id(1)
    is_ctx = j < ctx_tiles
    xin = jnp.where(is_ctx, xc_ref[0], x_ref[0])
    mod = jnp.where(is_ctx, modc_ref[0], modl_ref[0])
    sh1, s1 = mod[0:1], mod[1:2]
    hb = (_rms(xin, n1_ref[...]) * (1.0 + s1) + sh1).astype(BF16)
    cos = jnp.where(is_ctx, 1.0, cos_ref[...])
    sin = jnp.where(is_ctx, 0.0, sin_ref[...])

    kva = _dot(hb, wdkv_ref[...])
    kpe = _rope(kva[:, KV_LORA:], cos, sin).astype(BF16)
    kv = _dot(_rms(kva[:, :KV_LORA], kvg_ref[...]).astype(BF16), wukv_ref[...])
    nope_all = N_HEADS * QK_NOPE
    ones = jnp.ones((kv.shape[0], V_HEAD), BF16)
    for h in range(N_HEADS):
        k_ref[0, :, h * HEAD_PAD:h * HEAD_PAD + QK_NOPE] = \
            kv[:, h * QK_NOPE:(h + 1) * QK_NOPE].astype(BF16)
        k_ref[0, :, h * HEAD_PAD + QK_NOPE:(h + 1) * HEAD_PAD] = kpe
        v_ref[0, :, h * HEAD_PAD:h * HEAD_PAD + V_HEAD] = \
            kv[:, nope_all + h * V_HEAD:nope_all + (h + 1) * V_HEAD].astype(BF16)
        v_ref[0, :, h * HEAD_PAD + V_HEAD:(h + 1) * HEAD_PAD] = ones

    @pl.when(jnp.logical_not(is_ctx))
    def _():
        qa = _dot(hb, wdq_ref[...])
        q = _dot(_rms(qa, qg_ref[...]).astype(BF16), wuq_ref[...]) * Q_SCALE
        for h in range(N_HEADS):
            lo = h * HEAD_PAD
            q_ref[0, :, lo:lo + QK_NOPE] = q[:, lo:lo + QK_NOPE].astype(BF16)
            q_ref[0, :, lo + QK_NOPE:lo + HEAD_PAD] = \
                _rope(q[:, lo + QK_NOPE:lo + HEAD_PAD], cos, sin).astype(BF16)


def _projections(xc, x, modc8, modl8, n1, cos_t, sin_t, wdq, qg, wuq, wdkv, kvg, wukv):
    bsz, seq, d = x.shape
    ctx = xc.shape[1]
    rows = SEQ_TILE
    ctx_tiles = ctx // rows
    lat = lambda b, j: (b, jnp.maximum(j - ctx_tiles, 0), 0)
    full = lambda a: pl.BlockSpec(a.shape, lambda b, j: (0,) * a.ndim)
    kw = N_HEADS * HEAD_PAD
    vw = N_HEADS * HEAD_PAD
    return pl.pallas_call(
        functools.partial(_proj_kernel, ctx_tiles=ctx_tiles),
        grid=(bsz, (ctx + seq) // rows),
        in_specs=[
            pl.BlockSpec((1, rows, d), lambda b, j: (b, jnp.minimum(j, ctx_tiles - 1), 0)),
            pl.BlockSpec((1, rows, d), lat),
            pl.BlockSpec((1, SUBLANES, d), lambda b, j: (b, 0, 0)),
            pl.BlockSpec((1, SUBLANES, d), lambda b, j: (b, 0, 0)),
            full(n1),
            pl.BlockSpec((rows, LANES), lambda b, j: (jnp.maximum(j - ctx_tiles, 0), 0)),
            pl.BlockSpec((rows, LANES), lambda b, j: (jnp.maximum(j - ctx_tiles, 0), 0)),
            full(wdq), full(qg), full(wuq), full(wdkv), full(kvg), full(wukv),
        ],
        out_specs=[
            pl.BlockSpec((1, rows, kw), lat),
            pl.BlockSpec((1, rows, kw), lambda b, j: (b, j, 0)),
            pl.BlockSpec((1, rows, vw), lambda b, j: (b, j, 0)),
        ],
        out_shape=[
            jax.ShapeDtypeStruct((bsz, seq, kw), BF16),
            jax.ShapeDtypeStruct((bsz, ctx + seq, kw), BF16),
            jax.ShapeDtypeStruct((bsz, ctx + seq, vw), BF16),
        ],
        compiler_params=_cparams(("arbitrary", "arbitrary")),
        name="projections",
    )(xc, x, modc8, modl8, n1, cos_t, sin_t, wdq, qg, wuq, wdkv, kvg, wukv)


def _attn_kernel(q_ref, k_ref, v_ref, o_ref, *, chunk):
    q = q_ref[0]
    m = acc = None
    for c in range(k_ref.shape[1] // chunk):
        rows = slice(c * chunk, (c + 1) * chunk)
        s = lax.dot_general(q, k_ref[0, rows, :], (((1,), (1,)), ((), ())),
                            preferred_element_type=F32)
        m_c = jnp.max(s, axis=-1, keepdims=True)
        m_new = m_c if c == 0 else jnp.maximum(m, m_c)
        pv = _dot(jnp.exp2(s - m_new).astype(BF16), v_ref[0, rows, :])
        acc = pv if c == 0 else acc * jnp.exp2(m - m_new) + pv
        m = m_new
    o_ref[0] = (acc[:, :V_HEAD] / acc[:, V_HEAD:]).astype(BF16)


def _attention(q, k, v):
    bsz, seq, _ = q.shape
    keys = k.shape[1]
    rows = min(Q_TILE, seq)
    chunk = next(c for c in (KEY_CHUNK, 2 * LANES, keys) if keys % c == 0)
    return pl.pallas_call(
        functools.partial(_attn_kernel, chunk=chunk),
        grid=(bsz, N_HEADS, seq // rows),
        in_specs=[
            pl.BlockSpec((1, rows, HEAD_PAD), lambda b, h, i: (b, i, h)),
            pl.BlockSpec((1, keys, HEAD_PAD), lambda b, h, i: (b, 0, h)),
            pl.BlockSpec((1, keys, HEAD_PAD), lambda b, h, i: (b, 0, h)),
        ],
        out_specs=pl.BlockSpec((1, rows, V_HEAD), lambda b, h, i: (b, i, h)),
        out_shape=jax.ShapeDtypeStruct((bsz, seq, N_HEADS * V_HEAD), BF16),
        compiler_params=_cparams(("arbitrary", "arbitrary", "arbitrary")),
        name="attention",
    )(q, k, v)


def _oproj_kernel(o_ref, x_ref, mod_ref, wo_ref, n2_ref, rw_ref, rb_ref,
                  x1_ref, h2_ref, lg_ref, *, rows):
    mod = mod_ref[0]
    x1 = x_ref[0] + mod[2:3] * _dot(o_ref[0], wo_ref[...])
    _tail(x1, mod, n2_ref, rw_ref, rb_ref, x1_ref, h2_ref, lg_ref)


def _out_projection(o, x, mod8, wo, n2, rw, rb):
    bsz, seq, d = x.shape
    rows = SEQ_TILE
    const2 = lambda b, j: (0, 0)
    return pl.pallas_call(
        functools.partial(_oproj_kernel, rows=rows),
        grid=(bsz, seq // rows),
        in_specs=[
            pl.BlockSpec((1, rows, o.shape[2]), lambda b, j: (b, j, 0)),
            pl.BlockSpec((1, rows, d), lambda b, j: (b, j, 0)),
            pl.BlockSpec((1, SUBLANES, d), lambda b, j: (b, 0, 0)),
            pl.BlockSpec(wo.shape, const2),
            pl.BlockSpec((1, d), const2),
            pl.BlockSpec(rw.shape, const2),
            pl.BlockSpec((1, LANES), const2),
        ],
        out_specs=[
            pl.BlockSpec((1, rows, d), lambda b, j: (b, j, 0)),
            pl.BlockSpec((1, rows, d), lambda b, j: (b, j, 0)),
            pl.BlockSpec((1, rows, LANES), lambda b, j: (b, j, 0)),
        ],
        out_shape=[
            jax.ShapeDtypeStruct((bsz, seq, d), F32),
            jax.ShapeDtypeStruct((bsz, seq, d), BF16),
            jax.ShapeDtypeStruct((bsz, seq, LANES), F32),
        ],
        compiler_params=_cparams(("arbitrary", "arbitrary")),
        name="out_projection",
    )(o, x, mod8, wo, n2, rw, rb)


def _rope_cols(base):
    q = QK_ROPE // 4
    x1 = list(range(base, base + q)) + list(range(base + 2 * q, base + 3 * q))
    x2 = list(range(base + q, base + 2 * q)) + list(range(base + 3 * q, base + 4 * q))
    pad = [-1] * (LANES // 2 - 2 * q)
    return x1 + pad + x2 + pad


def _take_cols(w, cols):
    wz = jnp.concatenate([w, jnp.zeros((w.shape[0], 1), w.dtype)], axis=1)
    idx = jnp.asarray([c if c >= 0 else w.shape[1] for c in cols], jnp.int32)
    return jnp.take(wz, idx, axis=1)


def _rope_tables(seq):
    q = QK_ROPE // 4
    pos = jnp.arange(seq)
    inv = ROPE_BASE ** (-jnp.arange(0, QK_ROPE // 2, 2, dtype=F32) / (QK_ROPE // 2))
    ang = jnp.concatenate([(pos // GRID_W).astype(F32)[:, None] * inv,
                           (pos % GRID_W).astype(F32)[:, None] * inv], axis=1)
    pad1 = jnp.ones((seq, LANES // 2 - 2 * q), F32)
    pad0 = jnp.zeros((seq, LANES // 2 - 2 * q), F32)
    cos = jnp.concatenate([jnp.cos(ang), pad1, jnp.cos(ang), pad1], axis=1)
    sin = jnp.concatenate([-jnp.sin(ang), pad0, jnp.sin(ang), pad0], axis=1)
    return cos, sin


def _mod8(mod_rows):
    bsz, n = mod_rows.shape
    d = n // 6
    m = mod_rows.reshape(bsz, 6, d)
    return jnp.concatenate([m, jnp.zeros((bsz, SUBLANES - 6, d), F32)], axis=1)


def kernel(x, c, ctx, c_ctx, ada_w, ada_b, norm1_g, norm2_g, pool_w, pool_b, pool_scale,
           w_dq, q_norm_g, w_uq, w_dkv, kv_norm_g, w_ukv, w_o,
           router_w, router_b, w_gu, b_gu, w_down, b_down, final_g):
    bsz, seq, d = x.shape
    assert seq % SEQ_TILE == 0 and ctx.shape[1] % SEQ_TILE == 0 and seq % GRID_W == 0
    row = lambda a: a.reshape(1, -1)

    mod_rows = 2 * SUBLANES
    cvec = jnp.concatenate([c, c_ctx[None], jnp.zeros((mod_rows - bsz - 1, d), F32)], axis=0)
    mod = _modulation(cvec, ada_w, ada_b)
    modl = [_mod8(mod[i, :bsz]) for i in range(2)]
    modc = [_mod8(jnp.broadcast_to(mod[i, bsz:bsz + 1], (bsz, 6 * d))) for i in range(2)]

    n_exp = router_w.shape[2]
    rw = jnp.pad(router_w, ((0, 0), (0, 0), (0, LANES - n_exp)))
    rb = jnp.pad(router_b, ((0, 0), (0, LANES - n_exp)), constant_values=-jnp.inf)

    pw = pool_w[0].astype(BF16)
    args0 = (row(norm1_g[0]), row(norm2_g[0]), pw, row(pool_b[0]), row(pool_scale[0]),
             rw[0], row(rb[0]))
    x1, h2, lg = _pool_layer(x, modl[0], *args0)
    xc1, h2c, lgc = _pool_layer(ctx, modc[0], *args0)
    x, xc = _moe(0, [(x1, h2, lg, modl[0]), (xc1, h2c, lgc, modc[0])],
                 w_gu, b_gu, w_down, b_down, row(final_g), False)

    head = QK_NOPE + QK_ROPE
    q_cols = []
    for h in range(N_HEADS):
        q_cols += list(range(h * head, h * head + QK_NOPE)) + _rope_cols(h * head + QK_NOPE)
    kv_cols = list(range(KV_LORA)) + _rope_cols(KV_LORA)
    hk = QK_NOPE + V_HEAD
    ukv_cols = [h * hk + i for h in range(N_HEADS) for i in range(QK_NOPE)] + \
               [h * hk + QK_NOPE + i for h in range(N_HEADS) for i in range(V_HEAD)]
    wuq = _take_cols(w_uq[0], q_cols).astype(BF16)
    wdkv = _take_cols(w_dkv[0], kv_cols).astype(BF16)
    wukv = _take_cols(w_ukv[0], ukv_cols).astype(BF16)
    cos_t, sin_t = _rope_tables(seq)
    q, k, v = _projections(xc, x, modc[1], modl[1], row(norm1_g[1]), cos_t, sin_t,
                           w_dq[0].astype(BF16), row(q_norm_g[0]), wuq, wdkv,
                           row(kv_norm_g[0]), wukv)
    o = _attention(q, k, v)
    x1, h2, lg = _out_projection(o, x, modl[1], w_o[0].astype(BF16), row(norm2_g[1]),
                                 rw[1], row(rb[1]))
    (out,) = _moe(1, [(x1, h2, lg, modl[1])], w_gu, b_gu, w_down, b_down, row(final_g), True)
    return out
```

```python
import functools

import jax
import jax.numpy as jnp
from jax import lax
from jax.experimental import pallas as pl
from jax.experimental.pallas import tpu as pltpu

F32 = jnp.float32
BF16 = jnp.bfloat16

N_HEADS = 8
QK_NOPE = 128
QK_ROPE = 64
V_HEAD = 128
KV_LORA = 256
N_EXPERTS = 32
TOP_K = 4
POOL_WINDOWS = (2, 4, 8, 16)
GRID_W = 64
ROPE_BASE = 10000.0
ATTN_SCALE = (QK_NOPE + QK_ROPE) ** -0.5
Q_SCALE = ATTN_SCALE * 1.4426950408889634
SWIGLU_LIMIT = 7.0
SWIGLU_ALPHA = 1.702
EPS = 1e-6

LANES = 128
SUBLANES = 8
HEAD_PAD = 2 * LANES
VMEM_LIMIT = 56 * 1024 * 1024

SEQ_TILE = 256
EXPERT_BLOCK = 512
Q_TILE = 512
KEY_CHUNK = 768
MOD_COLS = 1536
HALO = 8
SORT_ROWS = -(-(TOP_K * SEQ_TILE + N_EXPERTS * (SUBLANES - 1)) // (2 * LANES)) * (2 * LANES)


def _cparams(sem, vmem=VMEM_LIMIT):
    return pltpu.CompilerParams(dimension_semantics=sem, vmem_limit_bytes=vmem)


def _dot(a, b):
    return jnp.dot(a, b, preferred_element_type=F32)


def _dot3(a, b):
    ah = a.astype(BF16)
    al = (a - ah.astype(F32)).astype(BF16)
    bh = b.astype(BF16)
    bl = (b - bh.astype(F32)).astype(BF16)
    return _dot(ah, bh) + _dot(ah, bl) + _dot(al, bh)


def _rms(x, g):
    return x * lax.rsqrt(jnp.mean(x * x, axis=-1, keepdims=True) + EPS) * g


def _sigmoid(x):
    return 1.0 / (1.0 + jnp.exp(-x))


def _mod_kernel(c_ref, w_ref, b_ref, o_ref):
    a = c_ref[...]
    s = a * _sigmoid(a)
    o_ref[0] = _dot3(s, w_ref[0]) + b_ref[0]


def _modulation(cvec, ada_w, ada_b):
    depth, d, n = ada_w.shape
    rows = cvec.shape[0]
    return pl.pallas_call(
        _mod_kernel,
        grid=(depth, n // MOD_COLS),
        in_specs=[
            pl.BlockSpec((rows, d), lambda i, j: (0, 0)),
            pl.BlockSpec((1, d, MOD_COLS), lambda i, j: (i, 0, j)),
            pl.BlockSpec((1, 1, MOD_COLS), lambda i, j: (i, 0, j)),
        ],
        out_specs=pl.BlockSpec((1, rows, MOD_COLS), lambda i, j: (i, 0, j)),
        out_shape=jax.ShapeDtypeStruct((depth, rows, n), F32),
        compiler_params=_cparams(("arbitrary", "arbitrary")),
        name="modulation",
    )(cvec, ada_w, ada_b.reshape(depth, 1, n))


def _tail(x1, mod, n2_ref, rw_ref, rb_ref, x1_ref, h2_ref, lg_ref):
    sh2, s2 = mod[3:4], mod[4:5]
    h2 = _rms(x1, n2_ref[...]) * (1.0 + s2) + sh2
    x1_ref[0] = x1
    h2_ref[0] = h2.astype(BF16)
    lg_ref[0] = _dot3(h2, rw_ref[...]) + rb_ref[...]


def _pool_kernel(x_ref, mod_ref, n1_ref, n2_ref, pw_ref, pb_ref, ps_ref, rw_ref, rb_ref,
                 x1_ref, h2_ref, lg_ref, *, rows, seq):
    j = pl.program_id(1)
    start = pl.multiple_of(j * rows, rows)
    prev0 = pl.multiple_of(jnp.maximum(start - HALO, 0), HALO)
    next0 = pl.multiple_of(jnp.minimum(start + rows, seq - HALO), HALO)
    xm = x_ref[0, pl.ds(start, rows), :]
    xe = jnp.concatenate(
        [x_ref[0, pl.ds(prev0, HALO), :], xm, x_ref[0, pl.ds(next0, HALO), :]], axis=0)
    mod = mod_ref[0]
    sh1, s1, g1 = mod[0:1], mod[1:2], mod[2:3]
    h = _rms(xe, n1_ref[...]) * (1.0 + s1) + sh1
    pos = start - HALO + lax.broadcasted_iota(jnp.int32, (rows + 2 * HALO, 1), 0)
    hz = jnp.where((pos >= 0) & (pos < seq), h, 0.0)
    tpos = start + lax.broadcasted_iota(jnp.int32, (rows, 1), 0)
    group = hz.shape[1] // len(POOL_WINDOWS)
    ys = []
    for g, w in enumerate(POOL_WINDOWS):
        half = w // 2
        hg = hz[:, g * group:(g + 1) * group]
        tot = hg[HALO - half:HALO - half + rows]
        for o in range(1 - half, half):
            tot = tot + hg[HALO + o:HALO + o + rows]
        cnt = (jnp.minimum(tpos + half, seq) - jnp.maximum(tpos - half, 0)).astype(F32)
        dlt = tot / cnt - hg[HALO:HALO + rows]
        ys.append(_dot(dlt.astype(BF16), pw_ref[g]))
    y = (jnp.concatenate(ys, axis=1) + pb_ref[...]) * ps_ref[...]
    x1 = xm + g1 * y
    _tail(x1, mod, n2_ref, rw_ref, rb_ref, x1_ref, h2_ref, lg_ref)


def _pool_layer(x, mod8, n1, n2, pw, pb, ps, rw, rb):
    bsz, seq, d = x.shape
    rows = min(SEQ_TILE, seq)
    grid = (bsz, seq // rows)
    const2 = lambda b, j: (0, 0)
    return pl.pallas_call(
        functools.partial(_pool_kernel, rows=rows, seq=seq),
        grid=grid,
        in_specs=[
            pl.BlockSpec((1, seq, d), lambda b, j: (b, 0, 0)),
            pl.BlockSpec((1, SUBLANES, d), lambda b, j: (b, 0, 0)),
            pl.BlockSpec((1, d), const2),
            pl.BlockSpec((1, d), const2),
            pl.BlockSpec(pw.shape, lambda b, j: (0, 0, 0)),
            pl.BlockSpec((1, d), const2),
            pl.BlockSpec((1, d), const2),
            pl.BlockSpec(rw.shape, const2),
            pl.BlockSpec((1, LANES), const2),
        ],
        out_specs=[
            pl.BlockSpec((1, rows, d), lambda b, j: (b, j, 0)),
            pl.BlockSpec((1, rows, d), lambda b, j: (b, j, 0)),
            pl.BlockSpec((1, rows, LANES), lambda b, j: (b, j, 0)),
        ],
        out_shape=[
            jax.ShapeDtypeStruct((bsz, seq, d), F32),
            jax.ShapeDtypeStruct((bsz, seq, d), BF16),
            jax.ShapeDtypeStruct((bsz, seq, LANES), F32),
        ],
        compiler_params=_cparams(("arbitrary", "arbitrary")),
        name="pool_layer",
    )(x, mod8, n1, n2, pw, pb, ps, rw, rb)


def _route_kernel(lg_ref, pos_ref, post_ref, gt_ref, seg_ref, span_ref, carry_ref, *, rows):
    i = pl.program_id(0)
    work = lg_ref[...]
    lane = lax.broadcasted_iota(jnp.int32, work.shape, 1).astype(F32)
    tops, hots = [], []
    for _ in range(TOP_K):
        m = jnp.max(work, axis=1, keepdims=True)
        idx = jnp.min(jnp.where(work == m, lane, float(LANES)), axis=1, keepdims=True)
        hot = lane == idx
        work = jnp.where(hot, -jnp.inf, work)
        tops.append(m)
        hots.append(hot)
    chosen = jnp.zeros(work.shape, F32)
    for hot in hots:
        chosen = chosen + jnp.where(hot, 1.0, 0.0)
    count = jnp.broadcast_to(jnp.sum(chosen, axis=0, keepdims=True), carry_ref.shape)
    seg_rows = jnp.floor((count + (SUBLANES - 1.0)) * (1.0 / SUBLANES)) * SUBLANES

    def lane_prefix(v):
        shift = 1
        while shift < N_EXPERTS:
            v = v + pltpu.roll(v, shift, axis=1)
            shift *= 2
        return v

    @pl.when(i == 0)
    def _():
        carry_ref[...] = jnp.zeros_like(carry_ref)

    r = lax.broadcasted_iota(jnp.int32, (rows, rows), 0)
    c = lax.broadcasted_iota(jnp.int32, (rows, rows), 1)
    earlier = jnp.where(c < r, 1.0, 0.0).astype(BF16)
    local = (lane_prefix(seg_rows) - seg_rows)[0:1]
    before = _dot(earlier, chosen.astype(BF16)) + local
    exps = [jnp.exp(t - tops[0]) for t in tops]
    den = exps[0] + exps[1] + exps[2] + exps[3]
    col = lax.broadcasted_iota(jnp.int32, (rows, TOP_K), 1)
    wide = lax.broadcasted_iota(jnp.int32, (rows, LANES), 1)
    gt = jnp.zeros((rows, TOP_K), F32)
    ps = jnp.zeros((rows, TOP_K), F32)
    ps_wide = jnp.zeros((rows, LANES), F32)
    for k in range(TOP_K):
        pos_k = jnp.sum(jnp.where(hots[k], before, 0.0), axis=1, keepdims=True)
        gt = jnp.where(col == k, exps[k] / den, gt)
        ps = jnp.where(col == k, pos_k, ps)
        ps_wide = jnp.where(wide == k, pos_k, ps_wide)
    pos_ref[...] = ps.astype(jnp.int32)
    post_ref[...] = jnp.transpose(ps_wide)[0:SUBLANES].astype(jnp.int32)
    gt_ref[...] = gt
    sub = lax.broadcasted_iota(jnp.int32, carry_ref.shape, 0)
    seg_ref[0] = jnp.where(sub == 0, carry_ref[...],
                           jnp.where(sub == 1, seg_rows, 0.0)).astype(jnp.int32)
    total = carry_ref[...] + seg_rows
    carry_ref[...] = total
    blocks = jnp.floor((total + (EXPERT_BLOCK - 1.0)) * (1.0 / EXPERT_BLOCK))
    ends = lane_prefix(blocks) * EXPERT_BLOCK
    span_ref[...] = jnp.where(sub == 0, ends, jnp.where(sub == 1, ends - blocks * EXPERT_BLOCK,
                                                        0.0)).astype(jnp.int32)


def _route(logits):
    t = logits.shape[0]
    rows = SEQ_TILE
    tiles = t // rows
    tok = pl.BlockSpec((rows, TOP_K), lambda i: (i, 0))
    return pl.pallas_call(
        functools.partial(_route_kernel, rows=rows),
        grid=(tiles,),
        in_specs=[pl.BlockSpec((rows, LANES), lambda i: (i, 0))],
        out_specs=[tok,
                   pl.BlockSpec((SUBLANES, rows), lambda i: (0, i)),
                   tok,
                   pl.BlockSpec((1, SUBLANES, LANES), lambda i: (i, 0, 0)),
                   pl.BlockSpec((SUBLANES, LANES), lambda i: (0, 0))],
        out_shape=[
            jax.ShapeDtypeStruct((t, TOP_K), jnp.int32),
            jax.ShapeDtypeStruct((SUBLANES, t), jnp.int32),
            jax.ShapeDtypeStruct((t, TOP_K), F32),
            jax.ShapeDtypeStruct((tiles, SUBLANES, LANES), jnp.int32),
            jax.ShapeDtypeStruct((SUBLANES, LANES), jnp.int32),
        ],
        scratch_shapes=[pltpu.VMEM((SUBLANES, LANES), F32)],
        compiler_params=_cparams(("arbitrary",)),
        name="route",
    )(logits)


def _segment_copies(go, dst_ref, cnt_ref, tile, make):
    def per_expert(e, local):
        n = cnt_ref[tile * N_EXPERTS + e]
        d = dst_ref[tile * N_EXPERTS + e]

        def chunk(c, carry):
            off = c * SUBLANES
            go(make(pl.multiple_of(local + off, SUBLANES), pl.multiple_of(d + off, SUBLANES)))
            return carry

        lax.fori_loop(0, n // SUBLANES, chunk, 0)
        return local + n

    lax.fori_loop(0, N_EXPERTS, per_expert, 0)


def _dispatch_kernel(dst_ref, cnt_ref, ends_ref, nu_ref, post_ref, *refs, tiles, n_blocks):
    n_src = len(tiles)
    srcs, (xs_ref, sorted_ref, zeros, sem, zsem) = refs[:n_src], refs[n_src:]
    i = pl.program_id(0)
    blk = EXPERT_BLOCK

    @pl.when(i == 0)
    def _():
        zeros[...] = jnp.zeros_like(zeros)

        def fill(b):
            return pltpu.make_async_copy(zeros, xs_ref.at[pl.ds(b * blk, blk)], zsem)

        def expert_tails(go):
            for e in range(N_EXPERTS):
                lo = ends_ref[e - 1] if e else 0

                @pl.when(ends_ref[e] > lo)
                def _(e=e):
                    go(fill(ends_ref[e] // blk - 1))

        def unused(go):
            def body(b, carry):
                go(fill(b))
                return carry
            lax.fori_loop(nu_ref[0], n_blocks, body, 0)

        expert_tails(lambda cp: cp.start())
        unused(lambda cp: cp.start())
        expert_tails(lambda cp: cp.wait())
        unused(lambda cp: cp.wait())

    h = srcs[0][...]
    first = tiles[0]
    for src, n in zip(srcs[1:], tiles[1:]):
        h = jnp.where(i >= first, src[...], h)
        first += n

    pos = post_ref[...]
    j = lax.broadcasted_iota(jnp.int32, (SORT_ROWS, h.shape[0]), 0)
    onehot = jnp.zeros(j.shape, F32)
    for k in range(TOP_K):
        onehot = jnp.where(j == pos[k:k + 1], 1.0, onehot)
    slot = i % 2
    sorted_ref[slot] = _dot(onehot.astype(BF16), h)

    def make(buf):
        def build(local, glob):
            return pltpu.make_async_copy(sorted_ref.at[buf, pl.ds(local, SUBLANES)],
                                         xs_ref.at[pl.ds(glob, SUBLANES)], sem.at[buf])
        return build

    _segment_copies(lambda cp: cp.start(), dst_ref, cnt_ref, i, make(slot))

    @pl.when(i > 0)
    def _():
        _segment_copies(lambda cp: cp.wait(), dst_ref, cnt_ref, i - 1, make(1 - slot))

    @pl.when(i == pl.num_programs(0) - 1)
    def _():
        _segment_copies(lambda cp: cp.wait(), dst_ref, cnt_ref, i, make(slot))


def _dispatch(seg_dst, seg_cnt, ends, n_used, pos_t, sources, n_blocks):
    rows = SEQ_TILE
    d = sources[0].shape[1]
    tiles = tuple(s.shape[0] // rows for s in sources)
    firsts = [sum(tiles[:k]) for k in range(len(tiles))]

    def src_spec(first, n):
        return pl.BlockSpec((rows, d), lambda i, *_: (jnp.clip(i - first, 0, n - 1), 0))

    grid_spec = pltpu.PrefetchScalarGridSpec(
        num_scalar_prefetch=4,
        grid=(sum(tiles),),
        in_specs=[pl.BlockSpec((SUBLANES, rows), lambda i, *_: (0, i))]
        + [src_spec(f, n) for f, n in zip(firsts, tiles)],
        out_specs=pl.BlockSpec(memory_space=pl.ANY),
        scratch_shapes=[pltpu.VMEM((2, SORT_ROWS, d), F32), pltpu.VMEM((EXPERT_BLOCK, d), F32),
                        pltpu.SemaphoreType.DMA((2,)), pltpu.SemaphoreType.DMA],
    )
    return pl.pallas_call(
        functools.partial(_dispatch_kernel, tiles=tiles, n_blocks=n_blocks),
        grid_spec=grid_spec,
        out_shape=jax.ShapeDtypeStruct((n_blocks * EXPERT_BLOCK, d), F32),
        compiler_params=_cparams(("arbitrary",)),
        name="dispatch",
    )(seg_dst, seg_cnt, ends, n_used, pos_t, *sources)


def _experts_kernel(be_ref, nu_ref, xs_ref, wgu_ref, bgu_ref, wd_ref, bd_ref, ys_ref,
                    wgu_bf, wd_bf, *, rows, dff):
    b = pl.program_id(0)
    prev = be_ref[jnp.maximum(b - 1, 0)]
    fresh = (b == 0) | (be_ref[b] != prev)
    live = b < nu_ref[0]

    @pl.when(live & fresh)
    def _():
        wgu_bf[...] = wgu_ref[0, 0].astype(BF16)
        wd_bf[...] = wd_ref[0, 0].astype(BF16)

    @pl.when(live)
    def _():
        gu = _dot(xs_ref[...].astype(BF16), wgu_bf[...]) + bgu_ref[0, 0]
        gate = jnp.minimum(gu[:, :dff], SWIGLU_LIMIT)
        up = jnp.clip(gu[:, dff:], -SWIGLU_LIMIT, SWIGLU_LIMIT)
        act = (up + 1.0) * (gate * _sigmoid(SWIGLU_ALPHA * gate))
        ys_ref[...] = _dot(act.astype(BF16), wd_bf[...]) + bd_ref[0, 0]

    @pl.when(jnp.logical_not(live))
    def _():
        ys_ref[...] = jnp.zeros_like(ys_ref)


def _experts(layer, block_expert, n_used, xs, w_gu, b_gu, w_down, b_down):
    n_rows = xs.shape[0]
    rows = EXPERT_BLOCK
    n_blocks = n_rows // rows
    depth, e, d, dff2 = w_gu.shape
    dff = dff2 // 2
    blk = lambda b, be, nu: (jnp.minimum(b, nu[0] - 1), 0)
    per_expert = lambda b, be, nu: (layer, be[b], 0, 0)
    grid_spec = pltpu.PrefetchScalarGridSpec(
        num_scalar_prefetch=2,
        grid=(n_blocks,),
        in_specs=[
            pl.BlockSpec((rows, d), blk),
            pl.BlockSpec((1, 1, d, dff2), per_expert),
            pl.BlockSpec((1, 1, 1, dff2), per_expert),
            pl.BlockSpec((1, 1, dff, d), per_expert),
            pl.BlockSpec((1, 1, 1, d), per_expert),
        ],
        out_specs=pl.BlockSpec((rows, d), lambda b, be, nu: (b, 0)),
        scratch_shapes=[pltpu.VMEM((d, dff2), BF16), pltpu.VMEM((dff, d), BF16)],
    )
    return pl.pallas_call(
        functools.partial(_experts_kernel, rows=rows, dff=dff),
        grid_spec=grid_spec,
        out_shape=jax.ShapeDtypeStruct((n_rows, d), F32),
        compiler_params=_cparams(("arbitrary",)),
        name="experts",
    )(block_expert, n_used, xs, w_gu, b_gu.reshape(depth, e, 1, dff2), w_down,
      b_down.reshape(depth, e, 1, d))


def _combine_kernel(dst_ref, cnt_ref, ys_ref, pos_ref, gt_ref, x1_ref, mod_ref, fg_ref, out_ref,
                    sorted_ref, sem, *, per, tile0, final):
    step = pl.program_id(0) * per + pl.program_id(1)
    steps = pl.num_programs(0) * per
    tile = tile0 + step
    slot = step % 2

    def make(buf):
        def build(local, glob):
            return pltpu.make_async_copy(ys_ref.at[pl.ds(glob, SUBLANES)],
                                         sorted_ref.at[buf, pl.ds(local, SUBLANES)], sem.at[buf])
        return build

    @pl.when(step == 0)
    def _():
        sorted_ref[...] = jnp.zeros_like(sorted_ref)
        _segment_copies(lambda cp: cp.start(), dst_ref, cnt_ref, tile, make(slot))

    @pl.when(step + 1 < steps)
    def _():
        _segment_copies(lambda cp: cp.start(), dst_ref, cnt_ref, tile + 1, make(1 - slot))

    _segment_copies(lambda cp: cp.wait(), dst_ref, cnt_ref, tile, make(slot))

    pos = pos_ref[...]
    gates = gt_ref[...]
    lane = lax.broadcasted_iota(jnp.int32, (pos.shape[0], SORT_ROWS), 1)
    weights = jnp.zeros(lane.shape, F32)
    for k in range(TOP_K):
        weights = jnp.where(lane == pos[:, k:k + 1], gates[:, k:k + 1], weights)
    y = _dot(weights.astype(BF16), sorted_ref[slot].astype(BF16))
    x2 = x1_ref[0] + mod_ref[0][5:6] * y
    if final:
        x2 = _rms(x2, fg_ref[...])
    out_ref[0] = x2


def _combine(seg_dst, seg_cnt, tile0, pos, gates, ys, x1, mod8, final_g, final):
    bsz, seq, d = x1.shape
    rows = SEQ_TILE
    per = seq // rows
    tok = pl.BlockSpec((rows, TOP_K), lambda b, j, *_: (tile0 + b * per + j, 0))
    grid_spec = pltpu.PrefetchScalarGridSpec(
        num_scalar_prefetch=2,
        grid=(bsz, per),
        in_specs=[
            pl.BlockSpec(memory_space=pl.ANY),
            tok, tok,
            pl.BlockSpec((1, rows, d), lambda b, j, *_: (b, j, 0)),
            pl.BlockSpec((1, SUBLANES, d), lambda b, j, *_: (b, 0, 0)),
            pl.BlockSpec((1, d), lambda b, j, *_: (0, 0)),
        ],
        out_specs=pl.BlockSpec((1, rows, d), lambda b, j, *_: (b, j, 0)),
        scratch_shapes=[pltpu.VMEM((2, SORT_ROWS, d), F32), pltpu.SemaphoreType.DMA((2,))],
    )
    return pl.pallas_call(
        functools.partial(_combine_kernel, per=per, tile0=tile0, final=final),
        grid_spec=grid_spec,
        out_shape=jax.ShapeDtypeStruct((bsz, seq, d), F32),
        compiler_params=_cparams(("arbitrary", "arbitrary")),
        name="combine",
    )(seg_dst, seg_cnt, ys, pos, gates, x1, mod8, final_g)


def _moe(layer, streams, w_gu, b_gu, w_down, b_down, final_g, final):
    logits = jnp.concatenate([s[2].reshape(-1, LANES) for s in streams], axis=0)
    t = logits.shape[0]
    d = streams[0][0].shape[2]
    pos, pos_t, gates, seg, span = _route(logits)
    tiles = seg.shape[0]
    ends = span[0, :N_EXPERTS]
    seg_dst = (seg[:, 0, :N_EXPERTS] + span[1:2, :N_EXPERTS]).reshape(-1)
    seg_cnt = seg[:, 1, :N_EXPERTS].reshape(-1)
    blk = EXPERT_BLOCK
    max_rows = t * TOP_K + tiles * N_EXPERTS * (SUBLANES - 1)
    n_blocks = -(-max_rows // blk) + N_EXPERTS
    n_used = ends[N_EXPERTS - 1] // blk
    block_start = jnp.arange(n_blocks, dtype=jnp.int32) * blk
    bexp = jnp.sum((ends[None, :] <= block_start[:, None]).astype(jnp.int32), axis=1)
    last = jnp.sum((ends <= (n_used - 1) * blk).astype(jnp.int32))
    bexp = jnp.where(jnp.arange(n_blocks) < n_used, bexp, last)
    sources = [s[1].reshape(-1, d) for s in streams]
    n_used = n_used.reshape(1)
    xs = _dispatch(seg_dst, seg_cnt, ends, n_used, pos_t, sources, n_blocks)
    ys = _experts(layer, bexp, n_used, xs, w_gu, b_gu, w_down, b_down)
    outs, tile0 = [], 0
    for x1, _, _, mod8 in streams:
        outs.append(_combine(seg_dst, seg_cnt, tile0, pos, gates, ys, x1, mod8, final_g, final))
        tile0 += x1.shape[0] * x1.shape[1] // SEQ_TILE
    return outs


def _rope(x, cos, sin):
    return x * cos + pltpu.roll(x, LANES // 2, axis=1) * sin


def _proj_kernel(xc_ref, x_ref, modc_ref, modl_ref, n1_ref, cos_ref, sin_ref,
                 wdq_ref, qg_ref, wuq_ref, wdkv_ref, kvg_ref, wukv_ref,
                 q_ref, k_ref, v_ref, *, ctx_tiles):
    j = pl.program_id(1)
    is_ctx = j < ctx_tiles
    xin = jnp.where(is_ctx, xc_ref[0], x_ref[0])
    mod = jnp.where(is_ctx, modc_ref[0], modl_ref[0])
    sh1, s1 = mod[0:1], mod[1:2]
    hb = (_rms(xin, n1_ref[...]) * (1.0 + s1) + sh1).astype(BF16)
    cos = jnp.where(is_ctx, 1.0, cos_ref[...])
    sin = jnp.where(is_ctx, 0.0, sin_ref[...])

    kva = _dot(hb, wdkv_ref[...])
    kpe = _rope(kva[:, KV_LORA:], cos, sin).astype(BF16)
    kv = _dot(_rms(kva[:, :KV_LORA], kvg_ref[...]).astype(BF16), wukv_ref[...])
    nope_all = N_HEADS * QK_NOPE
    ones = jnp.ones((kv.shape[0], V_HEAD), BF16)
    for h in range(N_HEADS):
        k_ref[0, :, h * HEAD_PAD:h * HEAD_PAD + QK_NOPE] = \
            kv[:, h * QK_NOPE:(h + 1) * QK_NOPE].astype(BF16)
        k_ref[0, :, h * HEAD_PAD + QK_NOPE:(h + 1) * HEAD_PAD] = kpe
        v_ref[0, :, h * HEAD_PAD:h * HEAD_PAD + V_HEAD] = \
            kv[:, nope_all + h * V_HEAD:nope_all + (h + 1) * V_HEAD].astype(BF16)
        v_ref[0, :, h * HEAD_PAD + V_HEAD:(h + 1) * HEAD_PAD] = ones

    @pl.when(jnp.logical_not(is_ctx))
    def _():
        qa = _dot(hb, wdq_ref[...])
        q = _dot(_rms(qa, qg_ref[...]).astype(BF16), wuq_ref[...]) * Q_SCALE
        for h in range(N_HEADS):
            lo = h * HEAD_PAD
            q_ref[0, :, lo:lo + QK_NOPE] = q[:, lo:lo + QK_NOPE].astype(BF16)
            q_ref[0, :, lo + QK_NOPE:lo + HEAD_PAD] = \
                _rope(q[:, lo + QK_NOPE:lo + HEAD_PAD], cos, sin).astype(BF16)


def _projections(xc, x, modc8, modl8, n1, cos_t, sin_t, wdq, qg, wuq, wdkv, kvg, wukv):
    bsz, seq, d = x.shape
    ctx = xc.shape[1]
    rows = SEQ_TILE
    ctx_tiles = ctx // rows
    lat = lambda b, j: (b, jnp.maximum(j - ctx_tiles, 0), 0)
    full = lambda a: pl.BlockSpec(a.shape, lambda b, j: (0,) * a.ndim)
    kw = N_HEADS * HEAD_PAD
    vw = N_HEADS * HEAD_PAD
    return pl.pallas_call(
        functools.partial(_proj_kernel, ctx_tiles=ctx_tiles),
        grid=(bsz, (ctx + seq) // rows),
        in_specs=[
            pl.BlockSpec((1, rows, d), lambda b, j: (b, jnp.minimum(j, ctx_tiles - 1), 0)),
            pl.BlockSpec((1, rows, d), lat),
            pl.BlockSpec((1, SUBLANES, d), lambda b, j: (b, 0, 0)),
            pl.BlockSpec((1, SUBLANES, d), lambda b, j: (b, 0, 0)),
            full(n1),
            pl.BlockSpec((rows, LANES), lambda b, j: (jnp.maximum(j - ctx_tiles, 0), 0)),
            pl.BlockSpec((rows, LANES), lambda b, j: (jnp.maximum(j - ctx_tiles, 0), 0)),
            full(wdq), full(qg), full(wuq), full(wdkv), full(kvg), full(wukv),
        ],
        out_specs=[
            pl.BlockSpec((1, rows, kw), lat),
            pl.BlockSpec((1, rows, kw), lambda b, j: (b, j, 0)),
            pl.BlockSpec((1, rows, vw), lambda b, j: (b, j, 0)),
        ],
        out_shape=[
            jax.ShapeDtypeStruct((bsz, seq, kw), BF16),
            jax.ShapeDtypeStruct((bsz, ctx + seq, kw), BF16),
            jax.ShapeDtypeStruct((bsz, ctx + seq, vw), BF16),
        ],
        compiler_params=_cparams(("arbitrary", "arbitrary")),
        name="projections",
    )(xc, x, modc8, modl8, n1, cos_t, sin_t, wdq, qg, wuq, wdkv, kvg, wukv)


def _attn_kernel(q_ref, k_ref, v_ref, o_ref, *, chunk):
    q = q_ref[0]
    m = acc = None
    for c in range(k_ref.shape[1] // chunk):
        rows = slice(c * chunk, (c + 1) * chunk)
        s = lax.dot_general(q, k_ref[0, rows, :], (((1,), (1,)), ((), ())),
                            preferred_element_type=F32)
        m_c = jnp.max(s, axis=-1, keepdims=True)
        m_new = m_c if c == 0 else jnp.maximum(m, m_c)
        pv = _dot(jnp.exp2(s - m_new).astype(BF16), v_ref[0, rows, :])
        acc = pv if c == 0 else acc * jnp.exp2(m - m_new) + pv
        m = m_new
    o_ref[0] = (acc[:, :V_HEAD] / acc[:, V_HEAD:]).astype(BF16)


def _attention(q, k, v):
    bsz, seq, _ = q.shape
    keys = k.shape[1]
    rows = min(Q_TILE, seq)
    chunk = next(c for c in (KEY_CHUNK, 2 * LANES, keys) if keys % c == 0)
    return pl.pallas_call(
        functools.partial(_attn_kernel, chunk=chunk),
        grid=(bsz, N_HEADS, seq // rows),
        in_specs=[
            pl.BlockSpec((1, rows, HEAD_PAD), lambda b, h, i: (b, i, h)),
            pl.BlockSpec((1, keys, HEAD_PAD), lambda b, h, i: (b, 0, h)),
            pl.BlockSpec((1, keys, HEAD_PAD), lambda b, h, i: (b, 0, h)),
        ],
        out_specs=pl.BlockSpec((1, rows, V_HEAD), lambda b, h, i: (b, i, h)),
        out_shape=jax.ShapeDtypeStruct((bsz, seq, N_HEADS * V_HEAD), BF16),
        compiler_params=_cparams(("arbitrary", "arbitrary", "arbitrary")),
        name="attention",
    )(q, k, v)


def _oproj_kernel(o_ref, x_ref, mod_ref, wo_ref, n2_ref, rw_ref, rb_ref,
                  x1_ref, h2_ref, lg_ref, *, rows):
    mod = mod_ref[0]
    x1 = x_ref[0] + mod[2:3] * _dot(o_ref[0], wo_ref[...])
    _tail(x1, mod, n2_ref, rw_ref, rb_ref, x1_ref, h2_ref, lg_ref)


def _out_projection(o, x, mod8, wo, n2, rw, rb):
    bsz, seq, d = x.shape
    rows = SEQ_TILE
    const2 = lambda b, j: (0, 0)
    return pl.pallas_call(
        functools.partial(_oproj_kernel, rows=rows),
        grid=(bsz, seq // rows),
        in_specs=[
            pl.BlockSpec((1, rows, o.shape[2]), lambda b, j: (b, j, 0)),
            pl.BlockSpec((1, rows, d), lambda b, j: (b, j, 0)),
            pl.BlockSpec((1, SUBLANES, d), lambda b, j: (b, 0, 0)),
            pl.BlockSpec(wo.shape, const2),
            pl.BlockSpec((1, d), const2),
            pl.BlockSpec(rw.shape, const2),
            pl.BlockSpec((1, LANES), const2),
        ],
        out_specs=[
            pl.BlockSpec((1, rows, d), lambda b, j: (b, j, 0)),
            pl.BlockSpec((1, rows, d), lambda b, j: (b, j, 0)),
            pl.BlockSpec((1, rows, LANES), lambda b, j: (b, j, 0)),
        ],
        out_shape=[
            jax.ShapeDtypeStruct((bsz, seq, d), F32),
            jax.ShapeDtypeStruct((bsz, seq, d), BF16),
            jax.ShapeDtypeStruct((bsz, seq, LANES), F32),
        ],
        compiler_params=_cparams(("arbitrary", "arbitrary")),
        name="out_projection",
    )(o, x, mod8, wo, n2, rw, rb)


def _rope_cols(base):
    q = QK_ROPE // 4
    x1 = list(range(base, base + q)) + list(range(base + 2 * q, base + 3 * q))
    x2 = list(range(base + q, base + 2 * q)) + list(range(base + 3 * q, base + 4 * q))
    pad = [-1] * (LANES // 2 - 2 * q)
    return x1 + pad + x2 + pad


def _take_cols(w, cols):
    wz = jnp.concatenate([w, jnp.zeros((w.shape[0], 1), w.dtype)], axis=1)
    idx = jnp.asarray([c if c >= 0 else w.shape[1] for c in cols], jnp.int32)
    return jnp.take(wz, idx, axis=1)


def _rope_tables(seq):
    q = QK_ROPE // 4
    pos = jnp.arange(seq)
    inv = ROPE_BASE ** (-jnp.arange(0, QK_ROPE // 2, 2, dtype=F32) / (QK_ROPE // 2))
    ang = jnp.concatenate([(pos // GRID_W).astype(F32)[:, None] * inv,
                           (pos % GRID_W).astype(F32)[:, None] * inv], axis=1)
    pad1 = jnp.ones((seq, LANES // 2 - 2 * q), F32)
    pad0 = jnp.zeros((seq, LANES // 2 - 2 * q), F32)
    cos = jnp.concatenate([jnp.cos(ang), pad1, jnp.cos(ang), pad1], axis=1)
    sin = jnp.concatenate([-jnp.sin(ang), pad0, jnp.sin(ang), pad0], axis=1)
    return cos, sin


def _mod8(mod_rows):
    bsz, n = mod_rows.shape
    d = n // 6
    m = mod_rows.reshape(bsz, 6, d)
    return jnp.concatenate([m, jnp.zeros((bsz, SUBLANES - 6, d), F32)], axis=1)


def kernel(x, c, ctx, c_ctx, ada_w, ada_b, norm1_g, norm2_g, pool_w, pool_b, pool_scale,
           w_dq, q_norm_g, w_uq, w_dkv, kv_norm_g, w_ukv, w_o,
           router_w, router_b, w_gu, b_gu, w_down, b_down, final_g):
    bsz, seq, d = x.shape
    assert seq % SEQ_TILE == 0 and ctx.shape[1] % SEQ_TILE == 0 and seq % GRID_W == 0
    row = lambda a: a.reshape(1, -1)

    mod_rows = 2 * SUBLANES
    cvec = jnp.concatenate([c, c_ctx[None], jnp.zeros((mod_rows - bsz - 1, d), F32)], axis=0)
    mod = _modulation(cvec, ada_w, ada_b)
    modl = [_mod8(mod[i, :bsz]) for i in range(2)]
    modc = [_mod8(jnp.broadcast_to(mod[i, bsz:bsz + 1], (bsz, 6 * d))) for i in range(2)]

    n_exp = router_w.shape[2]
    rw = jnp.pad(router_w, ((0, 0), (0, 0), (0, LANES - n_exp)))
    rb = jnp.pad(router_b, ((0, 0), (0, LANES - n_exp)), constant_values=-jnp.inf)

    pw = pool_w[0].astype(BF16)
    args0 = (row(norm1_g[0]), row(norm2_g[0]), pw, row(pool_b[0]), row(pool_scale[0]),
             rw[0], row(rb[0]))
    x1, h2, lg = _pool_layer(x, modl[0], *args0)
    xc1, h2c, lgc = _pool_layer(ctx, modc[0], *args0)
    x, xc = _moe(0, [(x1, h2, lg, modl[0]), (xc1, h2c, lgc, modc[0])],
                 w_gu, b_gu, w_down, b_down, row(final_g), False)

    head = QK_NOPE + QK_ROPE
    q_cols = []
    for h in range(N_HEADS):
        q_cols += list(range(h * head, h * head + QK_NOPE)) + _rope_cols(h * head + QK_NOPE)
    kv_cols = list(range(KV_LORA)) + _rope_cols(KV_LORA)
    hk = QK_NOPE + V_HEAD
    ukv_cols = [h * hk + i for h in range(N_HEADS) for i in range(QK_NOPE)] + \
               [h * hk + QK_NOPE + i for h in range(N_HEADS) for i in range(V_HEAD)]
    wuq = _take_cols(w_uq[0], q_cols).astype(BF16)
    wdkv = _take_cols(w_dkv[0], kv_cols).astype(BF16)
    wukv = _take_cols(w_ukv[0], ukv_cols).astype(BF16)
    cos_t, sin_t = _rope_tables(seq)
    q, k, v = _projections(xc, x, modc[1], modl[1], row(norm1_g[1]), cos_t, sin_t,
                           w_dq[0].astype(BF16), row(q_norm_g[0]), wuq, wdkv,
                           row(kv_norm_g[0]), wukv)
    o = _attention(q, k, v)
    x1, h2, lg = _out_projection(o, x, modl[1], w_o[0].astype(BF16), row(norm2_g[1]),
                                 rw[1], row(rb[1]))
    (out,) = _moe(1, [(x1, h2, lg, modl[1])], w_gu, b_gu, w_down, b_down, row(final_g), True)
    return out
```

```python
import functools

import jax
import jax.numpy as jnp
from jax import lax
from jax.experimental import pallas as pl
from jax.experimental.pallas import tpu as pltpu

F32 = jnp.float32
BF16 = jnp.bfloat16

N_HEADS = 8
QK_NOPE = 128
QK_ROPE = 64
V_HEAD = 128
KV_LORA = 256
N_EXPERTS = 32
TOP_K = 4
POOL_WINDOWS = (2, 4, 8, 16)
GRID_W = 64
ROPE_BASE = 10000.0
ATTN_SCALE = (QK_NOPE + QK_ROPE) ** -0.5
Q_SCALE = ATTN_SCALE * 1.4426950408889634
SWIGLU_LIMIT = 7.0
SWIGLU_ALPHA = 1.702
EPS = 1e-6

LANES = 128
SUBLANES = 8
HEAD_PAD = 2 * LANES
VMEM_LIMIT = 56 * 1024 * 1024

SEQ_TILE = 256
EXPERT_BLOCK = 512
Q_TILE = 2048
KEY_CHUNK = 256
MOD_COLS = 1536
HALO = 8
SORT_ROWS = -(-(TOP_K * SEQ_TILE + N_EXPERTS * (SUBLANES - 1)) // (2 * LANES)) * (2 * LANES)
CHUNK_LANES = -(-(SORT_ROWS // SUBLANES + 1) // LANES) * LANES


def _cparams(sem, vmem=VMEM_LIMIT):
    return pltpu.CompilerParams(dimension_semantics=sem, vmem_limit_bytes=vmem)


def _dot(a, b):
    return jnp.dot(a, b, preferred_element_type=F32)


def _dot3(a, b):
    ah = a.astype(BF16)
    al = (a - ah.astype(F32)).astype(BF16)
    bh = b.astype(BF16)
    bl = (b - bh.astype(F32)).astype(BF16)
    return _dot(ah, bh) + _dot(ah, bl) + _dot(al, bh)


def _rms(x, g):
    return x * lax.rsqrt(jnp.mean(x * x, axis=-1, keepdims=True) + EPS) * g


def _sigmoid(x):
    return 1.0 / (1.0 + jnp.exp(-x))


def _mod_kernel(c_ref, w_ref, b_ref, o_ref):
    a = c_ref[...]
    s = a * _sigmoid(a)
    o_ref[0] = _dot3(s, w_ref[0]) + b_ref[0]


def _modulation(cvec, ada_w, ada_b):
    depth, d, n = ada_w.shape
    rows = cvec.shape[0]
    return pl.pallas_call(
        _mod_kernel,
        grid=(depth, n // MOD_COLS),
        in_specs=[
            pl.BlockSpec((rows, d), lambda i, j: (0, 0)),
            pl.BlockSpec((1, d, MOD_COLS), lambda i, j: (i, 0, j)),
            pl.BlockSpec((1, 1, MOD_COLS), lambda i, j: (i, 0, j)),
        ],
        out_specs=pl.BlockSpec((1, rows, MOD_COLS), lambda i, j: (i, 0, j)),
        out_shape=jax.ShapeDtypeStruct((depth, rows, n), F32),
        compiler_params=_cparams(("arbitrary", "arbitrary")),
        name="modulation",
    )(cvec, ada_w, ada_b.reshape(depth, 1, n))


def _tail(x1, mod, n2_ref, rw_ref, rb_ref, x1_ref, h2_ref, lg_ref):
    sh2, s2 = mod[3:4], mod[4:5]
    h2 = _rms(x1, n2_ref[...]) * (1.0 + s2) + sh2
    x1_ref[0] = x1
    h2_ref[0] = h2.astype(BF16)
    lg_ref[0] = _dot3(h2, rw_ref[...]) + rb_ref[...]


def _pool_kernel(x_ref, mod_ref, n1_ref, n2_ref, pw_ref, pb_ref, ps_ref, rw_ref, rb_ref,
                 x1_ref, h2_ref, lg_ref, *, rows, seq):
    j = pl.program_id(1)
    start = pl.multiple_of(j * rows, rows)
    prev0 = pl.multiple_of(jnp.maximum(start - HALO, 0), HALO)
    next0 = pl.multiple_of(jnp.minimum(start + rows, seq - HALO), HALO)
    xm = x_ref[0, pl.ds(start, rows), :]
    xe = jnp.concatenate(
        [x_ref[0, pl.ds(prev0, HALO), :], xm, x_ref[0, pl.ds(next0, HALO), :]], axis=0)
    mod = mod_ref[0]
    sh1, s1, g1 = mod[0:1], mod[1:2], mod[2:3]
    h = _rms(xe, n1_ref[...]) * (1.0 + s1) + sh1
    pos = start - HALO + lax.broadcasted_iota(jnp.int32, (rows + 2 * HALO, 1), 0)
    hz = jnp.where((pos >= 0) & (pos < seq), h, 0.0)
    tpos = start + lax.broadcasted_iota(jnp.int32, (rows, 1), 0)
    group = hz.shape[1] // len(POOL_WINDOWS)
    ys = []
    for g, w in enumerate(POOL_WINDOWS):
        half = w // 2
        hg = hz[:, g * group:(g + 1) * group]
        tot = hg[HALO - half:HALO - half + rows]
        for o in range(1 - half, half):
            tot = tot + hg[HALO + o:HALO + o + rows]
        cnt = (jnp.minimum(tpos + half, seq) - jnp.maximum(tpos - half, 0)).astype(F32)
        dlt = tot / cnt - hg[HALO:HALO + rows]
        ys.append(_dot(dlt.astype(BF16), pw_ref[g]))
    y = (jnp.concatenate(ys, axis=1) + pb_ref[...]) * ps_ref[...]
    x1 = xm + g1 * y
    _tail(x1, mod, n2_ref, rw_ref, rb_ref, x1_ref, h2_ref, lg_ref)


def _pool_layer(x, mod8, n1, n2, pw, pb, ps, rw, rb):
    bsz, seq, d = x.shape
    rows = min(SEQ_TILE, seq)
    grid = (bsz, seq // rows)
    const2 = lambda b, j: (0, 0)
    return pl.pallas_call(
        functools.partial(_pool_kernel, rows=rows, seq=seq),
        grid=grid,
        in_specs=[
            pl.BlockSpec((1, seq, d), lambda b, j: (b, 0, 0)),
            pl.BlockSpec((1, SUBLANES, d), lambda b, j: (b, 0, 0)),
            pl.BlockSpec((1, d), const2),
            pl.BlockSpec((1, d), const2),
            pl.BlockSpec(pw.shape, lambda b, j: (0, 0, 0)),
            pl.BlockSpec((1, d), const2),
            pl.BlockSpec((1, d), const2),
            pl.BlockSpec(rw.shape, const2),
            pl.BlockSpec((1, LANES), const2),
        ],
        out_specs=[
            pl.BlockSpec((1, rows, d), lambda b, j: (b, j, 0)),
            pl.BlockSpec((1, rows, d), lambda b, j: (b, j, 0)),
            pl.BlockSpec((1, rows, LANES), lambda b, j: (b, j, 0)),
        ],
        out_shape=[
            jax.ShapeDtypeStruct((bsz, seq, d), F32),
            jax.ShapeDtypeStruct((bsz, seq, d), BF16),
            jax.ShapeDtypeStruct((bsz, seq, LANES), F32),
        ],
        compiler_params=_cparams(("arbitrary", "arbitrary")),
        name="pool_layer",
    )(x, mod8, n1, n2, pw, pb, ps, rw, rb)


def _route_kernel(lg_ref, pos_ref, post_ref, gt_ref, chunk_ref, span_ref, carry_ref, *, rows):
    i = pl.program_id(0)
    work = lg_ref[...]
    lane = lax.broadcasted_iota(jnp.int32, work.shape, 1).astype(F32)
    tops, hots = [], []
    for _ in range(TOP_K):
        m = jnp.max(work, axis=1, keepdims=True)
        idx = jnp.min(jnp.where(work == m, lane, float(LANES)), axis=1, keepdims=True)
        hot = lane == idx
        work = jnp.where(hot, -jnp.inf, work)
        tops.append(m)
        hots.append(hot)
    chosen = jnp.zeros(work.shape, F32)
    for hot in hots:
        chosen = chosen + jnp.where(hot, 1.0, 0.0)
    count = jnp.broadcast_to(jnp.sum(chosen, axis=0, keepdims=True), carry_ref.shape)
    seg_rows = jnp.floor((count + (SUBLANES - 1.0)) * (1.0 / SUBLANES)) * SUBLANES

    def lane_prefix(v):
        shift = 1
        while shift < N_EXPERTS:
            v = v + pltpu.roll(v, shift, axis=1)
            shift *= 2
        return v

    @pl.when(i == 0)
    def _():
        carry_ref[...] = jnp.zeros_like(carry_ref)

    r = lax.broadcasted_iota(jnp.int32, (rows, rows), 0)
    c = lax.broadcasted_iota(jnp.int32, (rows, rows), 1)
    earlier = jnp.where(c < r, 1.0, 0.0).astype(BF16)
    local = (lane_prefix(seg_rows) - seg_rows)[0:1]
    before = _dot(earlier, chosen.astype(BF16)) + local
    exps = [jnp.exp(t - tops[0]) for t in tops]
    den = exps[0] + exps[1] + exps[2] + exps[3]
    col = lax.broadcasted_iota(jnp.int32, (rows, TOP_K), 1)
    wide = lax.broadcasted_iota(jnp.int32, (rows, LANES), 1)
    gt = jnp.zeros((rows, TOP_K), F32)
    ps = jnp.zeros((rows, TOP_K), F32)
    ps_wide = jnp.zeros((rows, LANES), F32)
    for k in range(TOP_K):
        pos_k = jnp.sum(jnp.where(hots[k], before, 0.0), axis=1, keepdims=True)
        gt = jnp.where(col == k, exps[k] / den, gt)
        ps = jnp.where(col == k, pos_k, ps)
        ps_wide = jnp.where(wide == k, pos_k, ps_wide)
    pos_ref[...] = ps.astype(jnp.int32)
    post_ref[...] = jnp.transpose(ps_wide)[0:SUBLANES].astype(jnp.int32)
    gt_ref[...] = gt
    sub = lax.broadcasted_iota(jnp.int32, carry_ref.shape, 0)

    run_end = lane_prefix(seg_rows)
    run_start = run_end - seg_rows
    square = (LANES, LANES)
    srow = lax.broadcasted_iota(jnp.int32, square, 0)
    packed = jnp.where(srow == 0, jnp.broadcast_to(run_start[0:1], square),
                       jnp.where(srow == 1, jnp.broadcast_to(run_end[0:1], square),
                                 jnp.broadcast_to((carry_ref[...] - run_start)[0:1], square)))
    cols = jnp.transpose(packed)
    start_c, end_c, shift_c = cols[:, 0:1], cols[:, 1:2], cols[:, 2:3]
    erow = lax.broadcasted_iota(jnp.int32, (LANES, CHUNK_LANES), 0)
    row0 = (lax.broadcasted_iota(jnp.int32, (LANES, CHUNK_LANES), 1) * SUBLANES).astype(F32)
    holds = (erow < N_EXPERTS) & (start_c <= row0) & (row0 < end_c)
    rel = jnp.sum(jnp.where(holds, shift_c, 0.0), axis=0, keepdims=True) + row0[0:1]
    eid = jnp.sum(jnp.where(holds, erow.astype(F32), 0.0), axis=0, keepdims=True)
    n_chunks = jnp.sum(jnp.where(holds, 1.0, 0.0), keepdims=True)
    crow = lax.broadcasted_iota(jnp.int32, (SUBLANES, CHUNK_LANES), 0)
    chunk_ref[0] = jnp.where(crow == 0, rel, jnp.where(crow == 1, eid, n_chunks)).astype(jnp.int32)

    total = carry_ref[...] + seg_rows
    carry_ref[...] = total
    blocks = jnp.floor((total + (EXPERT_BLOCK - 1.0)) * (1.0 / EXPERT_BLOCK))
    ends = lane_prefix(blocks) * EXPERT_BLOCK
    span_ref[...] = jnp.where(sub == 0, ends, jnp.where(sub == 1, ends - blocks * EXPERT_BLOCK,
                                                        0.0)).astype(jnp.int32)


def _route(logits):
    t = logits.shape[0]
    rows = SEQ_TILE
    tiles = t // rows
    tok = pl.BlockSpec((rows, TOP_K), lambda i: (i, 0))
    return pl.pallas_call(
        functools.partial(_route_kernel, rows=rows),
        grid=(tiles,),
        in_specs=[pl.BlockSpec((rows, LANES), lambda i: (i, 0))],
        out_specs=[tok,
                   pl.BlockSpec((SUBLANES, rows), lambda i: (0, i)),
                   tok,
                   pl.BlockSpec((1, SUBLANES, CHUNK_LANES), lambda i: (i, 0, 0)),
                   pl.BlockSpec((SUBLANES, LANES), lambda i: (0, 0))],
        out_shape=[
            jax.ShapeDtypeStruct((t, TOP_K), jnp.int32),
            jax.ShapeDtypeStruct((SUBLANES, t), jnp.int32),
            jax.ShapeDtypeStruct((t, TOP_K), F32),
            jax.ShapeDtypeStruct((tiles, SUBLANES, CHUNK_LANES), jnp.int32),
            jax.ShapeDtypeStruct((SUBLANES, LANES), jnp.int32),
        ],
        scratch_shapes=[pltpu.VMEM((SUBLANES, LANES), F32)],
        compiler_params=_cparams(("arbitrary",)),
        name="route",
    )(logits)


def _chunk_copies(go, list_ref, make):
    def chunk(c, carry):
        go(make(pl.multiple_of(c * SUBLANES, SUBLANES), pl.multiple_of(list_ref[c], SUBLANES)))
        return carry

    lax.fori_loop(0, list_ref[CHUNK_LANES - 1], chunk, 0)


def _dispatch_kernel(ends_ref, nu_ref, list_ref, prev_ref, post_ref, *refs, tiles, n_blocks):
    n_src = len(tiles)
    srcs, (xs_ref, sorted_ref, zeros, sem, zsem) = refs[:n_src], refs[n_src:]
    i = pl.program_id(0)
    blk = EXPERT_BLOCK

    @pl.when(i == 0)
    def _():
        zeros[...] = jnp.zeros_like(zeros)

        def fill(b):
            return pltpu.make_async_copy(zeros, xs_ref.at[pl.ds(b * blk, blk)], zsem)

        def expert_tails(go):
            for e in range(N_EXPERTS):
                lo = ends_ref[e - 1] if e else 0

                @pl.when(ends_ref[e] > lo)
                def _(e=e):
                    go(fill(ends_ref[e] // blk - 1))

        def unused(go):
            def body(b, carry):
                go(fill(b))
                return carry
            lax.fori_loop(nu_ref[0], n_blocks, body, 0)

        expert_tails(lambda cp: cp.start())
        unused(lambda cp: cp.start())
        expert_tails(lambda cp: cp.wait())
        unused(lambda cp: cp.wait())

    h = srcs[0][...]
    first = tiles[0]
    for src, n in zip(srcs[1:], tiles[1:]):
        h = jnp.where(i >= first, src[...], h)
        first += n

    pos = post_ref[...]
    j = lax.broadcasted_iota(jnp.int32, (SORT_ROWS, h.shape[0]), 0)
    onehot = jnp.zeros(j.shape, F32)
    for k in range(TOP_K):
        onehot = jnp.where(j == pos[k:k + 1], 1.0, onehot)
    slot = i % 2
    sorted_ref[slot] = _dot(onehot.astype(BF16), h)

    def make(buf):
        def build(local, glob):
            return pltpu.make_async_copy(sorted_ref.at[buf, pl.ds(local, SUBLANES)],
                                         xs_ref.at[pl.ds(glob, SUBLANES)], sem.at[buf])
        return build

    _chunk_copies(lambda cp: cp.start(), list_ref, make(slot))

    @pl.when(i > 0)
    def _():
        _chunk_copies(lambda cp: cp.wait(), prev_ref, make(1 - slot))

    @pl.when(i == pl.num_programs(0) - 1)
    def _():
        _chunk_copies(lambda cp: cp.wait(), list_ref, make(slot))


def _dispatch(chunks, ends, n_used, pos_t, sources, n_blocks):
    rows = SEQ_TILE
    d = sources[0].shape[1]
    tiles = tuple(s.shape[0] // rows for s in sources)
    firsts = [sum(tiles[:k]) for k in range(len(tiles))]

    def src_spec(first, n):
        return pl.BlockSpec((rows, d), lambda i, *_: (jnp.clip(i - first, 0, n - 1), 0))

    grid_spec = pltpu.PrefetchScalarGridSpec(
        num_scalar_prefetch=2,
        grid=(sum(tiles),),
        in_specs=[pl.BlockSpec((CHUNK_LANES,), lambda i, *_: (i,), memory_space=pltpu.SMEM),
                  pl.BlockSpec((CHUNK_LANES,), lambda i, *_: (jnp.maximum(i - 1, 0),),
                               memory_space=pltpu.SMEM),
                  pl.BlockSpec((SUBLANES, rows), lambda i, *_: (0, i))]
        + [src_spec(f, n) for f, n in zip(firsts, tiles)],
        out_specs=pl.BlockSpec(memory_space=pl.ANY),
        scratch_shapes=[pltpu.VMEM((2, SORT_ROWS, d), F32), pltpu.VMEM((EXPERT_BLOCK, d), F32),
                        pltpu.SemaphoreType.DMA((2,)), pltpu.SemaphoreType.DMA],
    )
    return pl.pallas_call(
        functools.partial(_dispatch_kernel, tiles=tiles, n_blocks=n_blocks),
        grid_spec=grid_spec,
        out_shape=jax.ShapeDtypeStruct((n_blocks * EXPERT_BLOCK, d), F32),
        compiler_params=_cparams(("arbitrary",)),
        name="dispatch",
    )(ends, n_used, chunks, chunks, pos_t, *sources)


def _experts_kernel(be_ref, nu_ref, xs_ref, wgu_ref, bgu_ref, wd_ref, bd_ref, ys_ref,
                    wgu_bf, wd_bf, *, rows, dff):
    b = pl.program_id(0)
    prev = be_ref[jnp.maximum(b - 1, 0)]
    fresh = (b == 0) | (be_ref[b] != prev)
    live = b < nu_ref[0]

    @pl.when(live & fresh)
    def _():
        wgu_bf[...] = wgu_ref[0, 0].astype(BF16)
        wd_bf[...] = wd_ref[0, 0].astype(BF16)

    @pl.when(live)
    def _():
        gu = _dot(xs_ref[...].astype(BF16), wgu_bf[...]) + bgu_ref[0, 0]
        gate = jnp.minimum(gu[:, :dff], SWIGLU_LIMIT)
        up = jnp.clip(gu[:, dff:], -SWIGLU_LIMIT, SWIGLU_LIMIT)
        act = (up + 1.0) * (gate * _sigmoid(SWIGLU_ALPHA * gate))
        ys_ref[...] = _dot(act.astype(BF16), wd_bf[...]) + bd_ref[0, 0]

    @pl.when(jnp.logical_not(live))
    def _():
        ys_ref[...] = jnp.zeros_like(ys_ref)


def _experts(layer, block_expert, n_used, xs, w_gu, b_gu, w_down, b_down):
    n_rows = xs.shape[0]
    rows = EXPERT_BLOCK
    n_blocks = n_rows // rows
    depth, e, d, dff2 = w_gu.shape
    dff = dff2 // 2
    blk = lambda b, be, nu: (jnp.minimum(b, nu[0] - 1), 0)
    per_expert = lambda b, be, nu: (layer, be[b], 0, 0)
    grid_spec = pltpu.PrefetchScalarGridSpec(
        num_scalar_prefetch=2,
        grid=(n_blocks,),
        in_specs=[
            pl.BlockSpec((rows, d), blk),
            pl.BlockSpec((1, 1, d, dff2), per_expert),
            pl.BlockSpec((1, 1, 1, dff2), per_expert),
            pl.BlockSpec((1, 1, dff, d), per_expert),
            pl.BlockSpec((1, 1, 1, d), per_expert),
        ],
        out_specs=pl.BlockSpec((rows, d), lambda b, be, nu: (b, 0)),
        scratch_shapes=[pltpu.VMEM((d, dff2), BF16), pltpu.VMEM((dff, d), BF16)],
    )
    return pl.pallas_call(
        functools.partial(_experts_kernel, rows=rows, dff=dff),
        grid_spec=grid_spec,
        out_shape=jax.ShapeDtypeStruct((n_rows, d), F32),
        compiler_params=_cparams(("arbitrary",)),
        name="experts",
    )(block_expert, n_used, xs, w_gu, b_gu.reshape(depth, e, 1, dff2), w_down,
      b_down.reshape(depth, e, 1, d))


def _combine_kernel(list_ref, next_ref, ys_ref, pos_ref, gt_ref, x1_ref, mod_ref, fg_ref, out_ref,
                    sorted_ref, sem, *, per, final):
    step = pl.program_id(0) * per + pl.program_id(1)
    steps = pl.num_programs(0) * per
    slot = step % 2

    def make(buf):
        def build(local, glob):
            return pltpu.make_async_copy(ys_ref.at[pl.ds(glob, SUBLANES)],
                                         sorted_ref.at[buf, pl.ds(local, SUBLANES)], sem.at[buf])
        return build

    @pl.when(step == 0)
    def _():
        sorted_ref[...] = jnp.zeros_like(sorted_ref)
        _chunk_copies(lambda cp: cp.start(), list_ref, make(slot))

    @pl.when(step + 1 < steps)
    def _():
        _chunk_copies(lambda cp: cp.start(), next_ref, make(1 - slot))

    _chunk_copies(lambda cp: cp.wait(), list_ref, make(slot))

    pos = pos_ref[...]
    gates = gt_ref[...]
    lane = lax.broadcasted_iota(jnp.int32, (pos.shape[0], SORT_ROWS), 1)
    weights = jnp.zeros(lane.shape, F32)
    for k in range(TOP_K):
        weights = jnp.where(lane == pos[:, k:k + 1], gates[:, k:k + 1], weights)
    y = _dot(weights.astype(BF16), sorted_ref[slot].astype(BF16))
    x2 = x1_ref[0] + mod_ref[0][5:6] * y
    if final:
        x2 = _rms(x2, fg_ref[...])
    out_ref[0] = x2


def _combine(chunks, tile0, pos, gates, ys, x1, mod8, final_g, final):
    bsz, seq, d = x1.shape
    rows = SEQ_TILE
    per = seq // rows
    last = tile0 + bsz * per - 1
    tok = pl.BlockSpec((rows, TOP_K), lambda b, j: (tile0 + b * per + j, 0))
    return pl.pallas_call(
        functools.partial(_combine_kernel, per=per, final=final),
        grid=(bsz, per),
        in_specs=[
            pl.BlockSpec((CHUNK_LANES,), lambda b, j: (tile0 + b * per + j,),
                         memory_space=pltpu.SMEM),
            pl.BlockSpec((CHUNK_LANES,), lambda b, j: (jnp.minimum(tile0 + b * per + j + 1, last),),
                         memory_space=pltpu.SMEM),
            pl.BlockSpec(memory_space=pl.ANY),
            tok, tok,
            pl.BlockSpec((1, rows, d), lambda b, j: (b, j, 0)),
            pl.BlockSpec((1, SUBLANES, d), lambda b, j: (b, 0, 0)),
            pl.BlockSpec((1, d), lambda b, j: (0, 0)),
        ],
        out_specs=pl.BlockSpec((1, rows, d), lambda b, j: (b, j, 0)),
        out_shape=jax.ShapeDtypeStruct((bsz, seq, d), F32),
        scratch_shapes=[pltpu.VMEM((2, SORT_ROWS, d), F32), pltpu.SemaphoreType.DMA((2,))],
        compiler_params=_cparams(("arbitrary", "arbitrary")),
        name="combine",
    )(chunks, chunks, ys, pos, gates, x1, mod8, final_g)


def _moe(layer, streams, w_gu, b_gu, w_down, b_down, final_g, final):
    logits = jnp.concatenate([s[2].reshape(-1, LANES) for s in streams], axis=0)
    t = logits.shape[0]
    d = streams[0][0].shape[2]
    pos, pos_t, gates, chunk, span = _route(logits)
    tiles = chunk.shape[0]
    ends = span[0, :N_EXPERTS]
    starts = span[1, :N_EXPERTS]
    lane = jnp.arange(CHUNK_LANES, dtype=jnp.int32)
    chunks = jnp.where(lane == CHUNK_LANES - 1, chunk[:, 2], chunk[:, 0] + starts[chunk[:, 1]])
    chunks = chunks.reshape(-1)
    blk = EXPERT_BLOCK
    max_rows = t * TOP_K + tiles * N_EXPERTS * (SUBLANES - 1)
    n_blocks = -(-max_rows // blk) + N_EXPERTS
    n_used = ends[N_EXPERTS - 1] // blk
    block_start = jnp.arange(n_blocks, dtype=jnp.int32) * blk
    bexp = jnp.sum((ends[None, :] <= block_start[:, None]).astype(jnp.int32), axis=1)
    last = jnp.sum((ends <= (n_used - 1) * blk).astype(jnp.int32))
    bexp = jnp.where(jnp.arange(n_blocks) < n_used, bexp, last)
    sources = [s[1].reshape(-1, d) for s in streams]
    n_used = n_used.reshape(1)
    xs = _dispatch(chunks, ends, n_used, pos_t, sources, n_blocks)
    ys = _experts(layer, bexp, n_used, xs, w_gu, b_gu, w_down, b_down)
    outs, tile0 = [], 0
    for x1, _, _, mod8 in streams:
        outs.append(_combine(chunks, tile0, pos, gates, ys, x1, mod8, final_g, final))
        tile0 += x1.shape[0] * x1.shape[1] // SEQ_TILE
    return outs


def _rope(x, cos, sin):
    return x * cos + pltpu.roll(x, LANES // 2, axis=1) * sin


def _proj_kernel(xc_ref, x_ref, modc_ref, modl_ref, n1_ref, cos_ref, sin_ref,
                 wdq_ref, qg_ref, wuq_ref, wdkv_ref, kvg_ref, wukv_ref,
                 q_ref, k_ref, v_ref, *, ctx_tiles):
    j = pl.program_id(1)
    is_ctx = j < ctx_tiles
    xin = jnp.where(is_ctx, xc_ref[0], x_ref[0])
    mod = jnp.where(is_ctx, modc_ref[0], modl_ref[0])
    sh1, s1 = mod[0:1], mod[1:2]
    hb = (_rms(xin, n1_ref[...]) * (1.0 + s1) + sh1).astype(BF16)
    cos = jnp.where(is_ctx, 1.0, cos_ref[...])
    sin = jnp.where(is_ctx, 0.0, sin_ref[...])

    kva = _dot(hb, wdkv_ref[...])
    kpe = _rope(kva[:, KV_LORA:], cos, sin).astype(BF16)
    kv = _dot(_rms(kva[:, :KV_LORA], kvg_ref[...]).astype(BF16), wukv_ref[...])
    nope_all = N_HEADS * QK_NOPE
    ones = jnp.ones((kv.shape[0], V_HEAD), BF16)
    for h in range(N_HEADS):
        k_ref[0, :, h * HEAD_PAD:h * HEAD_PAD + QK_NOPE] = \
            kv[:, h * QK_NOPE:(h + 1) * QK_NOPE].astype(BF16)
        k_ref[0, :, h * HEAD_PAD + QK_NOPE:(h + 1) * HEAD_PAD] = kpe
        v_ref[0, :, h * HEAD_PAD:h * HEAD_PAD + V_HEAD] = \
            kv[:, nope_all + h * V_HEAD:nope_all + (h + 1) * V_HEAD].astype(BF16)
        v_ref[0, :, h * HEAD_PAD + V_HEAD:(h + 1) * HEAD_PAD] = ones

    @pl.when(jnp.logical_not(is_ctx))
    def _():
        qa = _dot(hb, wdq_ref[...])
        q = _dot(_rms(qa, qg_ref[...]).astype(BF16), wuq_ref[...]) * Q_SCALE
        for h in range(N_HEADS):
            lo = h * HEAD_PAD
            q_ref[0, :, lo:lo + QK_NOPE] = q[:, lo:lo + QK_NOPE].astype(BF16)
            q_ref[0, :, lo + QK_NOPE:lo + HEAD_PAD] = \
                _rope(q[:, lo + QK_NOPE:lo + HEAD_PAD], cos, sin).astype(BF16)


def _projections(xc, x, modc8, modl8, n1, cos_t, sin_t, wdq, qg, wuq, wdkv, kvg, wukv):
    bsz, seq, d = x.shape
    ctx = xc.shape[1]
    rows = SEQ_TILE
    ctx_tiles = ctx // rows
    lat = lambda b, j: (b, jnp.maximum(j - ctx_tiles, 0), 0)
    full = lambda a: pl.BlockSpec(a.shape, lambda b, j: (0,) * a.ndim)
    kw = N_HEADS * HEAD_PAD
    vw = N_HEADS * HEAD_PAD
    return pl.pallas_call(
        functools.partial(_proj_kernel, ctx_tiles=ctx_tiles),
        grid=(bsz, (ctx + seq) // rows),
        in_specs=[
            pl.BlockSpec((1, rows, d), lambda b, j: (b, jnp.minimum(j, ctx_tiles - 1), 0)),
            pl.BlockSpec((1, rows, d), lat),
            pl.BlockSpec((1, SUBLANES, d), lambda b, j: (b, 0, 0)),
            pl.BlockSpec((1, SUBLANES, d), lambda b, j: (b, 0, 0)),
            full(n1),
            pl.BlockSpec((rows, LANES), lambda b, j: (jnp.maximum(j - ctx_tiles, 0), 0)),
            pl.BlockSpec((rows, LANES), lambda b, j: (jnp.maximum(j - ctx_tiles, 0), 0)),
            full(wdq), full(qg), full(wuq), full(wdkv), full(kvg), full(wukv),
        ],
        out_specs=[
            pl.BlockSpec((1, rows, kw), lat),
            pl.BlockSpec((1, rows, kw), lambda b, j: (b, j, 0)),
            pl.BlockSpec((1, rows, vw), lambda b, j: (b, j, 0)),
        ],
        out_shape=[
            jax.ShapeDtypeStruct((bsz, seq, kw), BF16),
            jax.ShapeDtypeStruct((bsz, ctx + seq, kw), BF16),
            jax.ShapeDtypeStruct((bsz, ctx + seq, vw), BF16),
        ],
        compiler_params=_cparams(("arbitrary", "arbitrary")),
        name="projections",
    )(xc, x, modc8, modl8, n1, cos_t, sin_t, wdq, qg, wuq, wdkv, kvg, wukv)


def _attn_kernel(q_ref, k_ref, v_ref, o_ref, *, chunk):
    q = q_ref[0]
    m = acc = None
    for c in range(k_ref.shape[1] // chunk):
        rows = slice(c * chunk, (c + 1) * chunk)
        s = lax.dot_general(q, k_ref[0, rows, :], (((1,), (1,)), ((), ())),
                            preferred_element_type=F32)
        m_c = jnp.max(s, axis=-1, keepdims=True)
        m_new = m_c if c == 0 else jnp.maximum(m, m_c)
        pv = _dot(jnp.exp2(s - m_new).astype(BF16), v_ref[0, rows, :])
        acc = pv if c == 0 else acc * jnp.exp2(m - m_new) + pv
        m = m_new
    o_ref[0] = (acc[:, :V_HEAD] / acc[:, V_HEAD:]).astype(BF16)


def _attention(q, k, v):
    bsz, seq, _ = q.shape
    keys = k.shape[1]
    rows = min(Q_TILE, seq)
    chunk = next(c for c in (KEY_CHUNK, 2 * LANES, keys) if keys % c == 0)
    return pl.pallas_call(
        functools.partial(_attn_kernel, chunk=chunk),
        grid=(bsz, N_HEADS, seq // rows),
        in_specs=[
            pl.BlockSpec((1, rows, HEAD_PAD), lambda b, h, i: (b, i, h)),
            pl.BlockSpec((1, keys, HEAD_PAD), lambda b, h, i: (b, 0, h)),
            pl.BlockSpec((1, keys, HEAD_PAD), lambda b, h, i: (b, 0, h)),
        ],
        out_specs=pl.BlockSpec((1, rows, V_HEAD), lambda b, h, i: (b, i, h)),
        out_shape=jax.ShapeDtypeStruct((bsz, seq, N_HEADS * V_HEAD), BF16),
        compiler_params=_cparams(("arbitrary", "arbitrary", "arbitrary")),
        name="attention",
    )(q, k, v)


def _oproj_kernel(o_ref, x_ref, mod_ref, wo_ref, n2_ref, rw_ref, rb_ref,
                  x1_ref, h2_ref, lg_ref, *, rows):
    mod = mod_ref[0]
    x1 = x_ref[0] + mod[2:3] * _dot(o_ref[0], wo_ref[...])
    _tail(x1, mod, n2_ref, rw_ref, rb_ref, x1_ref, h2_ref, lg_ref)


def _out_projection(o, x, mod8, wo, n2, rw, rb):
    bsz, seq, d = x.shape
    rows = SEQ_TILE
    const2 = lambda b, j: (0, 0)
    return pl.pallas_call(
        functools.partial(_oproj_kernel, rows=rows),
        grid=(bsz, seq // rows),
        in_specs=[
            pl.BlockSpec((1, rows, o.shape[2]), lambda b, j: (b, j, 0)),
            pl.BlockSpec((1, rows, d), lambda b, j: (b, j, 0)),
            pl.BlockSpec((1, SUBLANES, d), lambda b, j: (b, 0, 0)),
            pl.BlockSpec(wo.shape, const2),
            pl.BlockSpec((1, d), const2),
            pl.BlockSpec(rw.shape, const2),
            pl.BlockSpec((1, LANES), const2),
        ],
        out_specs=[
            pl.BlockSpec((1, rows, d), lambda b, j: (b, j, 0)),
            pl.BlockSpec((1, rows, d), lambda b, j: (b, j, 0)),
            pl.BlockSpec((1, rows, LANES), lambda b, j: (b, j, 0)),
        ],
        out_shape=[
            jax.ShapeDtypeStruct((bsz, seq, d), F32),
            jax.ShapeDtypeStruct((bsz, seq, d), BF16),
            jax.ShapeDtypeStruct((bsz, seq, LANES), F32),
        ],
        compiler_params=_cparams(("arbitrary", "arbitrary")),
        name="out_projection",
    )(o, x, mod8, wo, n2, rw, rb)


def _rope_cols(base):
    q = QK_ROPE // 4
    x1 = list(range(base, base + q)) + list(range(base + 2 * q, base + 3 * q))
    x2 = list(range(base + q, base + 2 * q)) + list(range(base + 3 * q, base + 4 * q))
    pad = [-1] * (LANES // 2 - 2 * q)
    return x1 + pad + x2 + pad


def _take_cols(w, cols):
    wz = jnp.concatenate([w, jnp.zeros((w.shape[0], 1), w.dtype)], axis=1)
    idx = jnp.asarray([c if c >= 0 else w.shape[1] for c in cols], jnp.int32)
    return jnp.take(wz, idx, axis=1)


def _rope_tables(seq):
    q = QK_ROPE // 4
    pos = jnp.arange(seq)
    inv = ROPE_BASE ** (-jnp.arange(0, QK_ROPE // 2, 2, dtype=F32) / (QK_ROPE // 2))
    ang = jnp.concatenate([(pos // GRID_W).astype(F32)[:, None] * inv,
                           (pos % GRID_W).astype(F32)[:, None] * inv], axis=1)
    pad1 = jnp.ones((seq, LANES // 2 - 2 * q), F32)
    pad0 = jnp.zeros((seq, LANES // 2 - 2 * q), F32)
    cos = jnp.concatenate([jnp.cos(ang), pad1, jnp.cos(ang), pad1], axis=1)
    sin = jnp.concatenate([-jnp.sin(ang), pad0, jnp.sin(ang), pad0], axis=1)
    return cos, sin


def _mod8(mod_rows):
    bsz, n = mod_rows.shape
    d = n // 6
    m = mod_rows.reshape(bsz, 6, d)
    return jnp.concatenate([m, jnp.zeros((bsz, SUBLANES - 6, d), F32)], axis=1)


def kernel(x, c, ctx, c_ctx, ada_w, ada_b, norm1_g, norm2_g, pool_w, pool_b, pool_scale,
           w_dq, q_norm_g, w_uq, w_dkv, kv_norm_g, w_ukv, w_o,
           router_w, router_b, w_gu, b_gu, w_down, b_down, final_g):
    bsz, seq, d = x.shape
    assert seq % SEQ_TILE == 0 and ctx.shape[1] % SEQ_TILE == 0 and seq % GRID_W == 0
    row = lambda a: a.reshape(1, -1)

    mod_rows = 2 * SUBLANES
    cvec = jnp.concatenate([c, c_ctx[None], jnp.zeros((mod_rows - bsz - 1, d), F32)], axis=0)
    mod = _modulation(cvec, ada_w, ada_b)
    modl = [_mod8(mod[i, :bsz]) for i in range(2)]
    modc = [_mod8(jnp.broadcast_to(mod[i, bsz:bsz + 1], (bsz, 6 * d))) for i in range(2)]

    n_exp = router_w.shape[2]
    rw = jnp.pad(router_w, ((0, 0), (0, 0), (0, LANES - n_exp)))
    rb = jnp.pad(router_b, ((0, 0), (0, LANES - n_exp)), constant_values=-jnp.inf)

    pw = pool_w[0].astype(BF16)
    args0 = (row(norm1_g[0]), row(norm2_g[0]), pw, row(pool_b[0]), row(pool_scale[0]),
             rw[0], row(rb[0]))
    x1, h2, lg = _pool_layer(x, modl[0], *args0)
    xc1, h2c, lgc = _pool_layer(ctx, modc[0], *args0)
    x, xc = _moe(0, [(x1, h2, lg, modl[0]), (xc1, h2c, lgc, modc[0])],
                 w_gu, b_gu, w_down, b_down, row(final_g), False)

    head = QK_NOPE + QK_ROPE
    q_cols = []
    for h in range(N_HEADS):
        q_cols += list(range(h * head, h * head + QK_NOPE)) + _rope_cols(h * head + QK_NOPE)
    kv_cols = list(range(KV_LORA)) + _rope_cols(KV_LORA)
    hk = QK_NOPE + V_HEAD
    ukv_cols = [h * hk + i for h in range(N_HEADS) for i in range(QK_NOPE)] + \
               [h * hk + QK_NOPE + i for h in range(N_HEADS) for i in range(V_HEAD)]
    wuq = _take_cols(w_uq[0], q_cols).astype(BF16)
    wdkv = _take_cols(w_dkv[0], kv_cols).astype(BF16)
    wukv = _take_cols(w_ukv[0], ukv_cols).astype(BF16)
    cos_t, sin_t = _rope_tables(seq)
    q, k, v = _projections(xc, x, modc[1], modl[1], row(norm1_g[1]), cos_t, sin_t,
                           w_dq[0].astype(BF16), row(q_norm_g[0]), wuq, wdkv,
                           row(kv_norm_g[0]), wukv)
    o = _attention(q, k, v)
    x1, h2, lg = _out_projection(o, x, modl[1], w_o[0].astype(BF16), row(norm2_g[1]),
                                 rw[1], row(rb[1]))
    (out,) = _moe(1, [(x1, h2, lg, modl[1])], w_gu, b_gu, w_down, b_down, row(final_g), True)
    return out
```

```python
import functools

import jax
import jax.numpy as jnp
from jax import lax
from jax.experimental import pallas as pl
from jax.experimental.pallas import tpu as pltpu

F32 = jnp.float32
BF16 = jnp.bfloat16

N_HEADS = 8
QK_NOPE = 128
QK_ROPE = 64
V_HEAD = 128
KV_LORA = 256
N_EXPERTS = 32
TOP_K = 4
POOL_WINDOWS = (2, 4, 8, 16)
GRID_W = 64
ROPE_BASE = 10000.0
ATTN_SCALE = (QK_NOPE + QK_ROPE) ** -0.5
Q_SCALE = ATTN_SCALE * 1.4426950408889634
SWIGLU_LIMIT = 7.0
SWIGLU_ALPHA = 1.702
EPS = 1e-6

LANES = 128
SUBLANES = 8
HEAD_PAD = 2 * LANES
VMEM_LIMIT = 56 * 1024 * 1024

SEQ_TILE = 256
EXPERT_BLOCK = 512
Q_TILE = 2048
KEY_CHUNK = 256
MOD_COLS = 1536
HALO = 8
SORT_ROWS = -(-(TOP_K * SEQ_TILE + N_EXPERTS * (SUBLANES - 1)) // (2 * LANES)) * (2 * LANES)
CHUNK_LANES = -(-(SORT_ROWS // SUBLANES + 1) // LANES) * LANES
CHUNK_UNROLL = 4


def _cparams(sem, vmem=VMEM_LIMIT):
    return pltpu.CompilerParams(dimension_semantics=sem, vmem_limit_bytes=vmem)


def _dot(a, b):
    return jnp.dot(a, b, preferred_element_type=F32)


def _dot3(a, b):
    ah = a.astype(BF16)
    al = (a - ah.astype(F32)).astype(BF16)
    bh = b.astype(BF16)
    bl = (b - bh.astype(F32)).astype(BF16)
    return _dot(ah, bh) + _dot(ah, bl) + _dot(al, bh)


def _rms(x, g):
    return x * lax.rsqrt(jnp.mean(x * x, axis=-1, keepdims=True) + EPS) * g


def _sigmoid(x):
    return 1.0 / (1.0 + jnp.exp(-x))


def _mod_kernel(c_ref, w_ref, b_ref, o_ref):
    a = c_ref[...]
    s = a * _sigmoid(a)
    o_ref[0] = _dot3(s, w_ref[0]) + b_ref[0]


def _modulation(cvec, ada_w, ada_b):
    depth, d, n = ada_w.shape
    rows = cvec.shape[0]
    return pl.pallas_call(
        _mod_kernel,
        grid=(depth, n // MOD_COLS),
        in_specs=[
            pl.BlockSpec((rows, d), lambda i, j: (0, 0)),
            pl.BlockSpec((1, d, MOD_COLS), lambda i, j: (i, 0, j)),
            pl.BlockSpec((1, 1, MOD_COLS), lambda i, j: (i, 0, j)),
        ],
        out_specs=pl.BlockSpec((1, rows, MOD_COLS), lambda i, j: (i, 0, j)),
        out_shape=jax.ShapeDtypeStruct((depth, rows, n), F32),
        compiler_params=_cparams(("arbitrary", "arbitrary")),
        name="modulation",
    )(cvec, ada_w, ada_b.reshape(depth, 1, n))


def _tail(x1, mod, n2_ref, rw_ref, rb_ref, x1_ref, h2_ref, lg_ref):
    sh2, s2 = mod[3:4], mod[4:5]
    h2 = _rms(x1, n2_ref[...]) * (1.0 + s2) + sh2
    x1_ref[0] = x1
    h2_ref[0] = h2.astype(BF16)
    lg_ref[0] = _dot3(h2, rw_ref[...]) + rb_ref[...]


def _pool_kernel(x_ref, mod_ref, n1_ref, n2_ref, pw_ref, pb_ref, ps_ref, rw_ref, rb_ref,
                 x1_ref, h2_ref, lg_ref, *, rows, seq):
    j = pl.program_id(1)
    start = pl.multiple_of(j * rows, rows)
    prev0 = pl.multiple_of(jnp.maximum(start - HALO, 0), HALO)
    next0 = pl.multiple_of(jnp.minimum(start + rows, seq - HALO), HALO)
    xm = x_ref[0, pl.ds(start, rows), :]
    xe = jnp.concatenate(
        [x_ref[0, pl.ds(prev0, HALO), :], xm, x_ref[0, pl.ds(next0, HALO), :]], axis=0)
    mod = mod_ref[0]
    sh1, s1, g1 = mod[0:1], mod[1:2], mod[2:3]
    h = _rms(xe, n1_ref[...]) * (1.0 + s1) + sh1
    pos = start - HALO + lax.broadcasted_iota(jnp.int32, (rows + 2 * HALO, 1), 0)
    hz = jnp.where((pos >= 0) & (pos < seq), h, 0.0)
    tpos = start + lax.broadcasted_iota(jnp.int32, (rows, 1), 0)
    group = hz.shape[1] // len(POOL_WINDOWS)
    ys = []
    for g, w in enumerate(POOL_WINDOWS):
        half = w // 2
        hg = hz[:, g * group:(g + 1) * group]
        tot = hg[HALO - half:HALO - half + rows]
        for o in range(1 - half, half):
            tot = tot + hg[HALO + o:HALO + o + rows]
        cnt = (jnp.minimum(tpos + half, seq) - jnp.maximum(tpos - half, 0)).astype(F32)
        dlt = tot / cnt - hg[HALO:HALO + rows]
        ys.append(_dot(dlt.astype(BF16), pw_ref[g]))
    y = (jnp.concatenate(ys, axis=1) + pb_ref[...]) * ps_ref[...]
    x1 = xm + g1 * y
    _tail(x1, mod, n2_ref, rw_ref, rb_ref, x1_ref, h2_ref, lg_ref)


def _pool_layer(x, mod8, n1, n2, pw, pb, ps, rw, rb):
    bsz, seq, d = x.shape
    rows = min(SEQ_TILE, seq)
    grid = (bsz, seq // rows)
    const2 = lambda b, j: (0, 0)
    return pl.pallas_call(
        functools.partial(_pool_kernel, rows=rows, seq=seq),
        grid=grid,
        in_specs=[
            pl.BlockSpec((1, seq, d), lambda b, j: (b, 0, 0)),
            pl.BlockSpec((1, SUBLANES, d), lambda b, j: (b, 0, 0)),
            pl.BlockSpec((1, d), const2),
            pl.BlockSpec((1, d), const2),
            pl.BlockSpec(pw.shape, lambda b, j: (0, 0, 0)),
            pl.BlockSpec((1, d), const2),
            pl.BlockSpec((1, d), const2),
            pl.BlockSpec(rw.shape, const2),
            pl.BlockSpec((1, LANES), const2),
        ],
        out_specs=[
            pl.BlockSpec((1, rows, d), lambda b, j: (b, j, 0)),
            pl.BlockSpec((1, rows, d), lambda b, j: (b, j, 0)),
            pl.BlockSpec((1, rows, LANES), lambda b, j: (b, j, 0)),
        ],
        out_shape=[
            jax.ShapeDtypeStruct((bsz, seq, d), F32),
            jax.ShapeDtypeStruct((bsz, seq, d), BF16),
            jax.ShapeDtypeStruct((bsz, seq, LANES), F32),
        ],
        compiler_params=_cparams(("arbitrary", "arbitrary")),
        name="pool_layer",
    )(x, mod8, n1, n2, pw, pb, ps, rw, rb)


def _route_kernel(lg_ref, pos_ref, post_ref, gt_ref, chunk_ref, span_ref, carry_ref, *, rows):
    i = pl.program_id(0)
    work = lg_ref[...]
    lane = lax.broadcasted_iota(jnp.int32, work.shape, 1).astype(F32)
    tops, hots = [], []
    for _ in range(TOP_K):
        m = jnp.max(work, axis=1, keepdims=True)
        idx = jnp.min(jnp.where(work == m, lane, float(LANES)), axis=1, keepdims=True)
        hot = lane == idx
        work = jnp.where(hot, -jnp.inf, work)
        tops.append(m)
        hots.append(hot)
    chosen = jnp.zeros(work.shape, F32)
    for hot in hots:
        chosen = chosen + jnp.where(hot, 1.0, 0.0)
    count = jnp.broadcast_to(jnp.sum(chosen, axis=0, keepdims=True), carry_ref.shape)
    seg_rows = jnp.floor((count + (SUBLANES - 1.0)) * (1.0 / SUBLANES)) * SUBLANES

    def lane_prefix(v):
        shift = 1
        while shift < N_EXPERTS:
            v = v + pltpu.roll(v, shift, axis=1)
            shift *= 2
        return v

    @pl.when(i == 0)
    def _():
        carry_ref[...] = jnp.zeros_like(carry_ref)

    r = lax.broadcasted_iota(jnp.int32, (rows, rows), 0)
    c = lax.broadcasted_iota(jnp.int32, (rows, rows), 1)
    earlier = jnp.where(c < r, 1.0, 0.0).astype(BF16)
    local = (lane_prefix(seg_rows) - seg_rows)[0:1]
    before = _dot(earlier, chosen.astype(BF16)) + local
    exps = [jnp.exp(t - tops[0]) for t in tops]
    den = exps[0] + exps[1] + exps[2] + exps[3]
    col = lax.broadcasted_iota(jnp.int32, (rows, TOP_K), 1)
    wide = lax.broadcasted_iota(jnp.int32, (rows, LANES), 1)
    gt = jnp.zeros((rows, TOP_K), F32)
    ps = jnp.zeros((rows, TOP_K), F32)
    ps_wide = jnp.zeros((rows, LANES), F32)
    for k in range(TOP_K):
        pos_k = jnp.sum(jnp.where(hots[k], before, 0.0), axis=1, keepdims=True)
        gt = jnp.where(col == k, exps[k] / den, gt)
        ps = jnp.where(col == k, pos_k, ps)
        ps_wide = jnp.where(wide == k, pos_k, ps_wide)
    pos_ref[...] = ps.astype(jnp.int32)
    post_ref[...] = jnp.transpose(ps_wide)[0:SUBLANES].astype(jnp.int32)
    gt_ref[...] = gt
    sub = lax.broadcasted_iota(jnp.int32, carry_ref.shape, 0)

    run_end = lane_prefix(seg_rows)
    run_start = run_end - seg_rows
    square = (LANES, LANES)
    srow = lax.broadcasted_iota(jnp.int32, square, 0)
    packed = jnp.where(srow == 0, jnp.broadcast_to(run_start[0:1], square),
                       jnp.where(srow == 1, jnp.broadcast_to(run_end[0:1], square),
                                 jnp.broadcast_to((carry_ref[...] - run_start)[0:1], square)))
    cols = jnp.transpose(packed)
    start_c, end_c, shift_c = cols[:, 0:1], cols[:, 1:2], cols[:, 2:3]
    erow = lax.broadcasted_iota(jnp.int32, (LANES, CHUNK_LANES), 0)
    row0 = (lax.broadcasted_iota(jnp.int32, (LANES, CHUNK_LANES), 1) * SUBLANES).astype(F32)
    holds = (erow < N_EXPERTS) & (start_c <= row0) & (row0 < end_c)
    rel = jnp.sum(jnp.where(holds, shift_c, 0.0), axis=0, keepdims=True) + row0[0:1]
    eid = jnp.sum(jnp.where(holds, erow.astype(F32), 0.0), axis=0, keepdims=True)
    n_chunks = jnp.sum(jnp.where(holds, 1.0, 0.0), keepdims=True)
    crow = lax.broadcasted_iota(jnp.int32, (SUBLANES, CHUNK_LANES), 0)
    chunk_ref[0] = jnp.where(crow == 0, rel, jnp.where(crow == 1, eid, n_chunks)).astype(jnp.int32)

    total = carry_ref[...] + seg_rows
    carry_ref[...] = total
    blocks = jnp.floor((total + (EXPERT_BLOCK - 1.0)) * (1.0 / EXPERT_BLOCK))
    ends = lane_prefix(blocks) * EXPERT_BLOCK
    span_ref[...] = jnp.where(sub == 0, ends, jnp.where(sub == 1, ends - blocks * EXPERT_BLOCK,
                                                        0.0)).astype(jnp.int32)


def _route(logits):
    t = logits.shape[0]
    rows = SEQ_TILE
    tiles = t // rows
    tok = pl.BlockSpec((rows, TOP_K), lambda i: (i, 0))
    return pl.pallas_call(
        functools.partial(_route_kernel, rows=rows),
        grid=(tiles,),
        in_specs=[pl.BlockSpec((rows, LANES), lambda i: (i, 0))],
        out_specs=[tok,
                   pl.BlockSpec((SUBLANES, rows), lambda i: (0, i)),
                   tok,
                   pl.BlockSpec((1, SUBLANES, CHUNK_LANES), lambda i: (i, 0, 0)),
                   pl.BlockSpec((SUBLANES, LANES), lambda i: (0, 0))],
        out_shape=[
            jax.ShapeDtypeStruct((t, TOP_K), jnp.int32),
            jax.ShapeDtypeStruct((SUBLANES, t), jnp.int32),
            jax.ShapeDtypeStruct((t, TOP_K), F32),
            jax.ShapeDtypeStruct((tiles, SUBLANES, CHUNK_LANES), jnp.int32),
            jax.ShapeDtypeStruct((SUBLANES, LANES), jnp.int32),
        ],
        scratch_shapes=[pltpu.VMEM((SUBLANES, LANES), F32)],
        compiler_params=_cparams(("arbitrary",)),
        name="route",
    )(logits)


def _chunk_copies(go, list_ref, make):
    n = list_ref[CHUNK_LANES - 1]
    groups = n // CHUNK_UNROLL

    def one(c):
        go(make(pl.multiple_of(c * SUBLANES, SUBLANES), pl.multiple_of(list_ref[c], SUBLANES)))

    def group(g, carry):
        for u in range(CHUNK_UNROLL):
            one(g * CHUNK_UNROLL + u)
        return carry

    def rest(c, carry):
        one(c)
        return carry

    lax.fori_loop(0, groups, group, 0)
    lax.fori_loop(groups * CHUNK_UNROLL, n, rest, 0)


def _dispatch_kernel(ends_ref, nu_ref, list_ref, prev_ref, post_ref, *refs, tiles, n_blocks):
    n_src = len(tiles)
    srcs, (xs_ref, sorted_ref, zeros, sem, zsem) = refs[:n_src], refs[n_src:]
    i = pl.program_id(0)
    blk = EXPERT_BLOCK

    @pl.when(i == 0)
    def _():
        zeros[...] = jnp.zeros_like(zeros)

        def fill(b):
            return pltpu.make_async_copy(zeros, xs_ref.at[pl.ds(b * blk, blk)], zsem)

        def expert_tails(go):
            for e in range(N_EXPERTS):
                lo = ends_ref[e - 1] if e else 0

                @pl.when(ends_ref[e] > lo)
                def _(e=e):
                    go(fill(ends_ref[e] // blk - 1))

        def unused(go):
            def body(b, carry):
                go(fill(b))
                return carry
            lax.fori_loop(nu_ref[0], n_blocks, body, 0)

        expert_tails(lambda cp: cp.start())
        unused(lambda cp: cp.start())
        expert_tails(lambda cp: cp.wait())
        unused(lambda cp: cp.wait())

    h = srcs[0][...]
    first = tiles[0]
    for src, n in zip(srcs[1:], tiles[1:]):
        h = jnp.where(i >= first, src[...], h)
        first += n

    pos = post_ref[...]
    j = lax.broadcasted_iota(jnp.int32, (SORT_ROWS, h.shape[0]), 0)
    onehot = jnp.zeros(j.shape, F32)
    for k in range(TOP_K):
        onehot = jnp.where(j == pos[k:k + 1], 1.0, onehot)
    slot = i % 2
    sorted_ref[slot] = _dot(onehot.astype(BF16), h)

    def make(buf):
        def build(local, glob):
            return pltpu.make_async_copy(sorted_ref.at[buf, pl.ds(local, SUBLANES)],
                                         xs_ref.at[pl.ds(glob, SUBLANES)], sem.at[buf])
        return build

    _chunk_copies(lambda cp: cp.start(), list_ref, make(slot))

    @pl.when(i > 0)
    def _():
        _chunk_copies(lambda cp: cp.wait(), prev_ref, make(1 - slot))

    @pl.when(i == pl.num_programs(0) - 1)
    def _():
        _chunk_copies(lambda cp: cp.wait(), list_ref, make(slot))


def _dispatch(chunks, ends, n_used, pos_t, sources, n_blocks):
    rows = SEQ_TILE
    d = sources[0].shape[1]
    tiles = tuple(s.shape[0] // rows for s in sources)
    firsts = [sum(tiles[:k]) for k in range(len(tiles))]

    def src_spec(first, n):
        return pl.BlockSpec((rows, d), lambda i, *_: (jnp.clip(i - first, 0, n - 1), 0))

    grid_spec = pltpu.PrefetchScalarGridSpec(
        num_scalar_prefetch=2,
        grid=(sum(tiles),),
        in_specs=[pl.BlockSpec((CHUNK_LANES,), lambda i, *_: (i,), memory_space=pltpu.SMEM),
                  pl.BlockSpec((CHUNK_LANES,), lambda i, *_: (jnp.maximum(i - 1, 0),),
                               memory_space=pltpu.SMEM),
                  pl.BlockSpec((SUBLANES, rows), lambda i, *_: (0, i))]
        + [src_spec(f, n) for f, n in zip(firsts, tiles)],
        out_specs=pl.BlockSpec(memory_space=pl.ANY),
        scratch_shapes=[pltpu.VMEM((2, SORT_ROWS, d), F32), pltpu.VMEM((EXPERT_BLOCK, d), F32),
                        pltpu.SemaphoreType.DMA((2,)), pltpu.SemaphoreType.DMA],
    )
    return pl.pallas_call(
        functools.partial(_dispatch_kernel, tiles=tiles, n_blocks=n_blocks),
        grid_spec=grid_spec,
        out_shape=jax.ShapeDtypeStruct((n_blocks * EXPERT_BLOCK, d), F32),
        compiler_params=_cparams(("arbitrary",)),
        name="dispatch",
    )(ends, n_used, chunks, chunks, pos_t, *sources)


def _experts_kernel(be_ref, nu_ref, xs_ref, wgu_ref, bgu_ref, wd_ref, bd_ref, ys_ref,
                    wgu_bf, wd_bf, *, rows, dff):
    b = pl.program_id(0)
    prev = be_ref[jnp.maximum(b - 1, 0)]
    fresh = (b == 0) | (be_ref[b] != prev)
    live = b < nu_ref[0]

    @pl.when(live & fresh)
    def _():
        wgu_bf[...] = wgu_ref[0, 0].astype(BF16)
        wd_bf[...] = wd_ref[0, 0].astype(BF16)

    @pl.when(live)
    def _():
        gu = _dot(xs_ref[...].astype(BF16), wgu_bf[...]) + bgu_ref[0, 0]
        gate = jnp.minimum(gu[:, :dff], SWIGLU_LIMIT)
        up = jnp.clip(gu[:, dff:], -SWIGLU_LIMIT, SWIGLU_LIMIT)
        act = (up + 1.0) * (gate * _sigmoid(SWIGLU_ALPHA * gate))
        ys_ref[...] = _dot(act.astype(BF16), wd_bf[...]) + bd_ref[0, 0]

    @pl.when(jnp.logical_not(live))
    def _():
        ys_ref[...] = jnp.zeros_like(ys_ref)


def _experts(layer, block_expert, n_used, xs, w_gu, b_gu, w_down, b_down):
    n_rows = xs.shape[0]
    rows = EXPERT_BLOCK
    n_blocks = n_rows // rows
    depth, e, d, dff2 = w_gu.shape
    dff = dff2 // 2
    blk = lambda b, be, nu: (jnp.minimum(b, nu[0] - 1), 0)
    per_expert = lambda b, be, nu: (layer, be[b], 0, 0)
    grid_spec = pltpu.PrefetchScalarGridSpec(
        num_scalar_prefetch=2,
        grid=(n_blocks,),
        in_specs=[
            pl.BlockSpec((rows, d), blk),
            pl.BlockSpec((1, 1, d, dff2), per_expert),
            pl.BlockSpec((1, 1, 1, dff2), per_expert),
            pl.BlockSpec((1, 1, dff, d), per_expert),
            pl.BlockSpec((1, 1, 1, d), per_expert),
        ],
        out_specs=pl.BlockSpec((rows, d), lambda b, be, nu: (b, 0)),
        scratch_shapes=[pltpu.VMEM((d, dff2), BF16), pltpu.VMEM((dff, d), BF16)],
    )
    return pl.pallas_call(
        functools.partial(_experts_kernel, rows=rows, dff=dff),
        grid_spec=grid_spec,
        out_shape=jax.ShapeDtypeStruct((n_rows, d), F32),
        compiler_params=_cparams(("arbitrary",)),
        name="experts",
    )(block_expert, n_used, xs, w_gu, b_gu.reshape(depth, e, 1, dff2), w_down,
      b_down.reshape(depth, e, 1, d))


def _combine_kernel(list_ref, next_ref, ys_ref, pos_ref, gt_ref, x1_ref, mod_ref, fg_ref, out_ref,
                    sorted_ref, sem, *, per, final):
    step = pl.program_id(0) * per + pl.program_id(1)
    steps = pl.num_programs(0) * per
    slot = step % 2

    def make(buf):
        def build(local, glob):
            return pltpu.make_async_copy(ys_ref.at[pl.ds(glob, SUBLANES)],
                                         sorted_ref.at[buf, pl.ds(local, SUBLANES)], sem.at[buf])
        return build

    @pl.when(step == 0)
    def _():
        sorted_ref[...] = jnp.zeros_like(sorted_ref)
        _chunk_copies(lambda cp: cp.start(), list_ref, make(slot))

    @pl.when(step + 1 < steps)
    def _():
        _chunk_copies(lambda cp: cp.start(), next_ref, make(1 - slot))

    _chunk_copies(lambda cp: cp.wait(), list_ref, make(slot))

    pos = pos_ref[...]
    gates = gt_ref[...]
    lane = lax.broadcasted_iota(jnp.int32, (pos.shape[0], SORT_ROWS), 1)
    weights = jnp.zeros(lane.shape, F32)
    for k in range(TOP_K):
        weights = jnp.where(lane == pos[:, k:k + 1], gates[:, k:k + 1], weights)
    y = _dot(weights.astype(BF16), sorted_ref[slot].astype(BF16))
    x2 = x1_ref[0] + mod_ref[0][5:6] * y
    if final:
        x2 = _rms(x2, fg_ref[...])
    out_ref[0] = x2


def _combine(chunks, tile0, pos, gates, ys, x1, mod8, final_g, final):
    bsz, seq, d = x1.shape
    rows = SEQ_TILE
    per = seq // rows
    last = tile0 + bsz * per - 1
    tok = pl.BlockSpec((rows, TOP_K), lambda b, j: (tile0 + b * per + j, 0))
    return pl.pallas_call(
        functools.partial(_combine_kernel, per=per, final=final),
        grid=(bsz, per),
        in_specs=[
            pl.BlockSpec((CHUNK_LANES,), lambda b, j: (tile0 + b * per + j,),
                         memory_space=pltpu.SMEM),
            pl.BlockSpec((CHUNK_LANES,), lambda b, j: (jnp.minimum(tile0 + b * per + j + 1, last),),
                         memory_space=pltpu.SMEM),
            pl.BlockSpec(memory_space=pl.ANY),
            tok, tok,
            pl.BlockSpec((1, rows, d), lambda b, j: (b, j, 0)),
            pl.BlockSpec((1, SUBLANES, d), lambda b, j: (b, 0, 0)),
            pl.BlockSpec((1, d), lambda b, j: (0, 0)),
        ],
        out_specs=pl.BlockSpec((1, rows, d), lambda b, j: (b, j, 0)),
        out_shape=jax.ShapeDtypeStruct((bsz, seq, d), F32),
        scratch_shapes=[pltpu.VMEM((2, SORT_ROWS, d), F32), pltpu.SemaphoreType.DMA((2,))],
        compiler_params=_cparams(("arbitrary", "arbitrary")),
        name="combine",
    )(chunks, chunks, ys, pos, gates, x1, mod8, final_g)


def _moe(layer, streams, w_gu, b_gu, w_down, b_down, final_g, final):
    logits = jnp.concatenate([s[2].reshape(-1, LANES) for s in streams], axis=0)
    t = logits.shape[0]
    d = streams[0][0].shape[2]
    pos, pos_t, gates, chunk, span = _route(logits)
    tiles = chunk.shape[0]
    ends = span[0, :N_EXPERTS]
    starts = span[1, :N_EXPERTS]
    lane = jnp.arange(CHUNK_LANES, dtype=jnp.int32)
    hot = chunk[:, 1, :, None] == jnp.arange(N_EXPERTS, dtype=jnp.int32)
    base = jnp.sum(jnp.where(hot, starts, 0), axis=-1)
    chunks = jnp.where(lane == CHUNK_LANES - 1, chunk[:, 2], chunk[:, 0] + base).reshape(-1)
    blk = EXPERT_BLOCK
    max_rows = t * TOP_K + tiles * N_EXPERTS * (SUBLANES - 1)
    n_blocks = -(-max_rows // blk) + N_EXPERTS
    n_used = ends[N_EXPERTS - 1] // blk
    block_start = jnp.arange(n_blocks, dtype=jnp.int32) * blk
    bexp = jnp.sum((ends[None, :] <= block_start[:, None]).astype(jnp.int32), axis=1)
    last = jnp.sum((ends <= (n_used - 1) * blk).astype(jnp.int32))
    bexp = jnp.where(jnp.arange(n_blocks) < n_used, bexp, last)
    sources = [s[1].reshape(-1, d) for s in streams]
    n_used = n_used.reshape(1)
    xs = _dispatch(chunks, ends, n_used, pos_t, sources, n_blocks)
    ys = _experts(layer, bexp, n_used, xs, w_gu, b_gu, w_down, b_down)
    outs, tile0 = [], 0
    for x1, _, _, mod8 in streams:
        outs.append(_combine(chunks, tile0, pos, gates, ys, x1, mod8, final_g, final))
        tile0 += x1.shape[0] * x1.shape[1] // SEQ_TILE
    return outs


def _rope(x, cos, sin):
    return x * cos + pltpu.roll(x, LANES // 2, axis=1) * sin


def _proj_kernel(xc_ref, x_ref, modc_ref, modl_ref, n1_ref, cos_ref, sin_ref,
                 wdq_ref, qg_ref, wuq_ref, wdkv_ref, kvg_ref, wukv_ref,
                 q_ref, k_ref, v_ref, *, ctx_tiles):
    j = pl.program_id(1)
    is_ctx = j < ctx_tiles
    xin = jnp.where(is_ctx, xc_ref[0], x_ref[0])
    mod = jnp.where(is_ctx, modc_ref[0], modl_ref[0])
    sh1, s1 = mod[0:1], mod[1:2]
    hb = (_rms(xin, n1_ref[...]) * (1.0 + s1) + sh1).astype(BF16)
    cos = jnp.where(is_ctx, 1.0, cos_ref[...])
    sin = jnp.where(is_ctx, 0.0, sin_ref[...])

    kva = _dot(hb, wdkv_ref[...])
    kpe = _rope(kva[:, KV_LORA:], cos, sin).astype(BF16)
    kv = _dot(_rms(kva[:, :KV_LORA], kvg_ref[...]).astype(BF16), wukv_ref[...])
    nope_all = N_HEADS * QK_NOPE
    ones = jnp.ones((kv.shape[0], V_HEAD), BF16)
    for h in range(N_HEADS):
        k_ref[0, :, h * HEAD_PAD:h * HEAD_PAD + QK_NOPE] = \
            kv[:, h * QK_NOPE:(h + 1) * QK_NOPE].astype(BF16)
        k_ref[0, :, h * HEAD_PAD + QK_NOPE:(h + 1) * HEAD_PAD] = kpe
        v_ref[0, :, h * HEAD_PAD:h * HEAD_PAD + V_HEAD] = \
            kv[:, nope_all + h * V_HEAD:nope_all + (h + 1) * V_HEAD].astype(BF16)
        v_ref[0, :, h * HEAD_PAD + V_HEAD:(h + 1) * HEAD_PAD] = ones

    @pl.when(jnp.logical_not(is_ctx))
    def _():
        qa = _dot(hb, wdq_ref[...])
        q = _dot(_rms(qa, qg_ref[...]).astype(BF16), wuq_ref[...]) * Q_SCALE
        for h in range(N_HEADS):
            lo = h * HEAD_PAD
            q_ref[0, :, lo:lo + QK_NOPE] = q[:, lo:lo + QK_NOPE].astype(BF16)
            q_ref[0, :, lo + QK_NOPE:lo + HEAD_PAD] = \
                _rope(q[:, lo + QK_NOPE:lo + HEAD_PAD], cos, sin).astype(BF16)


def _projections(xc, x, modc8, modl8, n1, cos_t, sin_t, wdq, qg, wuq, wdkv, kvg, wukv):
    bsz, seq, d = x.shape
    ctx = xc.shape[1]
    rows = SEQ_TILE
    ctx_tiles = ctx // rows
    lat = lambda b, j: (b, jnp.maximum(j - ctx_tiles, 0), 0)
    full = lambda a: pl.BlockSpec(a.shape, lambda b, j: (0,) * a.ndim)
    kw = N_HEADS * HEAD_PAD
    vw = N_HEADS * HEAD_PAD
    return pl.pallas_call(
        functools.partial(_proj_kernel, ctx_tiles=ctx_tiles),
        grid=(bsz, (ctx + seq) // rows),
        in_specs=[
            pl.BlockSpec((1, rows, d), lambda b, j: (b, jnp.minimum(j, ctx_tiles - 1), 0)),
            pl.BlockSpec((1, rows, d), lat),
            pl.BlockSpec((1, SUBLANES, d), lambda b, j: (b, 0, 0)),
            pl.BlockSpec((1, SUBLANES, d), lambda b, j: (b, 0, 0)),
            full(n1),
            pl.BlockSpec((rows, LANES), lambda b, j: (jnp.maximum(j - ctx_tiles, 0), 0)),
            pl.BlockSpec((rows, LANES), lambda b, j: (jnp.maximum(j - ctx_tiles, 0), 0)),
            full(wdq), full(qg), full(wuq), full(wdkv), full(kvg), full(wukv),
        ],
        out_specs=[
            pl.BlockSpec((1, rows, kw), lat),
            pl.BlockSpec((1, rows, kw), lambda b, j: (b, j, 0)),
            pl.BlockSpec((1, rows, vw), lambda b, j: (b, j, 0)),
        ],
        out_shape=[
            jax.ShapeDtypeStruct((bsz, seq, kw), BF16),
            jax.ShapeDtypeStruct((bsz, ctx + seq, kw), BF16),
            jax.ShapeDtypeStruct((bsz, ctx + seq, vw), BF16),
        ],
        compiler_params=_cparams(("arbitrary", "arbitrary")),
        name="projections",
    )(xc, x, modc8, modl8, n1, cos_t, sin_t, wdq, qg, wuq, wdkv, kvg, wukv)


def _attn_kernel(q_ref, k_ref, v_ref, o_ref, *, chunk):
    q = q_ref[0]
    m = acc = None
    for c in range(k_ref.shape[1] // chunk):
        rows = slice(c * chunk, (c + 1) * chunk)
        s = lax.dot_general(q, k_ref[0, rows, :], (((1,), (1,)), ((), ())),
                            preferred_element_type=F32)
        m_c = jnp.max(s, axis=-1, keepdims=True)
        m_new = m_c if c == 0 else jnp.maximum(m, m_c)
        pv = _dot(jnp.exp2(s - m_new).astype(BF16), v_ref[0, rows, :])
        acc = pv if c == 0 else acc * jnp.exp2(m - m_new) + pv
        m = m_new
    o_ref[0] = (acc[:, :V_HEAD] / acc[:, V_HEAD:]).astype(BF16)


def _attention(q, k, v):
    bsz, seq, _ = q.shape
    keys = k.shape[1]
    rows = min(Q_TILE, seq)
    chunk = next(c for c in (KEY_CHUNK, 2 * LANES, keys) if keys % c == 0)
    return pl.pallas_call(
        functools.partial(_attn_kernel, chunk=chunk),
        grid=(bsz, N_HEADS, seq // rows),
        in_specs=[
            pl.BlockSpec((1, rows, HEAD_PAD), lambda b, h, i: (b, i, h)),
            pl.BlockSpec((1, keys, HEAD_PAD), lambda b, h, i: (b, 0, h)),
            pl.BlockSpec((1, keys, HEAD_PAD), lambda b, h, i: (b, 0, h)),
        ],
        out_specs=pl.BlockSpec((1, rows, V_HEAD), lambda b, h, i: (b, i, h)),
        out_shape=jax.ShapeDtypeStruct((bsz, seq, N_HEADS * V_HEAD), BF16),
        compiler_params=_cparams(("arbitrary", "arbitrary", "arbitrary")),
        name="attention",
    )(q, k, v)


def _oproj_kernel(o_ref, x_ref, mod_ref, wo_ref, n2_ref, rw_ref, rb_ref,
                  x1_ref, h2_ref, lg_ref, *, rows):
    mod = mod_ref[0]
    x1 = x_ref[0] + mod[2:3] * _dot(o_ref[0], wo_ref[...])
    _tail(x1, mod, n2_ref, rw_ref, rb_ref, x1_ref, h2_ref, lg_ref)


def _out_projection(o, x, mod8, wo, n2, rw, rb):
    bsz, seq, d = x.shape
    rows = SEQ_TILE
    const2 = lambda b, j: (0, 0)
    return pl.pallas_call(
        functools.partial(_oproj_kernel, rows=rows),
        grid=(bsz, seq // rows),
        in_specs=[
            pl.BlockSpec((1, rows, o.shape[2]), lambda b, j: (b, j, 0)),
            pl.BlockSpec((1, rows, d), lambda b, j: (b, j, 0)),
            pl.BlockSpec((1, SUBLANES, d), lambda b, j: (b, 0, 0)),
            pl.BlockSpec(wo.shape, const2),
            pl.BlockSpec((1, d), const2),
            pl.BlockSpec(rw.shape, const2),
            pl.BlockSpec((1, LANES), const2),
        ],
        out_specs=[
            pl.BlockSpec((1, rows, d), lambda b, j: (b, j, 0)),
            pl.BlockSpec((1, rows, d), lambda b, j: (b, j, 0)),
            pl.BlockSpec((1, rows, LANES), lambda b, j: (b, j, 0)),
        ],
        out_shape=[
            jax.ShapeDtypeStruct((bsz, seq, d), F32),
            jax.ShapeDtypeStruct((bsz, seq, d), BF16),
            jax.ShapeDtypeStruct((bsz, seq, LANES), F32),
        ],
        compiler_params=_cparams(("arbitrary", "arbitrary")),
        name="out_projection",
    )(o, x, mod8, wo, n2, rw, rb)


def _rope_cols(base):
    q = QK_ROPE // 4
    x1 = list(range(base, base + q)) + list(range(base + 2 * q, base + 3 * q))
    x2 = list(range(base + q, base + 2 * q)) + list(range(base + 3 * q, base + 4 * q))
    pad = [-1] * (LANES // 2 - 2 * q)
    return x1 + pad + x2 + pad


def _take_cols(w, cols):
    wz = jnp.concatenate([w, jnp.zeros((w.shape[0], 1), w.dtype)], axis=1)
    idx = jnp.asarray([c if c >= 0 else w.shape[1] for c in cols], jnp.int32)
    return jnp.take(wz, idx, axis=1)


def _rope_tables(seq):
    q = QK_ROPE // 4
    pos = jnp.arange(seq)
    inv = ROPE_BASE ** (-jnp.arange(0, QK_ROPE // 2, 2, dtype=F32) / (QK_ROPE // 2))
    ang = jnp.concatenate([(pos // GRID_W).astype(F32)[:, None] * inv,
                           (pos % GRID_W).astype(F32)[:, None] * inv], axis=1)
    pad1 = jnp.ones((seq, LANES // 2 - 2 * q), F32)
    pad0 = jnp.zeros((seq, LANES // 2 - 2 * q), F32)
    cos = jnp.concatenate([jnp.cos(ang), pad1, jnp.cos(ang), pad1], axis=1)
    sin = jnp.concatenate([-jnp.sin(ang), pad0, jnp.sin(ang), pad0], axis=1)
    return cos, sin


def _mod8(mod_rows):
    bsz, n = mod_rows.shape
    d = n // 6
    m = mod_rows.reshape(bsz, 6, d)
    return jnp.concatenate([m, jnp.zeros((bsz, SUBLANES - 6, d), F32)], axis=1)


def kernel(x, c, ctx, c_ctx, ada_w, ada_b, norm1_g, norm2_g, pool_w, pool_b, pool_scale,
           w_dq, q_norm_g, w_uq, w_dkv, kv_norm_g, w_ukv, w_o,
           router_w, router_b, w_gu, b_gu, w_down, b_down, final_g):
    bsz, seq, d = x.shape
    assert seq % SEQ_TILE == 0 and ctx.shape[1] % SEQ_TILE == 0 and seq % GRID_W == 0
    row = lambda a: a.reshape(1, -1)

    mod_rows = 2 * SUBLANES
    cvec = jnp.concatenate([c, c_ctx[None], jnp.zeros((mod_rows - bsz - 1, d), F32)], axis=0)
    mod = _modulation(cvec, ada_w, ada_b)
    modl = [_mod8(mod[i, :bsz]) for i in range(2)]
    modc = [_mod8(jnp.broadcast_to(mod[i, bsz:bsz + 1], (bsz, 6 * d))) for i in range(2)]

    n_exp = router_w.shape[2]
    rw = jnp.pad(router_w, ((0, 0), (0, 0), (0, LANES - n_exp)))
    rb = jnp.pad(router_b, ((0, 0), (0, LANES - n_exp)), constant_values=-jnp.inf)

    pw = pool_w[0].astype(BF16)
    args0 = (row(norm1_g[0]), row(norm2_g[0]), pw, row(pool_b[0]), row(pool_scale[0]),
             rw[0], row(rb[0]))
    x1, h2, lg = _pool_layer(x, modl[0], *args0)
    xc1, h2c, lgc = _pool_layer(ctx, modc[0], *args0)
    x, xc = _moe(0, [(x1, h2, lg, modl[0]), (xc1, h2c, lgc, modc[0])],
                 w_gu, b_gu, w_down, b_down, row(final_g), False)

    head = QK_NOPE + QK_ROPE
    q_cols = []
    for h in range(N_HEADS):
        q_cols += list(range(h * head, h * head + QK_NOPE)) + _rope_cols(h * head + QK_NOPE)
    kv_cols = list(range(KV_LORA)) + _rope_cols(KV_LORA)
    hk = QK_NOPE + V_HEAD
    ukv_cols = [h * hk + i for h in range(N_HEADS) for i in range(QK_NOPE)] + \
               [h * hk + QK_NOPE + i for h in range(N_HEADS) for i in range(V_HEAD)]
    wuq = _take_cols(w_uq[0], q_cols).astype(BF16)
    wdkv = _take_cols(w_dkv[0], kv_cols).astype(BF16)
    wukv = _take_cols(w_ukv[0], ukv_cols).astype(BF16)
    cos_t, sin_t = _rope_tables(seq)
    q, k, v = _projections(xc, x, modc[1], modl[1], row(norm1_g[1]), cos_t, sin_t,
                           w_dq[0].astype(BF16), row(q_norm_g[0]), wuq, wdkv,
                           row(kv_norm_g[0]), wukv)
    o = _attention(q, k, v)
    x1, h2, lg = _out_projection(o, x, modl[1], w_o[0].astype(BF16), row(norm2_g[1]),
                                 rw[1], row(rb[1]))
    (out,) = _moe(1, [(x1, h2, lg, modl[1])], w_gu, b_gu, w_down, b_down, row(final_g), True)
    return out
```

```python
import functools

import jax
import jax.numpy as jnp
from jax import lax
from jax.experimental import pallas as pl
from jax.experimental.pallas import tpu as pltpu

F32 = jnp.float32
BF16 = jnp.bfloat16

N_HEADS = 8
QK_NOPE = 128
QK_ROPE = 64
V_HEAD = 128
KV_LORA = 256
N_EXPERTS = 32
TOP_K = 4
POOL_WINDOWS = (2, 4, 8, 16)
GRID_W = 64
ROPE_BASE = 10000.0
ATTN_SCALE = (QK_NOPE + QK_ROPE) ** -0.5
Q_SCALE = ATTN_SCALE * 1.4426950408889634
SWIGLU_LIMIT = 7.0
SWIGLU_ALPHA = 1.702
EPS = 1e-6

LANES = 128
SUBLANES = 8
HEAD_PAD = 2 * LANES
VMEM_LIMIT = 56 * 1024 * 1024

SEQ_TILE = 256
EXPERT_BLOCK = 512
Q_TILE = 2048
KEY_CHUNK = 256
MOD_COLS = 1536
HALO = 8
SORT_ROWS = -(-(TOP_K * SEQ_TILE + N_EXPERTS * (SUBLANES - 1)) // (2 * LANES)) * (2 * LANES)
CHUNK_LANES = -(-(SORT_ROWS // SUBLANES + 1) // LANES) * LANES
CHUNK_UNROLL = 4


def _cparams(sem, vmem=VMEM_LIMIT):
    return pltpu.CompilerParams(dimension_semantics=sem, vmem_limit_bytes=vmem)


def _dot(a, b):
    return jnp.dot(a, b, preferred_element_type=F32)


def _dot3(a, b):
    ah = a.astype(BF16)
    al = (a - ah.astype(F32)).astype(BF16)
    bh = b.astype(BF16)
    bl = (b - bh.astype(F32)).astype(BF16)
    return _dot(ah, bh) + _dot(ah, bl) + _dot(al, bh)


def _rms(x, g):
    return x * lax.rsqrt(jnp.mean(x * x, axis=-1, keepdims=True) + EPS) * g


def _sigmoid(x):
    return 1.0 / (1.0 + jnp.exp(-x))


def _mod_kernel(c_ref, w_ref, b_ref, o_ref):
    a = c_ref[...]
    s = a * _sigmoid(a)
    o_ref[0] = _dot3(s, w_ref[0]) + b_ref[0]


def _modulation(cvec, ada_w, ada_b):
    depth, d, n = ada_w.shape
    rows = cvec.shape[0]
    return pl.pallas_call(
        _mod_kernel,
        grid=(depth, n // MOD_COLS),
        in_specs=[
            pl.BlockSpec((rows, d), lambda i, j: (0, 0)),
            pl.BlockSpec((1, d, MOD_COLS), lambda i, j: (i, 0, j)),
            pl.BlockSpec((1, 1, MOD_COLS), lambda i, j: (i, 0, j)),
        ],
        out_specs=pl.BlockSpec((1, rows, MOD_COLS), lambda i, j: (i, 0, j)),
        out_shape=jax.ShapeDtypeStruct((depth, rows, n), F32),
        compiler_params=_cparams(("arbitrary", "arbitrary")),
        name="modulation",
    )(cvec, ada_w, ada_b.reshape(depth, 1, n))


def _route_tile(work, first, cin_ref, pos_ref, post_ref, gt_ref, chunk_ref, span_ref, carry_ref):
    rows = work.shape[0]
    lane = lax.broadcasted_iota(jnp.int32, work.shape, 1).astype(F32)
    tops, hots = [], []
    for _ in range(TOP_K):
        m = jnp.max(work, axis=1, keepdims=True)
        idx = jnp.min(jnp.where(work == m, lane, float(LANES)), axis=1, keepdims=True)
        hot = lane == idx
        work = jnp.where(hot, -jnp.inf, work)
        tops.append(m)
        hots.append(hot)
    chosen = jnp.zeros(work.shape, F32)
    for hot in hots:
        chosen = chosen + jnp.where(hot, 1.0, 0.0)
    count = jnp.broadcast_to(jnp.sum(chosen, axis=0, keepdims=True), carry_ref.shape)
    seg_rows = jnp.floor((count + (SUBLANES - 1.0)) * (1.0 / SUBLANES)) * SUBLANES

    def lane_prefix(v):
        shift = 1
        while shift < N_EXPERTS:
            v = v + pltpu.roll(v, shift, axis=1)
            shift *= 2
        return v

    @pl.when(first)
    def _():
        carry_ref[...] = cin_ref[...]

    r = lax.broadcasted_iota(jnp.int32, (rows, rows), 0)
    c = lax.broadcasted_iota(jnp.int32, (rows, rows), 1)
    earlier = jnp.where(c < r, 1.0, 0.0).astype(BF16)
    run_end = lane_prefix(seg_rows)
    run_start = run_end - seg_rows
    before = _dot(earlier, chosen.astype(BF16)) + run_start[0:1]
    exps = [jnp.exp(t - tops[0]) for t in tops]
    den = exps[0] + exps[1] + exps[2] + exps[3]
    col = lax.broadcasted_iota(jnp.int32, (rows, TOP_K), 1)
    wide = lax.broadcasted_iota(jnp.int32, (rows, LANES), 1)
    gt = jnp.zeros((rows, TOP_K), F32)
    ps = jnp.zeros((rows, TOP_K), F32)
    ps_wide = jnp.zeros((rows, LANES), F32)
    for k in range(TOP_K):
        pos_k = jnp.sum(jnp.where(hots[k], before, 0.0), axis=1, keepdims=True)
        gt = jnp.where(col == k, exps[k] / den, gt)
        ps = jnp.where(col == k, pos_k, ps)
        ps_wide = jnp.where(wide == k, pos_k, ps_wide)
    pos_ref[...] = ps.astype(jnp.int32)
    post_ref[...] = jnp.transpose(ps_wide)[0:SUBLANES].astype(jnp.int32)
    gt_ref[...] = gt

    square = (LANES, LANES)
    srow = lax.broadcasted_iota(jnp.int32, square, 0)
    packed = jnp.where(srow == 0, jnp.broadcast_to(run_start[0:1], square),
                       jnp.where(srow == 1, jnp.broadcast_to(run_end[0:1], square),
                                 jnp.broadcast_to((carry_ref[...] - run_start)[0:1], square)))
    cols = jnp.transpose(packed)
    start_c, end_c, shift_c = cols[:, 0:1], cols[:, 1:2], cols[:, 2:3]
    erow = lax.broadcasted_iota(jnp.int32, (LANES, CHUNK_LANES), 0)
    row0 = (lax.broadcasted_iota(jnp.int32, (LANES, CHUNK_LANES), 1) * SUBLANES).astype(F32)
    holds = (erow < N_EXPERTS) & (start_c <= row0) & (row0 < end_c)
    rel = jnp.sum(jnp.where(holds, shift_c, 0.0), axis=0, keepdims=True) + row0[0:1]
    eid = jnp.sum(jnp.where(holds, erow.astype(F32), 0.0), axis=0, keepdims=True)
    n_chunks = jnp.sum(jnp.where(holds, 1.0, 0.0), keepdims=True)
    crow = lax.broadcasted_iota(jnp.int32, (SUBLANES, CHUNK_LANES), 0)
    chunk_ref[0] = jnp.where(crow == 0, rel, jnp.where(crow == 1, eid, n_chunks)).astype(jnp.int32)

    total = carry_ref[...] + seg_rows
    carry_ref[...] = total
    sub = lax.broadcasted_iota(jnp.int32, carry_ref.shape, 0)
    blocks = jnp.floor((total + (EXPERT_BLOCK - 1.0)) * (1.0 / EXPERT_BLOCK))
    ends = lane_prefix(blocks) * EXPERT_BLOCK
    span_ref[...] = jnp.where(sub == 0, ends, jnp.where(sub == 1, ends - blocks * EXPERT_BLOCK,
                                                        0.0)).astype(jnp.int32)


def _route_specs(tiles, per, rows):
    tile = lambda b, j: b * per + j
    tok = pl.BlockSpec((rows, TOP_K), lambda b, j: (tile(b, j), 0))
    keep = pl.BlockSpec((SUBLANES, LANES), lambda b, j: (0, 0))
    specs = [tok, pl.BlockSpec((SUBLANES, rows), lambda b, j: (0, tile(b, j))), tok,
             pl.BlockSpec((1, SUBLANES, CHUNK_LANES), lambda b, j: (tile(b, j), 0, 0)), keep, keep]
    t = tiles * rows
    shapes = [jax.ShapeDtypeStruct((t, TOP_K), jnp.int32),
              jax.ShapeDtypeStruct((SUBLANES, t), jnp.int32),
              jax.ShapeDtypeStruct((t, TOP_K), F32),
              jax.ShapeDtypeStruct((tiles, SUBLANES, CHUNK_LANES), jnp.int32),
              jax.ShapeDtypeStruct((SUBLANES, LANES), jnp.int32),
              jax.ShapeDtypeStruct((SUBLANES, LANES), F32)]
    return specs, shapes


def _tail(x1, mod, first, n2_ref, rw_ref, rb_ref, cin_ref, x1_ref, h2_ref, route_refs):
    sh2, s2 = mod[3:4], mod[4:5]
    h2 = _rms(x1, n2_ref[...]) * (1.0 + s2) + sh2
    x1_ref[0] = x1
    h2_ref[0] = h2.astype(BF16)
    _route_tile(_dot3(h2, rw_ref[...]) + rb_ref[...], first, cin_ref, *route_refs)


def _pool_kernel(x_ref, mod_ref, n1_ref, n2_ref, pw_ref, pb_ref, ps_ref, rw_ref, rb_ref, cin_ref,
                 x1_ref, h2_ref, *route_refs, rows, seq):
    j = pl.program_id(1)
    first = (pl.program_id(0) == 0) & (j == 0)
    start = pl.multiple_of(j * rows, rows)
    prev0 = pl.multiple_of(jnp.maximum(start - HALO, 0), HALO)
    next0 = pl.multiple_of(jnp.minimum(start + rows, seq - HALO), HALO)
    xm = x_ref[0, pl.ds(start, rows), :]
    xe = jnp.concatenate(
        [x_ref[0, pl.ds(prev0, HALO), :], xm, x_ref[0, pl.ds(next0, HALO), :]], axis=0)
    mod = mod_ref[0]
    sh1, s1, g1 = mod[0:1], mod[1:2], mod[2:3]
    h = _rms(xe, n1_ref[...]) * (1.0 + s1) + sh1
    pos = start - HALO + lax.broadcasted_iota(jnp.int32, (rows + 2 * HALO, 1), 0)
    hz = jnp.where((pos >= 0) & (pos < seq), h, 0.0)
    tpos = start + lax.broadcasted_iota(jnp.int32, (rows, 1), 0)
    group = hz.shape[1] // len(POOL_WINDOWS)
    ys = []
    for g, w in enumerate(POOL_WINDOWS):
        half = w // 2
        hg = hz[:, g * group:(g + 1) * group]
        tot = hg[HALO - half:HALO - half + rows]
        for o in range(1 - half, half):
            tot = tot + hg[HALO + o:HALO + o + rows]
        cnt = (jnp.minimum(tpos + half, seq) - jnp.maximum(tpos - half, 0)).astype(F32)
        dlt = tot / cnt - hg[HALO:HALO + rows]
        ys.append(_dot(dlt.astype(BF16), pw_ref[g]))
    y = (jnp.concatenate(ys, axis=1) + pb_ref[...]) * ps_ref[...]
    x1 = xm + g1 * y
    _tail(x1, mod, first, n2_ref, rw_ref, rb_ref, cin_ref, x1_ref, h2_ref, route_refs)


def _pool_layer(x, mod8, n1, n2, pw, pb, ps, rw, rb, carry_in):
    bsz, seq, d = x.shape
    rows = SEQ_TILE
    per = seq // rows
    grid = (bsz, per)
    const2 = lambda b, j: (0, 0)
    route_specs, route_shapes = _route_specs(bsz * per, per, rows)
    return pl.pallas_call(
        functools.partial(_pool_kernel, rows=rows, seq=seq),
        grid=grid,
        in_specs=[
            pl.BlockSpec((1, seq, d), lambda b, j: (b, 0, 0)),
            pl.BlockSpec((1, SUBLANES, d), lambda b, j: (b, 0, 0)),
            pl.BlockSpec((1, d), const2),
            pl.BlockSpec((1, d), const2),
            pl.BlockSpec(pw.shape, lambda b, j: (0, 0, 0)),
            pl.BlockSpec((1, d), const2),
            pl.BlockSpec((1, d), const2),
            pl.BlockSpec(rw.shape, const2),
            pl.BlockSpec((1, LANES), const2),
            pl.BlockSpec((SUBLANES, LANES), const2),
        ],
        out_specs=[
            pl.BlockSpec((1, rows, d), lambda b, j: (b, j, 0)),
            pl.BlockSpec((1, rows, d), lambda b, j: (b, j, 0)),
        ] + route_specs,
        out_shape=[
            jax.ShapeDtypeStruct((bsz, seq, d), F32),
            jax.ShapeDtypeStruct((bsz, seq, d), BF16),
        ] + route_shapes,
        compiler_params=_cparams(("arbitrary", "arbitrary")),
        name="pool_layer",
    )(x, mod8, n1, n2, pw, pb, ps, rw, rb, carry_in)


def _chunk_copies(go, list_ref, make):
    n = list_ref[CHUNK_LANES - 1]
    groups = n // CHUNK_UNROLL

    def one(c):
        go(make(pl.multiple_of(c * SUBLANES, SUBLANES), pl.multiple_of(list_ref[c], SUBLANES)))

    def group(g, carry):
        for u in range(CHUNK_UNROLL):
            one(g * CHUNK_UNROLL + u)
        return carry

    def rest(c, carry):
        one(c)
        return carry

    lax.fori_loop(0, groups, group, 0)
    lax.fori_loop(groups * CHUNK_UNROLL, n, rest, 0)


def _dispatch_kernel(ends_ref, nu_ref, list_ref, prev_ref, post_ref, *refs, tiles, n_blocks):
    n_src = len(tiles)
    srcs, (xs_ref, sorted_ref, zeros, sem, zsem) = refs[:n_src], refs[n_src:]
    i = pl.program_id(0)
    blk = EXPERT_BLOCK

    @pl.when(i == 0)
    def _():
        zeros[...] = jnp.zeros_like(zeros)

        def fill(b):
            return pltpu.make_async_copy(zeros, xs_ref.at[pl.ds(b * blk, blk)], zsem)

        def expert_tails(go):
            for e in range(N_EXPERTS):
                lo = ends_ref[e - 1] if e else 0

                @pl.when(ends_ref[e] > lo)
                def _(e=e):
                    go(fill(ends_ref[e] // blk - 1))

        def unused(go):
            def body(b, carry):
                go(fill(b))
                return carry
            lax.fori_loop(nu_ref[0], n_blocks, body, 0)

        expert_tails(lambda cp: cp.start())
        unused(lambda cp: cp.start())
        expert_tails(lambda cp: cp.wait())
        unused(lambda cp: cp.wait())

    h = srcs[0][...]
    first = tiles[0]
    for src, n in zip(srcs[1:], tiles[1:]):
        h = jnp.where(i >= first, src[...], h)
        first += n

    pos = post_ref[...]
    j = lax.broadcasted_iota(jnp.int32, (SORT_ROWS, h.shape[0]), 0)
    onehot = jnp.zeros(j.shape, F32)
    for k in range(TOP_K):
        onehot = jnp.where(j == pos[k:k + 1], 1.0, onehot)
    slot = i % 2
    sorted_ref[slot] = _dot(onehot.astype(BF16), h)

    def make(buf):
        def build(local, glob):
            return pltpu.make_async_copy(sorted_ref.at[buf, pl.ds(local, SUBLANES)],
                                         xs_ref.at[pl.ds(glob, SUBLANES)], sem.at[buf])
        return build

    _chunk_copies(lambda cp: cp.start(), list_ref, make(slot))

    @pl.when(i > 0)
    def _():
        _chunk_copies(lambda cp: cp.wait(), prev_ref, make(1 - slot))

    @pl.when(i == pl.num_programs(0) - 1)
    def _():
        _chunk_copies(lambda cp: cp.wait(), list_ref, make(slot))


def _dispatch(chunks, ends, n_used, pos_t, sources, n_blocks):
    rows = SEQ_TILE
    d = sources[0].shape[1]
    tiles = tuple(s.shape[0] // rows for s in sources)
    firsts = [sum(tiles[:k]) for k in range(len(tiles))]

    def src_spec(first, n):
        return pl.BlockSpec((rows, d), lambda i, *_: (jnp.clip(i - first, 0, n - 1), 0))

    grid_spec = pltpu.PrefetchScalarGridSpec(
        num_scalar_prefetch=2,
        grid=(sum(tiles),),
        in_specs=[pl.BlockSpec((CHUNK_LANES,), lambda i, *_: (i,), memory_space=pltpu.SMEM),
                  pl.BlockSpec((CHUNK_LANES,), lambda i, *_: (jnp.maximum(i - 1, 0),),
                               memory_space=pltpu.SMEM),
                  pl.BlockSpec((SUBLANES, rows), lambda i, *_: (0, i))]
        + [src_spec(f, n) for f, n in zip(firsts, tiles)],
        out_specs=pl.BlockSpec(memory_space=pl.ANY),
        scratch_shapes=[pltpu.VMEM((2, SORT_ROWS, d), F32), pltpu.VMEM((EXPERT_BLOCK, d), F32),
                        pltpu.SemaphoreType.DMA((2,)), pltpu.SemaphoreType.DMA],
    )
    return pl.pallas_call(
        functools.partial(_dispatch_kernel, tiles=tiles, n_blocks=n_blocks),
        grid_spec=grid_spec,
        out_shape=jax.ShapeDtypeStruct((n_blocks * EXPERT_BLOCK, d), F32),
        compiler_params=_cparams(("arbitrary",)),
        name="dispatch",
    )(ends, n_used, chunks, chunks, pos_t, *sources)


def _experts_kernel(be_ref, nu_ref, hf_ref, xs_ref, wgu_ref, bgu_ref, wd_ref, bd_ref, ys_ref,
                    wgu_bf, wd_bf, *, rows, dff):
    b = pl.program_id(0)
    prev = be_ref[jnp.maximum(b - 1, 0)]
    fresh = (b == 0) | (be_ref[b] != prev)
    live = b < nu_ref[0]
    half = hf_ref[b] == 1

    @pl.when(live & fresh)
    def _():
        wgu_bf[...] = wgu_ref[0, 0].astype(BF16)
        wd_bf[...] = wd_ref[0, 0].astype(BF16)

    def ffn(x):
        gu = _dot(x.astype(BF16), wgu_bf[...]) + bgu_ref[0, 0]
        gate = jnp.minimum(gu[:, :dff], SWIGLU_LIMIT)
        up = jnp.clip(gu[:, dff:], -SWIGLU_LIMIT, SWIGLU_LIMIT)
        act = (up + 1.0) * (gate * _sigmoid(SWIGLU_ALPHA * gate))
        return _dot(act.astype(BF16), wd_bf[...]) + bd_ref[0, 0]

    @pl.when(live & jnp.logical_not(half))
    def _():
        ys_ref[...] = ffn(xs_ref[...])

    @pl.when(live & half)
    def _():
        ys_ref[:rows // 2] = ffn(xs_ref[:rows // 2])
        ys_ref[rows // 2:] = jnp.zeros((rows // 2, ys_ref.shape[1]), F32)

    @pl.when(jnp.logical_not(live))
    def _():
        ys_ref[...] = jnp.zeros_like(ys_ref)


def _experts(layer, block_expert, n_used, half, xs, w_gu, b_gu, w_down, b_down):
    n_rows = xs.shape[0]
    rows = EXPERT_BLOCK
    n_blocks = n_rows // rows
    depth, e, d, dff2 = w_gu.shape
    dff = dff2 // 2
    blk = lambda b, be, nu, hf: (jnp.minimum(b, nu[0] - 1), 0)
    per_expert = lambda b, be, nu, hf: (layer, be[b], 0, 0)
    grid_spec = pltpu.PrefetchScalarGridSpec(
        num_scalar_prefetch=3,
        grid=(n_blocks,),
        in_specs=[
            pl.BlockSpec((rows, d), blk),
            pl.BlockSpec((1, 1, d, dff2), per_expert),
            pl.BlockSpec((1, 1, 1, dff2), per_expert),
            pl.BlockSpec((1, 1, dff, d), per_expert),
            pl.BlockSpec((1, 1, 1, d), per_expert),
        ],
        out_specs=pl.BlockSpec((rows, d), lambda b, be, nu, hf: (b, 0)),
        scratch_shapes=[pltpu.VMEM((d, dff2), BF16), pltpu.VMEM((dff, d), BF16)],
    )
    return pl.pallas_call(
        functools.partial(_experts_kernel, rows=rows, dff=dff),
        grid_spec=grid_spec,
        out_shape=jax.ShapeDtypeStruct((n_rows, d), F32),
        compiler_params=_cparams(("arbitrary",)),
        name="experts",
    )(block_expert, n_used, half, xs, w_gu, b_gu.reshape(depth, e, 1, dff2), w_down,
      b_down.reshape(depth, e, 1, d))


def _combine_kernel(list_ref, next_ref, ys_ref, pos_ref, gt_ref, x1_ref, mod_ref, fg_ref, out_ref,
                    sorted_ref, sem, *, per, final):
    step = pl.program_id(0) * per + pl.program_id(1)
    steps = pl.num_programs(0) * per
    slot = step % 2

    def make(buf):
        def build(local, glob):
            return pltpu.make_async_copy(ys_ref.at[pl.ds(glob, SUBLANES)],
                                         sorted_ref.at[buf, pl.ds(local, SUBLANES)], sem.at[buf])
        return build

    @pl.when(step == 0)
    def _():
        sorted_ref[...] = jnp.zeros_like(sorted_ref)
        _chunk_copies(lambda cp: cp.start(), list_ref, make(slot))

    @pl.when(step + 1 < steps)
    def _():
        _chunk_copies(lambda cp: cp.start(), next_ref, make(1 - slot))

    _chunk_copies(lambda cp: cp.wait(), list_ref, make(slot))

    pos = pos_ref[...]
    gates = gt_ref[...]
    lane = lax.broadcasted_iota(jnp.int32, (pos.shape[0], SORT_ROWS), 1)
    weights = jnp.zeros(lane.shape, F32)
    for k in range(TOP_K):
        weights = jnp.where(lane == pos[:, k:k + 1], gates[:, k:k + 1], weights)
    y = _dot(weights.astype(BF16), sorted_ref[slot].astype(BF16))
    x2 = x1_ref[0] + mod_ref[0][5:6] * y
    if final:
        x2 = _rms(x2, fg_ref[...])
    out_ref[0] = x2


def _combine(chunks, tile0, pos, gates, ys, x1, mod8, final_g, final):
    bsz, seq, d = x1.shape
    rows = SEQ_TILE
    per = seq // rows
    last = tile0 + bsz * per - 1
    tok = pl.BlockSpec((rows, TOP_K), lambda b, j: (tile0 + b * per + j, 0))
    return pl.pallas_call(
        functools.partial(_combine_kernel, per=per, final=final),
        grid=(bsz, per),
        in_specs=[
            pl.BlockSpec((CHUNK_LANES,), lambda b, j: (tile0 + b * per + j,),
                         memory_space=pltpu.SMEM),
            pl.BlockSpec((CHUNK_LANES,), lambda b, j: (jnp.minimum(tile0 + b * per + j + 1, last),),
                         memory_space=pltpu.SMEM),
            pl.BlockSpec(memory_space=pl.ANY),
            tok, tok,
            pl.BlockSpec((1, rows, d), lambda b, j: (b, j, 0)),
            pl.BlockSpec((1, SUBLANES, d), lambda b, j: (b, 0, 0)),
            pl.BlockSpec((1, d), lambda b, j: (0, 0)),
        ],
        out_specs=pl.BlockSpec((1, rows, d), lambda b, j: (b, j, 0)),
        out_shape=jax.ShapeDtypeStruct((bsz, seq, d), F32),
        scratch_shapes=[pltpu.VMEM((2, SORT_ROWS, d), F32), pltpu.SemaphoreType.DMA((2,))],
        compiler_params=_cparams(("arbitrary", "arbitrary")),
        name="combine",
    )(chunks, chunks, ys, pos, gates, x1, mod8, final_g)


def _moe(layer, streams, w_gu, b_gu, w_down, b_down, final_g, final):
    d = streams[0][0].shape[2]
    routes = [s[2] for s in streams]
    pos = jnp.concatenate([r[0] for r in routes], axis=0)
    pos_t = jnp.concatenate([r[1] for r in routes], axis=1)
    gates = jnp.concatenate([r[2] for r in routes], axis=0)
    chunk = jnp.concatenate([r[3] for r in routes], axis=0)
    span = routes[-1][4]
    t = pos.shape[0]
    tiles = chunk.shape[0]
    ends = span[0, :N_EXPERTS]
    starts = span[1, :N_EXPERTS]
    lane = jnp.arange(CHUNK_LANES, dtype=jnp.int32)
    hot = chunk[:, 1, :, None] == jnp.arange(N_EXPERTS, dtype=jnp.int32)
    base = jnp.sum(jnp.where(hot, starts, 0), axis=-1)
    chunks = jnp.where(lane == CHUNK_LANES - 1, chunk[:, 2], chunk[:, 0] + base).reshape(-1)
    blk = EXPERT_BLOCK
    max_rows = t * TOP_K + tiles * N_EXPERTS * (SUBLANES - 1)
    n_blocks = -(-max_rows // blk) + N_EXPERTS
    n_used = ends[N_EXPERTS - 1] // blk
    block_start = jnp.arange(n_blocks, dtype=jnp.int32) * blk
    bexp = jnp.sum((ends[None, :] <= block_start[:, None]).astype(jnp.int32), axis=1)
    last = jnp.sum((ends <= (n_used - 1) * blk).astype(jnp.int32))
    bexp = jnp.where(jnp.arange(n_blocks) < n_used, bexp, last)
    used = routes[-1][5][0, :N_EXPERTS].astype(jnp.int32)
    mine = bexp[:, None] == jnp.arange(N_EXPERTS, dtype=jnp.int32)
    left = jnp.sum(jnp.where(mine, starts + used, 0), axis=-1) - block_start
    half = (left <= blk // 2).astype(jnp.int32)
    sources = [s[1].reshape(-1, d) for s in streams]
    n_used = n_used.reshape(1)
    xs = _dispatch(chunks, ends, n_used, pos_t, sources, n_blocks)
    ys = _experts(layer, bexp, n_used, half, xs, w_gu, b_gu, w_down, b_down)
    outs, tile0 = [], 0
    for x1, _, _, mod8 in streams:
        outs.append(_combine(chunks, tile0, pos, gates, ys, x1, mod8, final_g, final))
        tile0 += x1.shape[0] * x1.shape[1] // SEQ_TILE
    return outs


def _rope(x, cos, sin):
    return x * cos + pltpu.roll(x, LANES // 2, axis=1) * sin


def _proj_kernel(xc_ref, x_ref, modc_ref, modl_ref, n1_ref, cos_ref, sin_ref,
                 wdq_ref, qg_ref, wuq_ref, wdkv_ref, kvg_ref, wukv_ref,
                 q_ref, k_ref, v_ref, *, ctx_tiles):
    j = pl.program_id(1)
    is_ctx = j < ctx_tiles
    xin = jnp.where(is_ctx, xc_ref[0], x_ref[0])
    mod = jnp.where(is_ctx, modc_ref[0], modl_ref[0])
    sh1, s1 = mod[0:1], mod[1:2]
    hb = (_rms(xin, n1_ref[...]) * (1.0 + s1) + sh1).astype(BF16)
    cos = jnp.where(is_ctx, 1.0, cos_ref[...])
    sin = jnp.where(is_ctx, 0.0, sin_ref[...])

    kva = _dot(hb, wdkv_ref[...])
    kpe = _rope(kva[:, KV_LORA:], cos, sin).astype(BF16)
    kv = _dot(_rms(kva[:, :KV_LORA], kvg_ref[...]).astype(BF16), wukv_ref[...])
    nope_all = N_HEADS * QK_NOPE
    ones = jnp.ones((kv.shape[0], V_HEAD), BF16)
    for h in range(N_HEADS):
        k_ref[0, :, h * HEAD_PAD:h * HEAD_PAD + QK_NOPE] = \
            kv[:, h * QK_NOPE:(h + 1) * QK_NOPE].astype(BF16)
        k_ref[0, :, h * HEAD_PAD + QK_NOPE:(h + 1) * HEAD_PAD] = kpe
        v_ref[0, :, h * HEAD_PAD:h * HEAD_PAD + V_HEAD] = \
            kv[:, nope_all + h * V_HEAD:nope_all + (h + 1) * V_HEAD].astype(BF16)
        v_ref[0, :, h * HEAD_PAD + V_HEAD:(h + 1) * HEAD_PAD] = ones

    @pl.when(jnp.logical_not(is_ctx))
    def _():
        qa = _dot(hb, wdq_ref[...])
        q = _dot(_rms(qa, qg_ref[...]).astype(BF16), wuq_ref[...]) * Q_SCALE
        for h in range(N_HEADS):
            lo = h * HEAD_PAD
            q_ref[0, :, lo:lo + QK_NOPE] = q[:, lo:lo + QK_NOPE].astype(BF16)
            q_ref[0, :, lo + QK_NOPE:lo + HEAD_PAD] = \
                _rope(q[:, lo + QK_NOPE:lo + HEAD_PAD], cos, sin).astype(BF16)


def _projections(xc, x, modc8, modl8, n1, cos_t, sin_t, wdq, qg, wuq, wdkv, kvg, wukv):
    bsz, seq, d = x.shape
    ctx = xc.shape[1]
    rows = SEQ_TILE
    ctx_tiles = ctx // rows
    lat = lambda b, j: (b, jnp.maximum(j - ctx_tiles, 0), 0)
    full = lambda a: pl.BlockSpec(a.shape, lambda b, j: (0,) * a.ndim)
    kw = N_HEADS * HEAD_PAD
    vw = N_HEADS * HEAD_PAD
    return pl.pallas_call(
        functools.partial(_proj_kernel, ctx_tiles=ctx_tiles),
        grid=(bsz, (ctx + seq) // rows),
        in_specs=[
            pl.BlockSpec((1, rows, d), lambda b, j: (b, jnp.minimum(j, ctx_tiles - 1), 0)),
            pl.BlockSpec((1, rows, d), lat),
            pl.BlockSpec((1, SUBLANES, d), lambda b, j: (b, 0, 0)),
            pl.BlockSpec((1, SUBLANES, d), lambda b, j: (b, 0, 0)),
            full(n1),
            pl.BlockSpec((rows, LANES), lambda b, j: (jnp.maximum(j - ctx_tiles, 0), 0)),
            pl.BlockSpec((rows, LANES), lambda b, j: (jnp.maximum(j - ctx_tiles, 0), 0)),
            full(wdq), full(qg), full(wuq), full(wdkv), full(kvg), full(wukv),
        ],
        out_specs=[
            pl.BlockSpec((1, rows, kw), lat),
            pl.BlockSpec((1, rows, kw), lambda b, j: (b, j, 0)),
            pl.BlockSpec((1, rows, vw), lambda b, j: (b, j, 0)),
        ],
        out_shape=[
            jax.ShapeDtypeStruct((bsz, seq, kw), BF16),
            jax.ShapeDtypeStruct((bsz, ctx + seq, kw), BF16),
            jax.ShapeDtypeStruct((bsz, ctx + seq, vw), BF16),
        ],
        compiler_params=_cparams(("arbitrary", "arbitrary")),
        name="projections",
    )(xc, x, modc8, modl8, n1, cos_t, sin_t, wdq, qg, wuq, wdkv, kvg, wukv)


def _attn_kernel(q_ref, k_ref, v_ref, o_ref, *, chunk):
    q = q_ref[0]
    m = acc = None
    for c in range(k_ref.shape[1] // chunk):
        rows = slice(c * chunk, (c + 1) * chunk)
        s = lax.dot_general(q, k_ref[0, rows, :], (((1,), (1,)), ((), ())),
                            preferred_element_type=F32)
        m_c = jnp.max(s, axis=-1, keepdims=True)
        m_new = m_c if c == 0 else jnp.maximum(m, m_c)
        pv = _dot(jnp.exp2(s - m_new).astype(BF16), v_ref[0, rows, :])
        acc = pv if c == 0 else acc * jnp.exp2(m - m_new) + pv
        m = m_new
    o_ref[0] = (acc[:, :V_HEAD] / acc[:, V_HEAD:]).astype(BF16)


def _attention(q, k, v):
    bsz, seq, _ = q.shape
    keys = k.shape[1]
    rows = min(Q_TILE, seq)
    chunk = next(c for c in (KEY_CHUNK, 2 * LANES, keys) if keys % c == 0)
    return pl.pallas_call(
        functools.partial(_attn_kernel, chunk=chunk),
        grid=(bsz, N_HEADS, seq // rows),
        in_specs=[
            pl.BlockSpec((1, rows, HEAD_PAD), lambda b, h, i: (b, i, h)),
            pl.BlockSpec((1, keys, HEAD_PAD), lambda b, h, i: (b, 0, h)),
            pl.BlockSpec((1, keys, HEAD_PAD), lambda b, h, i: (b, 0, h)),
        ],
        out_specs=pl.BlockSpec((1, rows, V_HEAD), lambda b, h, i: (b, i, h)),
        out_shape=jax.ShapeDtypeStruct((bsz, seq, N_HEADS * V_HEAD), BF16),
        compiler_params=_cparams(("arbitrary", "arbitrary", "arbitrary")),
        name="attention",
    )(q, k, v)


def _oproj_kernel(o_ref, x_ref, mod_ref, wo_ref, n2_ref, rw_ref, rb_ref, cin_ref,
                  x1_ref, h2_ref, *route_refs):
    first = (pl.program_id(0) == 0) & (pl.program_id(1) == 0)
    mod = mod_ref[0]
    x1 = x_ref[0] + mod[2:3] * _dot(o_ref[0], wo_ref[...])
    _tail(x1, mod, first, n2_ref, rw_ref, rb_ref, cin_ref, x1_ref, h2_ref, route_refs)


def _out_projection(o, x, mod8, wo, n2, rw, rb, carry_in):
    bsz, seq, d = x.shape
    rows = SEQ_TILE
    per = seq // rows
    const2 = lambda b, j: (0, 0)
    route_specs, route_shapes = _route_specs(bsz * per, per, rows)
    return pl.pallas_call(
        _oproj_kernel,
        grid=(bsz, per),
        in_specs=[
            pl.BlockSpec((1, rows, o.shape[2]), lambda b, j: (b, j, 0)),
            pl.BlockSpec((1, rows, d), lambda b, j: (b, j, 0)),
            pl.BlockSpec((1, SUBLANES, d), lambda b, j: (b, 0, 0)),
            pl.BlockSpec(wo.shape, const2),
            pl.BlockSpec((1, d), const2),
            pl.BlockSpec(rw.shape, const2),
            pl.BlockSpec((1, LANES), const2),
            pl.BlockSpec((SUBLANES, LANES), const2),
        ],
        out_specs=[
            pl.BlockSpec((1, rows, d), lambda b, j: (b, j, 0)),
            pl.BlockSpec((1, rows, d), lambda b, j: (b, j, 0)),
        ] + route_specs,
        out_shape=[
            jax.ShapeDtypeStruct((bsz, seq, d), F32),
            jax.ShapeDtypeStruct((bsz, seq, d), BF16),
        ] + route_shapes,
        compiler_params=_cparams(("arbitrary", "arbitrary")),
        name="out_projection",
    )(o, x, mod8, wo, n2, rw, rb, carry_in)


def _rope_cols(base):
    q = QK_ROPE // 4
    x1 = list(range(base, base + q)) + list(range(base + 2 * q, base + 3 * q))
    x2 = list(range(base + q, base + 2 * q)) + list(range(base + 3 * q, base + 4 * q))
    pad = [-1] * (LANES // 2 - 2 * q)
    return x1 + pad + x2 + pad


def _take_cols(w, cols):
    wz = jnp.concatenate([w, jnp.zeros((w.shape[0], 1), w.dtype)], axis=1)
    idx = jnp.asarray([c if c >= 0 else w.shape[1] for c in cols], jnp.int32)
    return jnp.take(wz, idx, axis=1)


def _rope_tables(seq):
    q = QK_ROPE // 4
    pos = jnp.arange(seq)
    inv = ROPE_BASE ** (-jnp.arange(0, QK_ROPE // 2, 2, dtype=F32) / (QK_ROPE // 2))
    ang = jnp.concatenate([(pos // GRID_W).astype(F32)[:, None] * inv,
                           (pos % GRID_W).astype(F32)[:, None] * inv], axis=1)
    pad1 = jnp.ones((seq, LANES // 2 - 2 * q), F32)
    pad0 = jnp.zeros((seq, LANES // 2 - 2 * q), F32)
    cos = jnp.concatenate([jnp.cos(ang), pad1, jnp.cos(ang), pad1], axis=1)
    sin = jnp.concatenate([-jnp.sin(ang), pad0, jnp.sin(ang), pad0], axis=1)
    return cos, sin


def _mod8(mod_rows):
    bsz, n = mod_rows.shape
    d = n // 6
    m = mod_rows.reshape(bsz, 6, d)
    return jnp.concatenate([m, jnp.zeros((bsz, SUBLANES - 6, d), F32)], axis=1)


def kernel(x, c, ctx, c_ctx, ada_w, ada_b, norm1_g, norm2_g, pool_w, pool_b, pool_scale,
           w_dq, q_norm_g, w_uq, w_dkv, kv_norm_g, w_ukv, w_o,
           router_w, router_b, w_gu, b_gu, w_down, b_down, final_g):
    bsz, seq, d = x.shape
    assert seq % SEQ_TILE == 0 and ctx.shape[1] % SEQ_TILE == 0 and seq % GRID_W == 0
    row = lambda a: a.reshape(1, -1)

    mod_rows = 2 * SUBLANES
    cvec = jnp.concatenate([c, c_ctx[None], jnp.zeros((mod_rows - bsz - 1, d), F32)], axis=0)
    mod = _modulation(cvec, ada_w, ada_b)
    modl = [_mod8(mod[i, :bsz]) for i in range(2)]
    modc = [_mod8(jnp.broadcast_to(mod[i, bsz:bsz + 1], (bsz, 6 * d))) for i in range(2)]

    n_exp = router_w.shape[2]
    rw = jnp.pad(router_w, ((0, 0), (0, 0), (0, LANES - n_exp)))
    rb = jnp.pad(router_b, ((0, 0), (0, LANES - n_exp)), constant_values=-jnp.inf)

    pw = pool_w[0].astype(BF16)
    args0 = (row(norm1_g[0]), row(norm2_g[0]), pw, row(pool_b[0]), row(pool_scale[0]),
             rw[0], row(rb[0]))
    no_rows = jnp.zeros((SUBLANES, LANES), F32)
    x1, h2, *route_l = _pool_layer(x, modl[0], *args0, no_rows)
    xc1, h2c, *route_c = _pool_layer(ctx, modc[0], *args0, route_l[5])
    x, xc = _moe(0, [(x1, h2, route_l, modl[0]), (xc1, h2c, route_c, modc[0])],
                 w_gu, b_gu, w_down, b_down, row(final_g), False)

    head = QK_NOPE + QK_ROPE
    q_cols = []
    for h in range(N_HEADS):
        q_cols += list(range(h * head, h * head + QK_NOPE)) + _rope_cols(h * head + QK_NOPE)
    kv_cols = list(range(KV_LORA)) + _rope_cols(KV_LORA)
    hk = QK_NOPE + V_HEAD
    ukv_cols = [h * hk + i for h in range(N_HEADS) for i in range(QK_NOPE)] + \
               [h * hk + QK_NOPE + i for h in range(N_HEADS) for i in range(V_HEAD)]
    wuq = _take_cols(w_uq[0], q_cols).astype(BF16)
    wdkv = _take_cols(w_dkv[0], kv_cols).astype(BF16)
    wukv = _take_cols(w_ukv[0], ukv_cols).astype(BF16)
    cos_t, sin_t = _rope_tables(seq)
    q, k, v = _projections(xc, x, modc[1], modl[1], row(norm1_g[1]), cos_t, sin_t,
                           w_dq[0].astype(BF16), row(q_norm_g[0]), wuq, wdkv,
                           row(kv_norm_g[0]), wukv)
    o = _attention(q, k, v)
    x1, h2, *route_l = _out_projection(o, x, modl[1], w_o[0].astype(BF16), row(norm2_g[1]),
                                       rw[1], row(rb[1]), no_rows)
    (out,) = _moe(1, [(x1, h2, route_l, modl[1])], w_gu, b_gu, w_down, b_down, row(final_g), True)
    return out
```

```python
import functools

import jax
import jax.numpy as jnp
from jax import lax
from jax.experimental import pallas as pl
from jax.experimental.pallas import tpu as pltpu

F32 = jnp.float32
BF16 = jnp.bfloat16

N_HEADS = 8
QK_NOPE = 128
QK_ROPE = 64
V_HEAD = 128
KV_LORA = 256
N_EXPERTS = 32
TOP_K = 4
POOL_WINDOWS = (2, 4, 8, 16)
GRID_W = 64
ROPE_BASE = 10000.0
ATTN_SCALE = (QK_NOPE + QK_ROPE) ** -0.5
Q_SCALE = ATTN_SCALE * 1.4426950408889634
SWIGLU_LIMIT = 7.0
SWIGLU_ALPHA = 1.702
EPS = 1e-6

LANES = 128
SUBLANES = 8
HEAD_PAD = 2 * LANES
VMEM_LIMIT = 56 * 1024 * 1024

SEQ_TILE = 256
EXPERT_BLOCK = 512
Q_TILE = 2048
KEY_CHUNK = 256
MOD_COLS = 1536
HALO = 8
SORT_ROWS = -(-(TOP_K * SEQ_TILE + N_EXPERTS * (SUBLANES - 1)) // (2 * LANES)) * (2 * LANES)
CHUNK_LANES = -(-(SORT_ROWS // SUBLANES + 1) // LANES) * LANES
CHUNK_UNROLL = 4


def _cparams(sem, vmem=VMEM_LIMIT):
    return pltpu.CompilerParams(dimension_semantics=sem, vmem_limit_bytes=vmem)


def _dot(a, b):
    return jnp.dot(a, b, preferred_element_type=F32)


def _dot3(a, b):
    ah = a.astype(BF16)
    al = (a - ah.astype(F32)).astype(BF16)
    bh = b.astype(BF16)
    bl = (b - bh.astype(F32)).astype(BF16)
    return _dot(ah, bh) + _dot(ah, bl) + _dot(al, bh)


def _rms(x, g):
    return x * lax.rsqrt(jnp.mean(x * x, axis=-1, keepdims=True) + EPS) * g


def _sigmoid(x):
    return 1.0 / (1.0 + jnp.exp(-x))


HIGH_HALF = -65536


def _pack_rows(x):
    half = x.shape[1] // 2
    lo = lax.bitcast_convert_type(x[:, :half], jnp.int32)
    hi = lax.bitcast_convert_type(x[:, half:], jnp.int32)
    return (hi & HIGH_HALF) | lax.shift_right_logical(lo, 16)


def _unpack_rows(w):
    lo = lax.bitcast_convert_type(lax.shift_left(w, 16), F32)
    hi = lax.bitcast_convert_type(w & HIGH_HALF, F32)
    return jnp.concatenate([lo, hi], axis=1)


def _mod_kernel(c_ref, w_ref, b_ref, o_ref):
    a = c_ref[...]
    s = a * _sigmoid(a)
    o_ref[0] = _dot3(s, w_ref[0]) + b_ref[0]


def _modulation(cvec, ada_w, ada_b):
    depth, d, n = ada_w.shape
    rows = cvec.shape[0]
    return pl.pallas_call(
        _mod_kernel,
        grid=(depth, n // MOD_COLS),
        in_specs=[
            pl.BlockSpec((rows, d), lambda i, j: (0, 0)),
            pl.BlockSpec((1, d, MOD_COLS), lambda i, j: (i, 0, j)),
            pl.BlockSpec((1, 1, MOD_COLS), lambda i, j: (i, 0, j)),
        ],
        out_specs=pl.BlockSpec((1, rows, MOD_COLS), lambda i, j: (i, 0, j)),
        out_shape=jax.ShapeDtypeStruct((depth, rows, n), F32),
        compiler_params=_cparams(("arbitrary", "arbitrary")),
        name="modulation",
    )(cvec, ada_w, ada_b.reshape(depth, 1, n))


def _route_tile(work, first, cin_ref, pos_ref, post_ref, gt_ref, chunk_ref, span_ref, carry_ref):
    rows = work.shape[0]
    lane = lax.broadcasted_iota(jnp.int32, work.shape, 1).astype(F32)
    tops, hots = [], []
    for _ in range(TOP_K):
        m = jnp.max(work, axis=1, keepdims=True)
        idx = jnp.min(jnp.where(work == m, lane, float(LANES)), axis=1, keepdims=True)
        hot = lane == idx
        work = jnp.where(hot, -jnp.inf, work)
        tops.append(m)
        hots.append(hot)
    chosen = jnp.zeros(work.shape, F32)
    for hot in hots:
        chosen = chosen + jnp.where(hot, 1.0, 0.0)
    count = jnp.broadcast_to(jnp.sum(chosen, axis=0, keepdims=True), carry_ref.shape)
    seg_rows = jnp.floor((count + (SUBLANES - 1.0)) * (1.0 / SUBLANES)) * SUBLANES

    def lane_prefix(v):
        shift = 1
        while shift < N_EXPERTS:
            v = v + pltpu.roll(v, shift, axis=1)
            shift *= 2
        return v

    @pl.when(first)
    def _():
        carry_ref[...] = cin_ref[...]

    r = lax.broadcasted_iota(jnp.int32, (rows, rows), 0)
    c = lax.broadcasted_iota(jnp.int32, (rows, rows), 1)
    earlier = jnp.where(c < r, 1.0, 0.0).astype(BF16)
    run_end = lane_prefix(seg_rows)
    run_start = run_end - seg_rows
    before = _dot(earlier, chosen.astype(BF16)) + run_start[0:1]
    exps = [jnp.exp(t - tops[0]) for t in tops]
    den = exps[0] + exps[1] + exps[2] + exps[3]
    col = lax.broadcasted_iota(jnp.int32, (rows, TOP_K), 1)
    wide = lax.broadcasted_iota(jnp.int32, (rows, LANES), 1)
    gt = jnp.zeros((rows, TOP_K), F32)
    ps = jnp.zeros((rows, TOP_K), F32)
    ps_wide = jnp.zeros((rows, LANES), F32)
    for k in range(TOP_K):
        pos_k = jnp.sum(jnp.where(hots[k], before, 0.0), axis=1, keepdims=True)
        gt = jnp.where(col == k, exps[k] / den, gt)
        ps = jnp.where(col == k, pos_k, ps)
        ps_wide = jnp.where(wide == k, pos_k, ps_wide)
    pos_ref[...] = ps.astype(jnp.int32)
    post_ref[...] = jnp.transpose(ps_wide)[0:SUBLANES].astype(jnp.int32)
    gt_ref[...] = gt

    square = (LANES, LANES)
    srow = lax.broadcasted_iota(jnp.int32, square, 0)
    packed = jnp.where(srow == 0, jnp.broadcast_to(run_start[0:1], square),
                       jnp.where(srow == 1, jnp.broadcast_to(run_end[0:1], square),
                                 jnp.broadcast_to((carry_ref[...] - run_start)[0:1], square)))
    cols = jnp.transpose(packed)
    start_c, end_c, shift_c = cols[:, 0:1], cols[:, 1:2], cols[:, 2:3]
    erow = lax.broadcasted_iota(jnp.int32, (LANES, CHUNK_LANES), 0)
    row0 = (lax.broadcasted_iota(jnp.int32, (LANES, CHUNK_LANES), 1) * SUBLANES).astype(F32)
    holds = (erow < N_EXPERTS) & (start_c <= row0) & (row0 < end_c)
    rel = jnp.sum(jnp.where(holds, shift_c, 0.0), axis=0, keepdims=True) + row0[0:1]
    eid = jnp.sum(jnp.where(holds, erow.astype(F32), 0.0), axis=0, keepdims=True)
    n_chunks = jnp.sum(jnp.where(holds, 1.0, 0.0), keepdims=True)
    crow = lax.broadcasted_iota(jnp.int32, (SUBLANES, CHUNK_LANES), 0)
    chunk_ref[0] = jnp.where(crow == 0, rel, jnp.where(crow == 1, eid, n_chunks)).astype(jnp.int32)

    total = carry_ref[...] + seg_rows
    carry_ref[...] = total
    sub = lax.broadcasted_iota(jnp.int32, carry_ref.shape, 0)
    blocks = jnp.floor((total + (EXPERT_BLOCK - 1.0)) * (1.0 / EXPERT_BLOCK))
    ends = lane_prefix(blocks) * EXPERT_BLOCK
    span_ref[...] = jnp.where(sub == 0, ends, jnp.where(sub == 1, ends - blocks * EXPERT_BLOCK,
                                                        0.0)).astype(jnp.int32)


def _route_specs(tiles, per, rows):
    tile = lambda b, j: b * per + j
    tok = pl.BlockSpec((rows, TOP_K), lambda b, j: (tile(b, j), 0))
    keep = pl.BlockSpec((SUBLANES, LANES), lambda b, j: (0, 0))
    specs = [tok, pl.BlockSpec((SUBLANES, rows), lambda b, j: (0, tile(b, j))), tok,
             pl.BlockSpec((1, SUBLANES, CHUNK_LANES), lambda b, j: (tile(b, j), 0, 0)), keep, keep]
    t = tiles * rows
    shapes = [jax.ShapeDtypeStruct((t, TOP_K), jnp.int32),
              jax.ShapeDtypeStruct((SUBLANES, t), jnp.int32),
              jax.ShapeDtypeStruct((t, TOP_K), F32),
              jax.ShapeDtypeStruct((tiles, SUBLANES, CHUNK_LANES), jnp.int32),
              jax.ShapeDtypeStruct((SUBLANES, LANES), jnp.int32),
              jax.ShapeDtypeStruct((SUBLANES, LANES), F32)]
    return specs, shapes


def _tail(x1, mod, first, n2_ref, rw_ref, rb_ref, cin_ref, x1_ref, h2_ref, route_refs):
    sh2, s2 = mod[3:4], mod[4:5]
    h2 = _rms(x1, n2_ref[...]) * (1.0 + s2) + sh2
    x1_ref[0] = x1
    h2_ref[0] = h2.astype(BF16)
    _route_tile(_dot3(h2, rw_ref[...]) + rb_ref[...], first, cin_ref, *route_refs)


def _pool_kernel(x_ref, mod_ref, n1_ref, n2_ref, pw_ref, pb_ref, ps_ref, rw_ref, rb_ref, cin_ref,
                 x1_ref, h2_ref, *route_refs, rows, seq):
    j = pl.program_id(1)
    first = (pl.program_id(0) == 0) & (j == 0)
    start = pl.multiple_of(j * rows, rows)
    prev0 = pl.multiple_of(jnp.maximum(start - HALO, 0), HALO)
    next0 = pl.multiple_of(jnp.minimum(start + rows, seq - HALO), HALO)
    xm = x_ref[0, pl.ds(start, rows), :]
    xe = jnp.concatenate(
        [x_ref[0, pl.ds(prev0, HALO), :], xm, x_ref[0, pl.ds(next0, HALO), :]], axis=0)
    mod = mod_ref[0]
    sh1, s1, g1 = mod[0:1], mod[1:2], mod[2:3]
    h = _rms(xe, n1_ref[...]) * (1.0 + s1) + sh1
    pos = start - HALO + lax.broadcasted_iota(jnp.int32, (rows + 2 * HALO, 1), 0)
    hz = jnp.where((pos >= 0) & (pos < seq), h, 0.0)
    tpos = start + lax.broadcasted_iota(jnp.int32, (rows, 1), 0)
    group = hz.shape[1] // len(POOL_WINDOWS)
    ys = []
    for g, w in enumerate(POOL_WINDOWS):
        half = w // 2
        hg = hz[:, g * group:(g + 1) * group]
        tot = hg[HALO - half:HALO - half + rows]
        for o in range(1 - half, half):
            tot = tot + hg[HALO + o:HALO + o + rows]
        cnt = (jnp.minimum(tpos + half, seq) - jnp.maximum(tpos - half, 0)).astype(F32)
        dlt = tot / cnt - hg[HALO:HALO + rows]
        ys.append(_dot(dlt.astype(BF16), pw_ref[g]))
    y = (jnp.concatenate(ys, axis=1) + pb_ref[...]) * ps_ref[...]
    x1 = xm + g1 * y
    _tail(x1, mod, first, n2_ref, rw_ref, rb_ref, cin_ref, x1_ref, h2_ref, route_refs)


def _pool_layer(x, mod8, n1, n2, pw, pb, ps, rw, rb, carry_in):
    bsz, seq, d = x.shape
    rows = SEQ_TILE
    per = seq // rows
    grid = (bsz, per)
    const2 = lambda b, j: (0, 0)
    route_specs, route_shapes = _route_specs(bsz * per, per, rows)
    return pl.pallas_call(
        functools.partial(_pool_kernel, rows=rows, seq=seq),
        grid=grid,
        in_specs=[
            pl.BlockSpec((1, seq, d), lambda b, j: (b, 0, 0)),
            pl.BlockSpec((1, SUBLANES, d), lambda b, j: (b, 0, 0)),
            pl.BlockSpec((1, d), const2),
            pl.BlockSpec((1, d), const2),
            pl.BlockSpec(pw.shape, lambda b, j: (0, 0, 0)),
            pl.BlockSpec((1, d), const2),
            pl.BlockSpec((1, d), const2),
            pl.BlockSpec(rw.shape, const2),
            pl.BlockSpec((1, LANES), const2),
            pl.BlockSpec((SUBLANES, LANES), const2),
        ],
        out_specs=[
            pl.BlockSpec((1, rows, d), lambda b, j: (b, j, 0)),
            pl.BlockSpec((1, rows, d), lambda b, j: (b, j, 0)),
        ] + route_specs,
        out_shape=[
            jax.ShapeDtypeStruct((bsz, seq, d), F32),
            jax.ShapeDtypeStruct((bsz, seq, d), BF16),
        ] + route_shapes,
        compiler_params=_cparams(("arbitrary", "arbitrary")),
        name="pool_layer",
    )(x, mod8, n1, n2, pw, pb, ps, rw, rb, carry_in)


def _chunk_copies(go, list_ref, make):
    n = list_ref[CHUNK_LANES - 1]
    groups = n // CHUNK_UNROLL

    def one(c):
        go(make(pl.multiple_of(c * SUBLANES, SUBLANES), pl.multiple_of(list_ref[c], SUBLANES)))

    def group(g, carry):
        for u in range(CHUNK_UNROLL):
            one(g * CHUNK_UNROLL + u)
        return carry

    def rest(c, carry):
        one(c)
        return carry

    lax.fori_loop(0, groups, group, 0)
    lax.fori_loop(groups * CHUNK_UNROLL, n, rest, 0)


def _dispatch_kernel(ends_ref, nu_ref, list_ref, prev_ref, post_ref, *refs, tiles, n_blocks):
    n_src = len(tiles)
    srcs, (xs_ref, sorted_ref, zeros, sem, zsem) = refs[:n_src], refs[n_src:]
    i = pl.program_id(0)
    blk = EXPERT_BLOCK

    @pl.when(i == 0)
    def _():
        zeros[...] = jnp.zeros_like(zeros)

        def fill(b):
            return pltpu.make_async_copy(zeros, xs_ref.at[pl.ds(b * blk, blk)], zsem)

        def expert_tails(go):
            for e in range(N_EXPERTS):
                lo = ends_ref[e - 1] if e else 0

                @pl.when(ends_ref[e] > lo)
                def _(e=e):
                    go(fill(ends_ref[e] // blk - 1))

        def unused(go):
            def body(b, carry):
                go(fill(b))
                return carry
            lax.fori_loop(nu_ref[0], n_blocks, body, 0)

        expert_tails(lambda cp: cp.start())
        unused(lambda cp: cp.start())
        expert_tails(lambda cp: cp.wait())
        unused(lambda cp: cp.wait())

    h = srcs[0][...]
    first = tiles[0]
    for src, n in zip(srcs[1:], tiles[1:]):
        h = jnp.where(i >= first, src[...], h)
        first += n

    pos = post_ref[...]
    j = lax.broadcasted_iota(jnp.int32, (SORT_ROWS, h.shape[0]), 0)
    onehot = jnp.zeros(j.shape, F32)
    for k in range(TOP_K):
        onehot = jnp.where(j == pos[k:k + 1], 1.0, onehot)
    slot = i % 2
    sorted_ref[slot] = _pack_rows(_dot(onehot.astype(BF16), h))

    def make(buf):
        def build(local, glob):
            return pltpu.make_async_copy(sorted_ref.at[buf, pl.ds(local, SUBLANES)],
                                         xs_ref.at[pl.ds(glob, SUBLANES)], sem.at[buf])
        return build

    _chunk_copies(lambda cp: cp.start(), list_ref, make(slot))

    @pl.when(i > 0)
    def _():
        _chunk_copies(lambda cp: cp.wait(), prev_ref, make(1 - slot))

    @pl.when(i == pl.num_programs(0) - 1)
    def _():
        _chunk_copies(lambda cp: cp.wait(), list_ref, make(slot))


def _dispatch(chunks, ends, n_used, pos_t, sources, n_blocks):
    rows = SEQ_TILE
    d = sources[0].shape[1]
    tiles = tuple(s.shape[0] // rows for s in sources)
    firsts = [sum(tiles[:k]) for k in range(len(tiles))]

    def src_spec(first, n):
        return pl.BlockSpec((rows, d), lambda i, *_: (jnp.clip(i - first, 0, n - 1), 0))

    grid_spec = pltpu.PrefetchScalarGridSpec(
        num_scalar_prefetch=2,
        grid=(sum(tiles),),
        in_specs=[pl.BlockSpec((CHUNK_LANES,), lambda i, *_: (i,), memory_space=pltpu.SMEM),
                  pl.BlockSpec((CHUNK_LANES,), lambda i, *_: (jnp.maximum(i - 1, 0),),
                               memory_space=pltpu.SMEM),
                  pl.BlockSpec((SUBLANES, rows), lambda i, *_: (0, i))]
        + [src_spec(f, n) for f, n in zip(firsts, tiles)],
        out_specs=pl.BlockSpec(memory_space=pl.ANY),
        scratch_shapes=[pltpu.VMEM((2, SORT_ROWS, d // 2), jnp.int32),
                        pltpu.VMEM((EXPERT_BLOCK, d // 2), jnp.int32),
                        pltpu.SemaphoreType.DMA((2,)), pltpu.SemaphoreType.DMA],
    )
    return pl.pallas_call(
        functools.partial(_dispatch_kernel, tiles=tiles, n_blocks=n_blocks),
        grid_spec=grid_spec,
        out_shape=jax.ShapeDtypeStruct((n_blocks * EXPERT_BLOCK, d // 2), jnp.int32),
        compiler_params=_cparams(("arbitrary",)),
        name="dispatch",
    )(ends, n_used, chunks, chunks, pos_t, *sources)


def _experts_kernel(be_ref, nu_ref, hf_ref, xs_ref, wgu_ref, bgu_ref, wd_ref, bd_ref, ys_ref,
                    wgu_bf, wd_bf, *, rows, dff):
    b = pl.program_id(0)
    prev = be_ref[jnp.maximum(b - 1, 0)]
    fresh = (b == 0) | (be_ref[b] != prev)
    live = b < nu_ref[0]
    half = hf_ref[b] == 1

    @pl.when(live & fresh)
    def _():
        wgu_bf[...] = wgu_ref[0, 0].astype(BF16)
        wd_bf[...] = wd_ref[0, 0].astype(BF16)

    def ffn(packed):
        x = _unpack_rows(packed).astype(BF16)
        gu = _dot(x, wgu_bf[...]) + bgu_ref[0, 0]
        gate = jnp.minimum(gu[:, :dff], SWIGLU_LIMIT)
        up = jnp.clip(gu[:, dff:], -SWIGLU_LIMIT, SWIGLU_LIMIT)
        act = (up + 1.0) * (gate * _sigmoid(SWIGLU_ALPHA * gate))
        y = _dot(act.astype(BF16), wd_bf[...]) + bd_ref[0, 0]
        return _pack_rows(y.astype(BF16).astype(F32))

    @pl.when(live & jnp.logical_not(half))
    def _():
        ys_ref[...] = ffn(xs_ref[...])

    @pl.when(live & half)
    def _():
        ys_ref[:rows // 2] = ffn(xs_ref[:rows // 2])
        ys_ref[rows // 2:] = jnp.zeros((rows // 2, ys_ref.shape[1]), jnp.int32)

    @pl.when(jnp.logical_not(live))
    def _():
        ys_ref[...] = jnp.zeros_like(ys_ref)


def _experts(layer, block_expert, n_used, half, xs, w_gu, b_gu, w_down, b_down):
    n_rows = xs.shape[0]
    rows = EXPERT_BLOCK
    n_blocks = n_rows // rows
    depth, e, d, dff2 = w_gu.shape
    dff = dff2 // 2
    blk = lambda b, be, nu, hf: (jnp.minimum(b, nu[0] - 1), 0)
    per_expert = lambda b, be, nu, hf: (layer, be[b], 0, 0)
    grid_spec = pltpu.PrefetchScalarGridSpec(
        num_scalar_prefetch=3,
        grid=(n_blocks,),
        in_specs=[
            pl.BlockSpec((rows, d // 2), blk),
            pl.BlockSpec((1, 1, d, dff2), per_expert),
            pl.BlockSpec((1, 1, 1, dff2), per_expert),
            pl.BlockSpec((1, 1, dff, d), per_expert),
            pl.BlockSpec((1, 1, 1, d), per_expert),
        ],
        out_specs=pl.BlockSpec((rows, d // 2), lambda b, be, nu, hf: (b, 0)),
        scratch_shapes=[pltpu.VMEM((d, dff2), BF16), pltpu.VMEM((dff, d), BF16)],
    )
    return pl.pallas_call(
        functools.partial(_experts_kernel, rows=rows, dff=dff),
        grid_spec=grid_spec,
        out_shape=jax.ShapeDtypeStruct((n_rows, d // 2), jnp.int32),
        compiler_params=_cparams(("arbitrary",)),
        name="experts",
    )(block_expert, n_used, half, xs, w_gu, b_gu.reshape(depth, e, 1, dff2), w_down,
      b_down.reshape(depth, e, 1, d))


def _combine_kernel(list_ref, next_ref, ys_ref, pos_ref, gt_ref, x1_ref, mod_ref, fg_ref, out_ref,
                    sorted_ref, sem, *, per, final):
    step = pl.program_id(0) * per + pl.program_id(1)
    steps = pl.num_programs(0) * per
    slot = step % 2

    def make(buf):
        def build(local, glob):
            return pltpu.make_async_copy(ys_ref.at[pl.ds(glob, SUBLANES)],
                                         sorted_ref.at[buf, pl.ds(local, SUBLANES)], sem.at[buf])
        return build

    @pl.when(step == 0)
    def _():
        sorted_ref[...] = jnp.zeros_like(sorted_ref)
        _chunk_copies(lambda cp: cp.start(), list_ref, make(slot))

    @pl.when(step + 1 < steps)
    def _():
        _chunk_copies(lambda cp: cp.start(), next_ref, make(1 - slot))

    _chunk_copies(lambda cp: cp.wait(), list_ref, make(slot))

    pos = pos_ref[...]
    gates = gt_ref[...]
    lane = lax.broadcasted_iota(jnp.int32, (pos.shape[0], SORT_ROWS), 1)
    weights = jnp.zeros(lane.shape, F32)
    for k in range(TOP_K):
        weights = jnp.where(lane == pos[:, k:k + 1], gates[:, k:k + 1], weights)
    y = _dot(weights.astype(BF16), _unpack_rows(sorted_ref[slot]).astype(BF16))
    x2 = x1_ref[0] + mod_ref[0][5:6] * y
    if final:
        x2 = _rms(x2, fg_ref[...])
    out_ref[0] = x2


def _combine(chunks, tile0, pos, gates, ys, x1, mod8, final_g, final):
    bsz, seq, d = x1.shape
    rows = SEQ_TILE
    per = seq // rows
    last = tile0 + bsz * per - 1
    tok = pl.BlockSpec((rows, TOP_K), lambda b, j: (tile0 + b * per + j, 0))
    return pl.pallas_call(
        functools.partial(_combine_kernel, per=per, final=final),
        grid=(bsz, per),
        in_specs=[
            pl.BlockSpec((CHUNK_LANES,), lambda b, j: (tile0 + b * per + j,),
                         memory_space=pltpu.SMEM),
            pl.BlockSpec((CHUNK_LANES,), lambda b, j: (jnp.minimum(tile0 + b * per + j + 1, last),),
                         memory_space=pltpu.SMEM),
            pl.BlockSpec(memory_space=pl.ANY),
            tok, tok,
            pl.BlockSpec((1, rows, d), lambda b, j: (b, j, 0)),
            pl.BlockSpec((1, SUBLANES, d), lambda b, j: (b, 0, 0)),
            pl.BlockSpec((1, d), lambda b, j: (0, 0)),
        ],
        out_specs=pl.BlockSpec((1, rows, d), lambda b, j: (b, j, 0)),
        out_shape=jax.ShapeDtypeStruct((bsz, seq, d), F32),
        scratch_shapes=[pltpu.VMEM((2, SORT_ROWS, d // 2), jnp.int32),
                        pltpu.SemaphoreType.DMA((2,))],
        compiler_params=_cparams(("arbitrary", "arbitrary")),
        name="combine",
    )(chunks, chunks, ys, pos, gates, x1, mod8, final_g)


def _moe(layer, streams, w_gu, b_gu, w_down, b_down, final_g, final):
    d = streams[0][0].shape[2]
    routes = [s[2] for s in streams]
    pos = jnp.concatenate([r[0] for r in routes], axis=0)
    pos_t = jnp.concatenate([r[1] for r in routes], axis=1)
    gates = jnp.concatenate([r[2] for r in routes], axis=0)
    chunk = jnp.concatenate([r[3] for r in routes], axis=0)
    span = routes[-1][4]
    t = pos.shape[0]
    tiles = chunk.shape[0]
    ends = span[0, :N_EXPERTS]
    starts = span[1, :N_EXPERTS]
    lane = jnp.arange(CHUNK_LANES, dtype=jnp.int32)
    hot = chunk[:, 1, :, None] == jnp.arange(N_EXPERTS, dtype=jnp.int32)
    base = jnp.sum(jnp.where(hot, starts, 0), axis=-1)
    chunks = jnp.where(lane == CHUNK_LANES - 1, chunk[:, 2], chunk[:, 0] + base).reshape(-1)
    blk = EXPERT_BLOCK
    max_rows = t * TOP_K + tiles * N_EXPERTS * (SUBLANES - 1)
    n_blocks = -(-max_rows // blk) + N_EXPERTS
    n_used = ends[N_EXPERTS - 1] // blk
    block_start = jnp.arange(n_blocks, dtype=jnp.int32) * blk
    bexp = jnp.sum((ends[None, :] <= block_start[:, None]).astype(jnp.int32), axis=1)
    last = jnp.sum((ends <= (n_used - 1) * blk).astype(jnp.int32))
    bexp = jnp.where(jnp.arange(n_blocks) < n_used, bexp, last)
    used = routes[-1][5][0, :N_EXPERTS].astype(jnp.int32)
    mine = bexp[:, None] == jnp.arange(N_EXPERTS, dtype=jnp.int32)
    left = jnp.sum(jnp.where(mine, starts + used, 0), axis=-1) - block_start
    half = (left <= blk // 2).astype(jnp.int32)
    sources = [s[1].reshape(-1, d) for s in streams]
    n_used = n_used.reshape(1)
    xs = _dispatch(chunks, ends, n_used, pos_t, sources, n_blocks)
    ys = _experts(layer, bexp, n_used, half, xs, w_gu, b_gu, w_down, b_down)
    outs, tile0 = [], 0
    for x1, _, _, mod8 in streams:
        outs.append(_combine(chunks, tile0, pos, gates, ys, x1, mod8, final_g, final))
        tile0 += x1.shape[0] * x1.shape[1] // SEQ_TILE
    return outs


def _rope(x, cos, sin):
    return x * cos + pltpu.roll(x, LANES // 2, axis=1) * sin


def _proj_kernel(xc_ref, x_ref, modc_ref, modl_ref, n1_ref, cos_ref, sin_ref,
                 wdq_ref, qg_ref, wuq_ref, wdkv_ref, kvg_ref, wukv_ref,
                 q_ref, k_ref, v_ref, *, ctx_tiles):
    j = pl.program_id(1)
    is_ctx = j < ctx_tiles
    xin = jnp.where(is_ctx, xc_ref[0], x_ref[0])
    mod = jnp.where(is_ctx, modc_ref[0], modl_ref[0])
    sh1, s1 = mod[0:1], mod[1:2]
    hb = (_rms(xin, n1_ref[...]) * (1.0 + s1) + sh1).astype(BF16)
    cos = jnp.where(is_ctx, 1.0, cos_ref[...])
    sin = jnp.where(is_ctx, 0.0, sin_ref[...])

    kva = _dot(hb, wdkv_ref[...])
    kpe = _rope(kva[:, KV_LORA:], cos, sin).astype(BF16)
    kv = _dot(_rms(kva[:, :KV_LORA], kvg_ref[...]).astype(BF16), wukv_ref[...])
    nope_all = N_HEADS * QK_NOPE
    ones = jnp.ones((kv.shape[0], V_HEAD), BF16)
    for h in range(N_HEADS):
        k_ref[0, :, h * HEAD_PAD:h * HEAD_PAD + QK_NOPE] = \
            kv[:, h * QK_NOPE:(h + 1) * QK_NOPE].astype(BF16)
        k_ref[0, :, h * HEAD_PAD + QK_NOPE:(h + 1) * HEAD_PAD] = kpe
        v_ref[0, :, h * HEAD_PAD:h * HEAD_PAD + V_HEAD] = \
            kv[:, nope_all + h * V_HEAD:nope_all + (h + 1) * V_HEAD].astype(BF16)
        v_ref[0, :, h * HEAD_PAD + V_HEAD:(h + 1) * HEAD_PAD] = ones

    @pl.when(jnp.logical_not(is_ctx))
    def _():
        qa = _dot(hb, wdq_ref[...])
        q = _dot(_rms(qa, qg_ref[...]).astype(BF16), wuq_ref[...]) * Q_SCALE
        for h in range(N_HEADS):
            lo = h * HEAD_PAD
            q_ref[0, :, lo:lo + QK_NOPE] = q[:, lo:lo + QK_NOPE].astype(BF16)
            q_ref[0, :, lo + QK_NOPE:lo + HEAD_PAD] = \
                _rope(q[:, lo + QK_NOPE:lo + HEAD_PAD], cos, sin).astype(BF16)


def _projections(xc, x, modc8, modl8, n1, cos_t, sin_t, wdq, qg, wuq, wdkv, kvg, wukv):
    bsz, seq, d = x.shape
    ctx = xc.shape[1]
    rows = SEQ_TILE
    ctx_tiles = ctx // rows
    lat = lambda b, j: (b, jnp.maximum(j - ctx_tiles, 0), 0)
    full = lambda a: pl.BlockSpec(a.shape, lambda b, j: (0,) * a.ndim)
    kw = N_HEADS * HEAD_PAD
    vw = N_HEADS * HEAD_PAD
    return pl.pallas_call(
        functools.partial(_proj_kernel, ctx_tiles=ctx_tiles),
        grid=(bsz, (ctx + seq) // rows),
        in_specs=[
            pl.BlockSpec((1, rows, d), lambda b, j: (b, jnp.minimum(j, ctx_tiles - 1), 0)),
            pl.BlockSpec((1, rows, d), lat),
            pl.BlockSpec((1, SUBLANES, d), lambda b, j: (b, 0, 0)),
            pl.BlockSpec((1, SUBLANES, d), lambda b, j: (b, 0, 0)),
            full(n1),
            pl.BlockSpec((rows, LANES), lambda b, j: (jnp.maximum(j - ctx_tiles, 0), 0)),
            pl.BlockSpec((rows, LANES), lambda b, j: (jnp.maximum(j - ctx_tiles, 0), 0)),
            full(wdq), full(qg), full(wuq), full(wdkv), full(kvg), full(wukv),
        ],
        out_specs=[
            pl.BlockSpec((1, rows, kw), lat),
            pl.BlockSpec((1, rows, kw), lambda b, j: (b, j, 0)),
            pl.BlockSpec((1, rows, vw), lambda b, j: (b, j, 0)),
        ],
        out_shape=[
            jax.ShapeDtypeStruct((bsz, seq, kw), BF16),
            jax.ShapeDtypeStruct((bsz, ctx + seq, kw), BF16),
            jax.ShapeDtypeStruct((bsz, ctx + seq, vw), BF16),
        ],
        compiler_params=_cparams(("arbitrary", "arbitrary")),
        name="projections",
    )(xc, x, modc8, modl8, n1, cos_t, sin_t, wdq, qg, wuq, wdkv, kvg, wukv)


def _attn_kernel(q_ref, k_ref, v_ref, o_ref, *, chunk):
    q = q_ref[0]
    m = acc = None
    for c in range(k_ref.shape[1] // chunk):
        rows = slice(c * chunk, (c + 1) * chunk)
        s = lax.dot_general(q, k_ref[0, rows, :], (((1,), (1,)), ((), ())),
                            preferred_element_type=F32)
        m_c = jnp.max(s, axis=-1, keepdims=True)
        m_new = m_c if c == 0 else jnp.maximum(m, m_c)
        pv = _dot(jnp.exp2(s - m_new).astype(BF16), v_ref[0, rows, :])
        acc = pv if c == 0 else acc * jnp.exp2(m - m_new) + pv
        m = m_new
    o_ref[0] = (acc[:, :V_HEAD] / acc[:, V_HEAD:]).astype(BF16)


def _attention(q, k, v):
    bsz, seq, _ = q.shape
    keys = k.shape[1]
    rows = min(Q_TILE, seq)
    chunk = next(c for c in (KEY_CHUNK, 2 * LANES, keys) if keys % c == 0)
    return pl.pallas_call(
        functools.partial(_attn_kernel, chunk=chunk),
        grid=(bsz, N_HEADS, seq // rows),
        in_specs=[
            pl.BlockSpec((1, rows, HEAD_PAD), lambda b, h, i: (b, i, h)),
            pl.BlockSpec((1, keys, HEAD_PAD), lambda b, h, i: (b, 0, h)),
            pl.BlockSpec((1, keys, HEAD_PAD), lambda b, h, i: (b, 0, h)),
        ],
        out_specs=pl.BlockSpec((1, rows, V_HEAD), lambda b, h, i: (b, i, h)),
        out_shape=jax.ShapeDtypeStruct((bsz, seq, N_HEADS * V_HEAD), BF16),
        compiler_params=_cparams(("arbitrary", "arbitrary", "arbitrary")),
        name="attention",
    )(q, k, v)


def _oproj_kernel(o_ref, x_ref, mod_ref, wo_ref, n2_ref, rw_ref, rb_ref, cin_ref,
                  x1_ref, h2_ref, *route_refs):
    first = (pl.program_id(0) == 0) & (pl.program_id(1) == 0)
    mod = mod_ref[0]
    x1 = x_ref[0] + mod[2:3] * _dot(o_ref[0], wo_ref[...])
    _tail(x1, mod, first, n2_ref, rw_ref, rb_ref, cin_ref, x1_ref, h2_ref, route_refs)


def _out_projection(o, x, mod8, wo, n2, rw, rb, carry_in):
    bsz, seq, d = x.shape
    rows = SEQ_TILE
    per = seq // rows
    const2 = lambda b, j: (0, 0)
    route_specs, route_shapes = _route_specs(bsz * per, per, rows)
    return pl.pallas_call(
        _oproj_kernel,
        grid=(bsz, per),
        in_specs=[
            pl.BlockSpec((1, rows, o.shape[2]), lambda b, j: (b, j, 0)),
            pl.BlockSpec((1, rows, d), lambda b, j: (b, j, 0)),
            pl.BlockSpec((1, SUBLANES, d), lambda b, j: (b, 0, 0)),
            pl.BlockSpec(wo.shape, const2),
            pl.BlockSpec((1, d), const2),
            pl.BlockSpec(rw.shape, const2),
            pl.BlockSpec((1, LANES), const2),
            pl.BlockSpec((SUBLANES, LANES), const2),
        ],
        out_specs=[
            pl.BlockSpec((1, rows, d), lambda b, j: (b, j, 0)),
            pl.BlockSpec((1, rows, d), lambda b, j: (b, j, 0)),
        ] + route_specs,
        out_shape=[
            jax.ShapeDtypeStruct((bsz, seq, d), F32),
            jax.ShapeDtypeStruct((bsz, seq, d), BF16),
        ] + route_shapes,
        compiler_params=_cparams(("arbitrary", "arbitrary")),
        name="out_projection",
    )(o, x, mod8, wo, n2, rw, rb, carry_in)


def _rope_cols(base):
    q = QK_ROPE // 4
    x1 = list(range(base, base + q)) + list(range(base + 2 * q, base + 3 * q))
    x2 = list(range(base + q, base + 2 * q)) + list(range(base + 3 * q, base + 4 * q))
    pad = [-1] * (LANES // 2 - 2 * q)
    return x1 + pad + x2 + pad


def _take_cols(w, cols):
    wz = jnp.concatenate([w, jnp.zeros((w.shape[0], 1), w.dtype)], axis=1)
    idx = jnp.asarray([c if c >= 0 else w.shape[1] for c in cols], jnp.int32)
    return jnp.take(wz, idx, axis=1)


def _rope_tables(seq):
    q = QK_ROPE // 4
    pos = jnp.arange(seq)
    inv = ROPE_BASE ** (-jnp.arange(0, QK_ROPE // 2, 2, dtype=F32) / (QK_ROPE // 2))
    ang = jnp.concatenate([(pos // GRID_W).astype(F32)[:, None] * inv,
                           (pos % GRID_W).astype(F32)[:, None] * inv], axis=1)
    pad1 = jnp.ones((seq, LANES // 2 - 2 * q), F32)
    pad0 = jnp.zeros((seq, LANES // 2 - 2 * q), F32)
    cos = jnp.concatenate([jnp.cos(ang), pad1, jnp.cos(ang), pad1], axis=1)
    sin = jnp.concatenate([-jnp.sin(ang), pad0, jnp.sin(ang), pad0], axis=1)
    return cos, sin


def _mod8(mod_rows):
    bsz, n = mod_rows.shape
    d = n // 6
    m = mod_rows.reshape(bsz, 6, d)
    return jnp.concatenate([m, jnp.zeros((bsz, SUBLANES - 6, d), F32)], axis=1)


def kernel(x, c, ctx, c_ctx, ada_w, ada_b, norm1_g, norm2_g, pool_w, pool_b, pool_scale,
           w_dq, q_norm_g, w_uq, w_dkv, kv_norm_g, w_ukv, w_o,
           router_w, router_b, w_gu, b_gu, w_down, b_down, final_g):
    bsz, seq, d = x.shape
    assert seq % SEQ_TILE == 0 and ctx.shape[1] % SEQ_TILE == 0 and seq % GRID_W == 0
    row = lambda a: a.reshape(1, -1)

    mod_rows = 2 * SUBLANES
    cvec = jnp.concatenate([c, c_ctx[None], jnp.zeros((mod_rows - bsz - 1, d), F32)], axis=0)
    mod = _modulation(cvec, ada_w, ada_b)
    modl = [_mod8(mod[i, :bsz]) for i in range(2)]
    modc = [_mod8(jnp.broadcast_to(mod[i, bsz:bsz + 1], (bsz, 6 * d))) for i in range(2)]

    n_exp = router_w.shape[2]
    rw = jnp.pad(router_w, ((0, 0), (0, 0), (0, LANES - n_exp)))
    rb = jnp.pad(router_b, ((0, 0), (0, LANES - n_exp)), constant_values=-jnp.inf)

    pw = pool_w[0].astype(BF16)
    args0 = (row(norm1_g[0]), row(norm2_g[0]), pw, row(pool_b[0]), row(pool_scale[0]),
             rw[0], row(rb[0]))
    no_rows = jnp.zeros((SUBLANES, LANES), F32)
    x1, h2, *route_l = _pool_layer(x, modl[0], *args0, no_rows)
    xc1, h2c, *route_c = _pool_layer(ctx, modc[0], *args0, route_l[5])
    x, xc = _moe(0, [(x1, h2, route_l, modl[0]), (xc1, h2c, route_c, modc[0])],
                 w_gu, b_gu, w_down, b_down, row(final_g), False)

    head = QK_NOPE + QK_ROPE
    q_cols = []
    for h in range(N_HEADS):
        q_cols += list(range(h * head, h * head + QK_NOPE)) + _rope_cols(h * head + QK_NOPE)
    kv_cols = list(range(KV_LORA)) + _rope_cols(KV_LORA)
    hk = QK_NOPE + V_HEAD
    ukv_cols = [h * hk + i for h in range(N_HEADS) for i in range(QK_NOPE)] + \
               [h * hk + QK_NOPE + i for h in range(N_HEADS) for i in range(V_HEAD)]
    wuq = _take_cols(w_uq[0], q_cols).astype(BF16)
    wdkv = _take_cols(w_dkv[0], kv_cols).astype(BF16)
    wukv = _take_cols(w_ukv[0], ukv_cols).astype(BF16)
    cos_t, sin_t = _rope_tables(seq)
    q, k, v = _projections(xc, x, modc[1], modl[1], row(norm1_g[1]), cos_t, sin_t,
                           w_dq[0].astype(BF16), row(q_norm_g[0]), wuq, wdkv,
                           row(kv_norm_g[0]), wukv)
    o = _attention(q, k, v)
    x1, h2, *route_l = _out_projection(o, x, modl[1], w_o[0].astype(BF16), row(norm2_g[1]),
                                       rw[1], row(rb[1]), no_rows)
    (out,) = _moe(1, [(x1, h2, route_l, modl[1])], w_gu, b_gu, w_down, b_down, row(final_g), True)
    return out
```

```python
import functools

import jax
import jax.numpy as jnp
from jax import lax
from jax.experimental import pallas as pl
from jax.experimental.pallas import tpu as pltpu

F32 = jnp.float32
BF16 = jnp.bfloat16

N_HEADS = 8
QK_NOPE = 128
QK_ROPE = 64
V_HEAD = 128
KV_LORA = 256
N_EXPERTS = 32
TOP_K = 4
POOL_WINDOWS = (2, 4, 8, 16)
GRID_W = 64
ROPE_BASE = 10000.0
ATTN_SCALE = (QK_NOPE + QK_ROPE) ** -0.5
Q_SCALE = ATTN_SCALE * 1.4426950408889634
SWIGLU_LIMIT = 7.0
SWIGLU_ALPHA = 1.702
EPS = 1e-6

LANES = 128
SUBLANES = 8
HEAD_PAD = 2 * LANES
VMEM_LIMIT = 56 * 1024 * 1024

SEQ_TILE = 256
EXPERT_BLOCK = 512
Q_TILE = 2048
KEY_CHUNK = 256
MOD_COLS = 1536
HALO = 8
SORT_ROWS = -(-(TOP_K * SEQ_TILE + N_EXPERTS * (SUBLANES - 1)) // (2 * LANES)) * (2 * LANES)
CHUNK_LANES = -(-(SORT_ROWS // SUBLANES + 1) // LANES) * LANES
CHUNK_UNROLL = 4


def _cparams(sem, vmem=VMEM_LIMIT):
    return pltpu.CompilerParams(dimension_semantics=sem, vmem_limit_bytes=vmem)


def _dot(a, b):
    return jnp.dot(a, b, preferred_element_type=F32)


def _dot3(a, b):
    ah = a.astype(BF16)
    al = (a - ah.astype(F32)).astype(BF16)
    bh = b.astype(BF16)
    bl = (b - bh.astype(F32)).astype(BF16)
    return _dot(ah, bh) + _dot(ah, bl) + _dot(al, bh)


def _rms(x, g):
    return x * lax.rsqrt(jnp.mean(x * x, axis=-1, keepdims=True) + EPS) * g


def _sigmoid(x):
    return 1.0 / (1.0 + jnp.exp(-x))


HIGH_HALF = -65536


def _pack_rows(x):
    half = x.shape[1] // 2
    lo = lax.bitcast_convert_type(x[:, :half], jnp.int32)
    hi = lax.bitcast_convert_type(x[:, half:], jnp.int32)
    return (hi & HIGH_HALF) | lax.shift_right_logical(lo, 16)


def _unpack_rows(w):
    lo = lax.bitcast_convert_type(lax.shift_left(w, 16), F32)
    hi = lax.bitcast_convert_type(w & HIGH_HALF, F32)
    return jnp.concatenate([lo, hi], axis=1)


def _mod_kernel(c_ref, w_ref, b_ref, o_ref):
    a = c_ref[...]
    s = a * _sigmoid(a)
    o_ref[0] = _dot3(s, w_ref[0]) + b_ref[0]


def _modulation(cvec, ada_w, ada_b):
    depth, d, n = ada_w.shape
    rows = cvec.shape[0]
    return pl.pallas_call(
        _mod_kernel,
        grid=(depth, n // MOD_COLS),
        in_specs=[
            pl.BlockSpec((rows, d), lambda i, j: (0, 0)),
            pl.BlockSpec((1, d, MOD_COLS), lambda i, j: (i, 0, j)),
            pl.BlockSpec((1, 1, MOD_COLS), lambda i, j: (i, 0, j)),
        ],
        out_specs=pl.BlockSpec((1, rows, MOD_COLS), lambda i, j: (i, 0, j)),
        out_shape=jax.ShapeDtypeStruct((depth, rows, n), F32),
        compiler_params=_cparams(("arbitrary", "arbitrary")),
        name="modulation",
    )(cvec, ada_w, ada_b.reshape(depth, 1, n))


def _route_tile(work, first, cin_ref, pos_ref, post_ref, gt_ref, chunk_ref, span_ref, carry_ref):
    rows = work.shape[0]
    lane = lax.broadcasted_iota(jnp.int32, work.shape, 1).astype(F32)
    tops, hots = [], []
    for _ in range(TOP_K):
        m = jnp.max(work, axis=1, keepdims=True)
        idx = jnp.min(jnp.where(work == m, lane, float(LANES)), axis=1, keepdims=True)
        hot = lane == idx
        work = jnp.where(hot, -jnp.inf, work)
        tops.append(m)
        hots.append(hot)
    chosen = jnp.zeros(work.shape, F32)
    for hot in hots:
        chosen = chosen + jnp.where(hot, 1.0, 0.0)
    count = jnp.broadcast_to(jnp.sum(chosen, axis=0, keepdims=True), carry_ref.shape)
    seg_rows = jnp.floor((count + (SUBLANES - 1.0)) * (1.0 / SUBLANES)) * SUBLANES

    def lane_prefix(v):
        shift = 1
        while shift < N_EXPERTS:
            v = v + pltpu.roll(v, shift, axis=1)
            shift *= 2
        return v

    @pl.when(first)
    def _():
        carry_ref[...] = cin_ref[...]

    r = lax.broadcasted_iota(jnp.int32, (rows, rows), 0)
    c = lax.broadcasted_iota(jnp.int32, (rows, rows), 1)
    earlier = jnp.where(c < r, 1.0, 0.0).astype(BF16)
    run_end = lane_prefix(seg_rows)
    run_start = run_end - seg_rows
    before = _dot(earlier, chosen.astype(BF16)) + run_start[0:1]
    exps = [jnp.exp(t - tops[0]) for t in tops]
    den = exps[0] + exps[1] + exps[2] + exps[3]
    col = lax.broadcasted_iota(jnp.int32, (rows, TOP_K), 1)
    wide = lax.broadcasted_iota(jnp.int32, (rows, LANES), 1)
    gt = jnp.zeros((rows, TOP_K), F32)
    ps = jnp.zeros((rows, TOP_K), F32)
    ps_wide = jnp.zeros((rows, LANES), F32)
    for k in range(TOP_K):
        pos_k = jnp.sum(jnp.where(hots[k], before, 0.0), axis=1, keepdims=True)
        gt = jnp.where(col == k, exps[k] / den, gt)
        ps = jnp.where(col == k, pos_k, ps)
        ps_wide = jnp.where(wide == k, pos_k, ps_wide)
    pos_ref[...] = ps.astype(jnp.int32)
    post_ref[...] = jnp.transpose(ps_wide)[0:SUBLANES].astype(jnp.int32)
    gt_ref[...] = gt

    square = (LANES, LANES)
    srow = lax.broadcasted_iota(jnp.int32, square, 0)
    packed = jnp.where(srow == 0, jnp.broadcast_to(run_start[0:1], square),
                       jnp.where(srow == 1, jnp.broadcast_to(run_end[0:1], square),
                                 jnp.broadcast_to((carry_ref[...] - run_start)[0:1], square)))
    cols = jnp.transpose(packed)
    start_c, end_c, shift_c = cols[:, 0:1], cols[:, 1:2], cols[:, 2:3]
    erow = lax.broadcasted_iota(jnp.int32, (LANES, CHUNK_LANES), 0)
    row0 = (lax.broadcasted_iota(jnp.int32, (LANES, CHUNK_LANES), 1) * SUBLANES).astype(F32)
    holds = (erow < N_EXPERTS) & (start_c <= row0) & (row0 < end_c)
    rel = jnp.sum(jnp.where(holds, shift_c, 0.0), axis=0, keepdims=True) + row0[0:1]
    eid = jnp.sum(jnp.where(holds, erow.astype(F32), 0.0), axis=0, keepdims=True)
    n_chunks = jnp.sum(jnp.where(holds, 1.0, 0.0), keepdims=True)
    crow = lax.broadcasted_iota(jnp.int32, (SUBLANES, CHUNK_LANES), 0)
    chunk_ref[0] = jnp.where(crow == 0, rel, jnp.where(crow == 1, eid, n_chunks)).astype(jnp.int32)

    total = carry_ref[...] + seg_rows
    carry_ref[...] = total
    sub = lax.broadcasted_iota(jnp.int32, carry_ref.shape, 0)
    blocks = jnp.floor((total + (EXPERT_BLOCK - 1.0)) * (1.0 / EXPERT_BLOCK))
    ends = lane_prefix(blocks) * EXPERT_BLOCK
    span_ref[...] = jnp.where(sub == 0, ends, jnp.where(sub == 1, ends - blocks * EXPERT_BLOCK,
                                                        0.0)).astype(jnp.int32)


def _route_specs(tiles, per, rows):
    tile = lambda b, j: b * per + j
    tok = pl.BlockSpec((rows, TOP_K), lambda b, j: (tile(b, j), 0))
    keep = pl.BlockSpec((SUBLANES, LANES), lambda b, j: (0, 0))
    specs = [tok, pl.BlockSpec((SUBLANES, rows), lambda b, j: (0, tile(b, j))), tok,
             pl.BlockSpec((1, SUBLANES, CHUNK_LANES), lambda b, j: (tile(b, j), 0, 0)), keep, keep]
    t = tiles * rows
    shapes = [jax.ShapeDtypeStruct((t, TOP_K), jnp.int32),
              jax.ShapeDtypeStruct((SUBLANES, t), jnp.int32),
              jax.ShapeDtypeStruct((t, TOP_K), F32),
              jax.ShapeDtypeStruct((tiles, SUBLANES, CHUNK_LANES), jnp.int32),
              jax.ShapeDtypeStruct((SUBLANES, LANES), jnp.int32),
              jax.ShapeDtypeStruct((SUBLANES, LANES), F32)]
    return specs, shapes


def _tail(x1, mod, first, n2_ref, rw_ref, rb_ref, cin_ref, x1_ref, h2_ref, route_refs):
    sh2, s2 = mod[3:4], mod[4:5]
    h2 = _rms(x1, n2_ref[...]) * (1.0 + s2) + sh2
    x1_ref[0] = x1
    h2_ref[0] = h2.astype(BF16)
    _route_tile(_dot3(h2, rw_ref[...]) + rb_ref[...], first, cin_ref, *route_refs)


def _pool_kernel(x_ref, mod_ref, n1_ref, n2_ref, pw_ref, pb_ref, ps_ref, rw_ref, rb_ref, cin_ref,
                 x1_ref, h2_ref, *route_refs, rows, seq):
    j = pl.program_id(1)
    first = (pl.program_id(0) == 0) & (j == 0)
    start = pl.multiple_of(j * rows, rows)
    prev0 = pl.multiple_of(jnp.maximum(start - HALO, 0), HALO)
    next0 = pl.multiple_of(jnp.minimum(start + rows, seq - HALO), HALO)
    xm = x_ref[0, pl.ds(start, rows), :]
    xe = jnp.concatenate(
        [x_ref[0, pl.ds(prev0, HALO), :], xm, x_ref[0, pl.ds(next0, HALO), :]], axis=0)
    mod = mod_ref[0]
    sh1, s1, g1 = mod[0:1], mod[1:2], mod[2:3]
    h = _rms(xe, n1_ref[...]) * (1.0 + s1) + sh1
    pos = start - HALO + lax.broadcasted_iota(jnp.int32, (rows + 2 * HALO, 1), 0)
    hz = jnp.where((pos >= 0) & (pos < seq), h, 0.0)
    tpos = start + lax.broadcasted_iota(jnp.int32, (rows, 1), 0)
    group = hz.shape[1] // len(POOL_WINDOWS)
    ys = []
    for g, w in enumerate(POOL_WINDOWS):
        half = w // 2
        hg = hz[:, g * group:(g + 1) * group]
        tot = hg[HALO - half:HALO - half + rows]
        for o in range(1 - half, half):
            tot = tot + hg[HALO + o:HALO + o + rows]
        cnt = (jnp.minimum(tpos + half, seq) - jnp.maximum(tpos - half, 0)).astype(F32)
        dlt = tot / cnt - hg[HALO:HALO + rows]
        ys.append(_dot(dlt.astype(BF16), pw_ref[g]))
    y = (jnp.concatenate(ys, axis=1) + pb_ref[...]) * ps_ref[...]
    x1 = xm + g1 * y
    _tail(x1, mod, first, n2_ref, rw_ref, rb_ref, cin_ref, x1_ref, h2_ref, route_refs)


def _pool_layer(x, mod8, n1, n2, pw, pb, ps, rw, rb, carry_in):
    bsz, seq, d = x.shape
    rows = SEQ_TILE
    per = seq // rows
    grid = (bsz, per)
    const2 = lambda b, j: (0, 0)
    route_specs, route_shapes = _route_specs(bsz * per, per, rows)
    return pl.pallas_call(
        functools.partial(_pool_kernel, rows=rows, seq=seq),
        grid=grid,
        in_specs=[
            pl.BlockSpec((1, seq, d), lambda b, j: (b, 0, 0)),
            pl.BlockSpec((1, SUBLANES, d), lambda b, j: (b, 0, 0)),
            pl.BlockSpec((1, d), const2),
            pl.BlockSpec((1, d), const2),
            pl.BlockSpec(pw.shape, lambda b, j: (0, 0, 0)),
            pl.BlockSpec((1, d), const2),
            pl.BlockSpec((1, d), const2),
            pl.BlockSpec(rw.shape, const2),
            pl.BlockSpec((1, LANES), const2),
            pl.BlockSpec((SUBLANES, LANES), const2),
        ],
        out_specs=[
            pl.BlockSpec((1, rows, d), lambda b, j: (b, j, 0)),
            pl.BlockSpec((1, rows, d), lambda b, j: (b, j, 0)),
        ] + route_specs,
        out_shape=[
            jax.ShapeDtypeStruct((bsz, seq, d), F32),
            jax.ShapeDtypeStruct((bsz, seq, d), BF16),
        ] + route_shapes,
        compiler_params=_cparams(("arbitrary", "arbitrary")),
        name="pool_layer",
    )(x, mod8, n1, n2, pw, pb, ps, rw, rb, carry_in)


def _chunk_copies(go, list_ref, make):
    n = list_ref[CHUNK_LANES - 1]
    groups = n // CHUNK_UNROLL

    def one(c):
        go(make(pl.multiple_of(c * SUBLANES, SUBLANES), pl.multiple_of(list_ref[c], SUBLANES)))

    def group(g, carry):
        for u in range(CHUNK_UNROLL):
            one(g * CHUNK_UNROLL + u)
        return carry

    def rest(c, carry):
        one(c)
        return carry

    lax.fori_loop(0, groups, group, 0)
    lax.fori_loop(groups * CHUNK_UNROLL, n, rest, 0)


def _dispatch_kernel(ends_ref, nu_ref, list_ref, prev_ref, post_ref, *refs, tiles, n_blocks):
    n_src = len(tiles)
    srcs, (xs_ref, sorted_ref, zeros, sem, zsem) = refs[:n_src], refs[n_src:]
    i = pl.program_id(0)
    blk = EXPERT_BLOCK

    @pl.when(i == 0)
    def _():
        zeros[...] = jnp.zeros_like(zeros)

        def fill(b):
            return pltpu.make_async_copy(zeros, xs_ref.at[pl.ds(b * blk, blk)], zsem)

        def expert_tails(go):
            for e in range(N_EXPERTS):
                lo = ends_ref[e - 1] if e else 0

                @pl.when(ends_ref[e] > lo)
                def _(e=e):
                    go(fill(ends_ref[e] // blk - 1))

        def unused(go):
            def body(b, carry):
                go(fill(b))
                return carry
            lax.fori_loop(nu_ref[0], n_blocks, body, 0)

        expert_tails(lambda cp: cp.start())
        unused(lambda cp: cp.start())
        expert_tails(lambda cp: cp.wait())
        unused(lambda cp: cp.wait())

    h = srcs[0][...]
    first = tiles[0]
    for src, n in zip(srcs[1:], tiles[1:]):
        h = jnp.where(i >= first, src[...], h)
        first += n

    pos = post_ref[...]
    j = lax.broadcasted_iota(jnp.int32, (SORT_ROWS, h.shape[0]), 0)
    onehot = jnp.zeros(j.shape, F32)
    for k in range(TOP_K):
        onehot = jnp.where(j == pos[k:k + 1], 1.0, onehot)
    slot = i % 2
    sorted_ref[slot] = _pack_rows(_dot(onehot.astype(BF16), h))

    def make(buf):
        def build(local, glob):
            return pltpu.make_async_copy(sorted_ref.at[buf, pl.ds(local, SUBLANES)],
                                         xs_ref.at[pl.ds(glob, SUBLANES)], sem.at[buf])
        return build

    _chunk_copies(lambda cp: cp.start(), list_ref, make(slot))

    @pl.when(i > 0)
    def _():
        _chunk_copies(lambda cp: cp.wait(), prev_ref, make(1 - slot))

    @pl.when(i == pl.num_programs(0) - 1)
    def _():
        _chunk_copies(lambda cp: cp.wait(), list_ref, make(slot))


def _dispatch(chunks, ends, n_used, pos_t, sources, n_blocks):
    rows = SEQ_TILE
    d = sources[0].shape[1]
    tiles = tuple(s.shape[0] // rows for s in sources)
    firsts = [sum(tiles[:k]) for k in range(len(tiles))]

    def src_spec(first, n):
        return pl.BlockSpec((rows, d), lambda i, *_: (jnp.clip(i - first, 0, n - 1), 0))

    grid_spec = pltpu.PrefetchScalarGridSpec(
        num_scalar_prefetch=2,
        grid=(sum(tiles),),
        in_specs=[pl.BlockSpec((CHUNK_LANES,), lambda i, *_: (i,), memory_space=pltpu.SMEM),
                  pl.BlockSpec((CHUNK_LANES,), lambda i, *_: (jnp.maximum(i - 1, 0),),
                               memory_space=pltpu.SMEM),
                  pl.BlockSpec((SUBLANES, rows), lambda i, *_: (0, i))]
        + [src_spec(f, n) for f, n in zip(firsts, tiles)],
        out_specs=pl.BlockSpec(memory_space=pl.ANY),
        scratch_shapes=[pltpu.VMEM((2, SORT_ROWS, d // 2), jnp.int32),
                        pltpu.VMEM((EXPERT_BLOCK, d // 2), jnp.int32),
                        pltpu.SemaphoreType.DMA((2,)), pltpu.SemaphoreType.DMA],
    )
    return pl.pallas_call(
        functools.partial(_dispatch_kernel, tiles=tiles, n_blocks=n_blocks),
        grid_spec=grid_spec,
        out_shape=jax.ShapeDtypeStruct((n_blocks * EXPERT_BLOCK, d // 2), jnp.int32),
        compiler_params=_cparams(("arbitrary",)),
        name="dispatch",
    )(ends, n_used, chunks, chunks, pos_t, *sources)


def _experts_kernel(be_ref, nu_ref, hf_ref, od_ref, nx_ref, xs_ref, wgu_hbm, bgu_ref, wd_hbm,
                    bd_ref, ys_ref, wgu_bf, wd_bf, wgu_in, wd_in, gsem, dsem, *, rows, dff, layer):
    b = pl.program_id(0)
    prev = be_ref[jnp.maximum(b - 1, 0)]
    fresh = (b == 0) | (be_ref[b] != prev)
    live = b < nu_ref[0]
    half = hf_ref[b] == 1

    def fetch(expert, slot):
        return (pltpu.make_async_copy(wgu_hbm.at[layer, expert], wgu_in.at[slot], gsem.at[slot]),
                pltpu.make_async_copy(wd_hbm.at[layer, expert], wd_in.at[slot], dsem.at[slot]))

    @pl.when(live & fresh)
    def _():
        slot = od_ref[b] % 2

        @pl.when(b == 0)
        def _():
            for cp in fetch(be_ref[b], slot):
                cp.start()

        for cp in fetch(be_ref[b], slot):
            cp.wait()
        wgu_bf[...] = wgu_in[slot].astype(BF16)
        wd_bf[...] = wd_in[slot].astype(BF16)

        @pl.when(nx_ref[b] >= 0)
        def _():
            for cp in fetch(nx_ref[b], 1 - slot):
                cp.start()

    def ffn(packed):
        x = _unpack_rows(packed).astype(BF16)
        gu = _dot(x, wgu_bf[...]) + bgu_ref[0, 0]
        gate = jnp.minimum(gu[:, :dff], SWIGLU_LIMIT)
        up = jnp.clip(gu[:, dff:], -SWIGLU_LIMIT, SWIGLU_LIMIT)
        act = (up + 1.0) * (gate * _sigmoid(SWIGLU_ALPHA * gate))
        y = _dot(act.astype(BF16), wd_bf[...]) + bd_ref[0, 0]
        return _pack_rows(y.astype(BF16).astype(F32))

    @pl.when(live & jnp.logical_not(half))
    def _():
        ys_ref[...] = ffn(xs_ref[...])

    @pl.when(live & half)
    def _():
        ys_ref[:rows // 2] = ffn(xs_ref[:rows // 2])
        ys_ref[rows // 2:] = jnp.zeros((rows // 2, ys_ref.shape[1]), jnp.int32)

    @pl.when(jnp.logical_not(live))
    def _():
        ys_ref[...] = jnp.zeros_like(ys_ref)


def _experts(layer, block_expert, n_used, half, order, following, xs, w_gu, b_gu, w_down, b_down):
    n_rows = xs.shape[0]
    rows = EXPERT_BLOCK
    n_blocks = n_rows // rows
    depth, e, d, dff2 = w_gu.shape
    dff = dff2 // 2
    blk = lambda b, be, nu, *_: (jnp.minimum(b, nu[0] - 1), 0)
    per_expert = lambda b, be, *_: (layer, be[b], 0, 0)
    grid_spec = pltpu.PrefetchScalarGridSpec(
        num_scalar_prefetch=5,
        grid=(n_blocks,),
        in_specs=[
            pl.BlockSpec((rows, d // 2), blk),
            pl.BlockSpec(memory_space=pl.ANY),
            pl.BlockSpec((1, 1, 1, dff2), per_expert),
            pl.BlockSpec(memory_space=pl.ANY),
            pl.BlockSpec((1, 1, 1, d), per_expert),
        ],
        out_specs=pl.BlockSpec((rows, d // 2), lambda b, *_: (b, 0)),
        scratch_shapes=[pltpu.VMEM((d, dff2), BF16), pltpu.VMEM((dff, d), BF16),
                        pltpu.VMEM((2, d, dff2), F32), pltpu.VMEM((2, dff, d), F32),
                        pltpu.SemaphoreType.DMA((2,)), pltpu.SemaphoreType.DMA((2,))],
    )
    return pl.pallas_call(
        functools.partial(_experts_kernel, rows=rows, dff=dff, layer=layer),
        grid_spec=grid_spec,
        out_shape=jax.ShapeDtypeStruct((n_rows, d // 2), jnp.int32),
        compiler_params=_cparams(("arbitrary",)),
        name="experts",
    )(block_expert, n_used, half, order, following, xs, w_gu, b_gu.reshape(depth, e, 1, dff2),
      w_down, b_down.reshape(depth, e, 1, d))


def _combine_kernel(list_ref, next_ref, ys_ref, pos_ref, gt_ref, x1_ref, mod_ref, fg_ref, out_ref,
                    sorted_ref, sem, *, per, final):
    step = pl.program_id(0) * per + pl.program_id(1)
    steps = pl.num_programs(0) * per
    slot = step % 2

    def make(buf):
        def build(local, glob):
            return pltpu.make_async_copy(ys_ref.at[pl.ds(glob, SUBLANES)],
                                         sorted_ref.at[buf, pl.ds(local, SUBLANES)], sem.at[buf])
        return build

    @pl.when(step == 0)
    def _():
        sorted_ref[...] = jnp.zeros_like(sorted_ref)
        _chunk_copies(lambda cp: cp.start(), list_ref, make(slot))

    @pl.when(step + 1 < steps)
    def _():
        _chunk_copies(lambda cp: cp.start(), next_ref, make(1 - slot))

    _chunk_copies(lambda cp: cp.wait(), list_ref, make(slot))

    pos = pos_ref[...]
    gates = gt_ref[...]
    lane = lax.broadcasted_iota(jnp.int32, (pos.shape[0], SORT_ROWS), 1)
    weights = jnp.zeros(lane.shape, F32)
    for k in range(TOP_K):
        weights = jnp.where(lane == pos[:, k:k + 1], gates[:, k:k + 1], weights)
    y = _dot(weights.astype(BF16), _unpack_rows(sorted_ref[slot]).astype(BF16))
    x2 = x1_ref[0] + mod_ref[0][5:6] * y
    if final:
        x2 = _rms(x2, fg_ref[...])
    out_ref[0] = x2


def _combine(chunks, tile0, pos, gates, ys, x1, mod8, final_g, final):
    bsz, seq, d = x1.shape
    rows = SEQ_TILE
    per = seq // rows
    last = tile0 + bsz * per - 1
    tok = pl.BlockSpec((rows, TOP_K), lambda b, j: (tile0 + b * per + j, 0))
    return pl.pallas_call(
        functools.partial(_combine_kernel, per=per, final=final),
        grid=(bsz, per),
        in_specs=[
            pl.BlockSpec((CHUNK_LANES,), lambda b, j: (tile0 + b * per + j,),
                         memory_space=pltpu.SMEM),
            pl.BlockSpec((CHUNK_LANES,), lambda b, j: (jnp.minimum(tile0 + b * per + j + 1, last),),
                         memory_space=pltpu.SMEM),
            pl.BlockSpec(memory_space=pl.ANY),
            tok, tok,
            pl.BlockSpec((1, rows, d), lambda b, j: (b, j, 0)),
            pl.BlockSpec((1, SUBLANES, d), lambda b, j: (b, 0, 0)),
            pl.BlockSpec((1, d), lambda b, j: (0, 0)),
        ],
        out_specs=pl.BlockSpec((1, rows, d), lambda b, j: (b, j, 0)),
        out_shape=jax.ShapeDtypeStruct((bsz, seq, d), F32),
        scratch_shapes=[pltpu.VMEM((2, SORT_ROWS, d // 2), jnp.int32),
                        pltpu.SemaphoreType.DMA((2,))],
        compiler_params=_cparams(("arbitrary", "arbitrary")),
        name="combine",
    )(chunks, chunks, ys, pos, gates, x1, mod8, final_g)


def _moe(layer, streams, w_gu, b_gu, w_down, b_down, final_g, final):
    d = streams[0][0].shape[2]
    routes = [s[2] for s in streams]
    pos = jnp.concatenate([r[0] for r in routes], axis=0)
    pos_t = jnp.concatenate([r[1] for r in routes], axis=1)
    gates = jnp.concatenate([r[2] for r in routes], axis=0)
    chunk = jnp.concatenate([r[3] for r in routes], axis=0)
    span = routes[-1][4]
    t = pos.shape[0]
    tiles = chunk.shape[0]
    ends = span[0, :N_EXPERTS]
    starts = span[1, :N_EXPERTS]
    lane = jnp.arange(CHUNK_LANES, dtype=jnp.int32)
    hot = chunk[:, 1, :, None] == jnp.arange(N_EXPERTS, dtype=jnp.int32)
    base = jnp.sum(jnp.where(hot, starts, 0), axis=-1)
    chunks = jnp.where(lane == CHUNK_LANES - 1, chunk[:, 2], chunk[:, 0] + base).reshape(-1)
    blk = EXPERT_BLOCK
    max_rows = t * TOP_K + tiles * N_EXPERTS * (SUBLANES - 1)
    n_blocks = -(-max_rows // blk) + N_EXPERTS
    n_used = ends[N_EXPERTS - 1] // blk
    block_start = jnp.arange(n_blocks, dtype=jnp.int32) * blk
    bexp = jnp.sum((ends[None, :] <= block_start[:, None]).astype(jnp.int32), axis=1)
    last = jnp.sum((ends <= (n_used - 1) * blk).astype(jnp.int32))
    bexp = jnp.where(jnp.arange(n_blocks) < n_used, bexp, last)
    used = routes[-1][5][0, :N_EXPERTS].astype(jnp.int32)
    mine = bexp[:, None] == jnp.arange(N_EXPERTS, dtype=jnp.int32)
    left = jnp.sum(jnp.where(mine, starts + used, 0), axis=-1) - block_start
    half = (left <= blk // 2).astype(jnp.int32)
    ids = jnp.arange(N_EXPERTS, dtype=jnp.int32)
    owns = ends > starts
    rank = jnp.cumsum(owns.astype(jnp.int32)) - 1
    later = jnp.where(owns[None, :] & (ids[None, :] > ids[:, None]), ids[None, :], N_EXPERTS)
    after = jnp.min(later, axis=1)
    after = jnp.where(after == N_EXPERTS, -1, after)
    order = jnp.sum(jnp.where(mine, rank, 0), axis=-1)
    following = jnp.sum(jnp.where(mine, after, 0), axis=-1)
    sources = [s[1].reshape(-1, d) for s in streams]
    n_used = n_used.reshape(1)
    xs = _dispatch(chunks, ends, n_used, pos_t, sources, n_blocks)
    ys = _experts(layer, bexp, n_used, half, order, following, xs, w_gu, b_gu, w_down, b_down)
    outs, tile0 = [], 0
    for x1, _, _, mod8 in streams:
        outs.append(_combine(chunks, tile0, pos, gates, ys, x1, mod8, final_g, final))
        tile0 += x1.shape[0] * x1.shape[1] // SEQ_TILE
    return outs


def _rope(x, cos, sin):
    return x * cos + pltpu.roll(x, LANES // 2, axis=1) * sin


def _proj_kernel(xc_ref, x_ref, modc_ref, modl_ref, n1_ref, cos_ref, sin_ref,
                 wdq_ref, qg_ref, wuq_ref, wdkv_ref, kvg_ref, wukv_ref,
                 q_ref, k_ref, v_ref, *, ctx_tiles):
    j = pl.program_id(1)
    is_ctx = j < ctx_tiles
    xin = jnp.where(is_ctx, xc_ref[0], x_ref[0])
    mod = jnp.where(is_ctx, modc_ref[0], modl_ref[0])
    sh1, s1 = mod[0:1], mod[1:2]
    hb = (_rms(xin, n1_ref[...]) * (1.0 + s1) + sh1).astype(BF16)
    cos = jnp.where(is_ctx, 1.0, cos_ref[...])
    sin = jnp.where(is_ctx, 0.0, sin_ref[...])

    kva = _dot(hb, wdkv_ref[...])
    kpe = _rope(kva[:, KV_LORA:], cos, sin).astype(BF16)
    kv = _dot(_rms(kva[:, :KV_LORA], kvg_ref[...]).astype(BF16), wukv_ref[...])
    nope_all = N_HEADS * QK_NOPE
    ones = jnp.ones((kv.shape[0], V_HEAD), BF16)
    for h in range(N_HEADS):
        k_ref[0, :, h * HEAD_PAD:h * HEAD_PAD + QK_NOPE] = \
            kv[:, h * QK_NOPE:(h + 1) * QK_NOPE].astype(BF16)
        k_ref[0, :, h * HEAD_PAD + QK_NOPE:(h + 1) * HEAD_PAD] = kpe
        v_ref[0, :, h * HEAD_PAD:h * HEAD_PAD + V_HEAD] = \
            kv[:, nope_all + h * V_HEAD:nope_all + (h + 1) * V_HEAD].astype(BF16)
        v_ref[0, :, h * HEAD_PAD + V_HEAD:(h + 1) * HEAD_PAD] = ones

    @pl.when(jnp.logical_not(is_ctx))
    def _():
        qa = _dot(hb, wdq_ref[...])
        q = _dot(_rms(qa, qg_ref[...]).astype(BF16), wuq_ref[...]) * Q_SCALE
        for h in range(N_HEADS):
            lo = h * HEAD_PAD
            q_ref[0, :, lo:lo + QK_NOPE] = q[:, lo:lo + QK_NOPE].astype(BF16)
            q_ref[0, :, lo + QK_NOPE:lo + HEAD_PAD] = \
                _rope(q[:, lo + QK_NOPE:lo + HEAD_PAD], cos, sin).astype(BF16)


def _projections(xc, x, modc8, modl8, n1, cos_t, sin_t, wdq, qg, wuq, wdkv, kvg, wukv):
    bsz, seq, d = x.shape
    ctx = xc.shape[1]
    rows = SEQ_TILE
    ctx_tiles = ctx // rows
    lat = lambda b, j: (b, jnp.maximum(j - ctx_tiles, 0), 0)
    full = lambda a: pl.BlockSpec(a.shape, lambda b, j: (0,) * a.ndim)
    kw = N_HEADS * HEAD_PAD
    vw = N_HEADS * HEAD_PAD
    return pl.pallas_call(
        functools.partial(_proj_kernel, ctx_tiles=ctx_tiles),
        grid=(bsz, (ctx + seq) // rows),
        in_specs=[
            pl.BlockSpec((1, rows, d), lambda b, j: (b, jnp.minimum(j, ctx_tiles - 1), 0)),
            pl.BlockSpec((1, rows, d), lat),
            pl.BlockSpec((1, SUBLANES, d), lambda b, j: (b, 0, 0)),
            pl.BlockSpec((1, SUBLANES, d), lambda b, j: (b, 0, 0)),
            full(n1),
            pl.BlockSpec((rows, LANES), lambda b, j: (jnp.maximum(j - ctx_tiles, 0), 0)),
            pl.BlockSpec((rows, LANES), lambda b, j: (jnp.maximum(j - ctx_tiles, 0), 0)),
            full(wdq), full(qg), full(wuq), full(wdkv), full(kvg), full(wukv),
        ],
        out_specs=[
            pl.BlockSpec((1, rows, kw), lat),
            pl.BlockSpec((1, rows, kw), lambda b, j: (b, j, 0)),
            pl.BlockSpec((1, rows, vw), lambda b, j: (b, j, 0)),
        ],
        out_shape=[
            jax.ShapeDtypeStruct((bsz, seq, kw), BF16),
            jax.ShapeDtypeStruct((bsz, ctx + seq, kw), BF16),
            jax.ShapeDtypeStruct((bsz, ctx + seq, vw), BF16),
        ],
        compiler_params=_cparams(("arbitrary", "arbitrary")),
        name="projections",
    )(xc, x, modc8, modl8, n1, cos_t, sin_t, wdq, qg, wuq, wdkv, kvg, wukv)


def _attn_kernel(q_ref, k_ref, v_ref, o_ref, *, chunk):
    q = q_ref[0]
    m = acc = None
    for c in range(k_ref.shape[1] // chunk):
        rows = slice(c * chunk, (c + 1) * chunk)
        s = lax.dot_general(q, k_ref[0, rows, :], (((1,), (1,)), ((), ())),
                            preferred_element_type=F32)
        m_c = jnp.max(s, axis=-1, keepdims=True)
        m_new = m_c if c == 0 else jnp.maximum(m, m_c)
        pv = _dot(jnp.exp2(s - m_new).astype(BF16), v_ref[0, rows, :])
        acc = pv if c == 0 else acc * jnp.exp2(m - m_new) + pv
        m = m_new
    o_ref[0] = (acc[:, :V_HEAD] / acc[:, V_HEAD:]).astype(BF16)


def _attention(q, k, v):
    bsz, seq, _ = q.shape
    keys = k.shape[1]
    rows = min(Q_TILE, seq)
    chunk = next(c for c in (KEY_CHUNK, 2 * LANES, keys) if keys % c == 0)
    return pl.pallas_call(
        functools.partial(_attn_kernel, chunk=chunk),
        grid=(bsz, N_HEADS, seq // rows),
        in_specs=[
            pl.BlockSpec((1, rows, HEAD_PAD), lambda b, h, i: (b, i, h)),
            pl.BlockSpec((1, keys, HEAD_PAD), lambda b, h, i: (b, 0, h)),
            pl.BlockSpec((1, keys, HEAD_PAD), lambda b, h, i: (b, 0, h)),
        ],
        out_specs=pl.BlockSpec((1, rows, V_HEAD), lambda b, h, i: (b, i, h)),
        out_shape=jax.ShapeDtypeStruct((bsz, seq, N_HEADS * V_HEAD), BF16),
        compiler_params=_cparams(("arbitrary", "arbitrary", "arbitrary")),
        name="attention",
    )(q, k, v)


def _oproj_kernel(o_ref, x_ref, mod_ref, wo_ref, n2_ref, rw_ref, rb_ref, cin_ref,
                  x1_ref, h2_ref, *route_refs):
    first = (pl.program_id(0) == 0) & (pl.program_id(1) == 0)
    mod = mod_ref[0]
    x1 = x_ref[0] + mod[2:3] * _dot(o_ref[0], wo_ref[...])
    _tail(x1, mod, first, n2_ref, rw_ref, rb_ref, cin_ref, x1_ref, h2_ref, route_refs)


def _out_projection(o, x, mod8, wo, n2, rw, rb, carry_in):
    bsz, seq, d = x.shape
    rows = SEQ_TILE
    per = seq // rows
    const2 = lambda b, j: (0, 0)
    route_specs, route_shapes = _route_specs(bsz * per, per, rows)
    return pl.pallas_call(
        _oproj_kernel,
        grid=(bsz, per),
        in_specs=[
            pl.BlockSpec((1, rows, o.shape[2]), lambda b, j: (b, j, 0)),
            pl.BlockSpec((1, rows, d), lambda b, j: (b, j, 0)),
            pl.BlockSpec((1, SUBLANES, d), lambda b, j: (b, 0, 0)),
            pl.BlockSpec(wo.shape, const2),
            pl.BlockSpec((1, d), const2),
            pl.BlockSpec(rw.shape, const2),
            pl.BlockSpec((1, LANES), const2),
            pl.BlockSpec((SUBLANES, LANES), const2),
        ],
        out_specs=[
            pl.BlockSpec((1, rows, d), lambda b, j: (b, j, 0)),
            pl.BlockSpec((1, rows, d), lambda b, j: (b, j, 0)),
        ] + route_specs,
        out_shape=[
            jax.ShapeDtypeStruct((bsz, seq, d), F32),
            jax.ShapeDtypeStruct((bsz, seq, d), BF16),
        ] + route_shapes,
        compiler_params=_cparams(("arbitrary", "arbitrary")),
        name="out_projection",
    )(o, x, mod8, wo, n2, rw, rb, carry_in)


def _rope_cols(base):
    q = QK_ROPE // 4
    x1 = list(range(base, base + q)) + list(range(base + 2 * q, base + 3 * q))
    x2 = list(range(base + q, base + 2 * q)) + list(range(base + 3 * q, base + 4 * q))
    pad = [-1] * (LANES // 2 - 2 * q)
    return x1 + pad + x2 + pad


def _take_cols(w, cols):
    wz = jnp.concatenate([w, jnp.zeros((w.shape[0], 1), w.dtype)], axis=1)
    idx = jnp.asarray([c if c >= 0 else w.shape[1] for c in cols], jnp.int32)
    return jnp.take(wz, idx, axis=1)


def _rope_tables(seq):
    q = QK_ROPE // 4
    pos = jnp.arange(seq)
    inv = ROPE_BASE ** (-jnp.arange(0, QK_ROPE // 2, 2, dtype=F32) / (QK_ROPE // 2))
    ang = jnp.concatenate([(pos // GRID_W).astype(F32)[:, None] * inv,
                           (pos % GRID_W).astype(F32)[:, None] * inv], axis=1)
    pad1 = jnp.ones((seq, LANES // 2 - 2 * q), F32)
    pad0 = jnp.zeros((seq, LANES // 2 - 2 * q), F32)
    cos = jnp.concatenate([jnp.cos(ang), pad1, jnp.cos(ang), pad1], axis=1)
    sin = jnp.concatenate([-jnp.sin(ang), pad0, jnp.sin(ang), pad0], axis=1)
    return cos, sin


def _mod8(mod_rows):
    bsz, n = mod_rows.shape
    d = n // 6
    m = mod_rows.reshape(bsz, 6, d)
    return jnp.concatenate([m, jnp.zeros((bsz, SUBLANES - 6, d), F32)], axis=1)


def kernel(x, c, ctx, c_ctx, ada_w, ada_b, norm1_g, norm2_g, pool_w, pool_b, pool_scale,
           w_dq, q_norm_g, w_uq, w_dkv, kv_norm_g, w_ukv, w_o,
           router_w, router_b, w_gu, b_gu, w_down, b_down, final_g):
    bsz, seq, d = x.shape
    assert seq % SEQ_TILE == 0 and ctx.shape[1] % SEQ_TILE == 0 and seq % GRID_W == 0
    row = lambda a: a.reshape(1, -1)

    mod_rows = 2 * SUBLANES
    cvec = jnp.concatenate([c, c_ctx[None], jnp.zeros((mod_rows - bsz - 1, d), F32)], axis=0)
    mod = _modulation(cvec, ada_w, ada_b)
    modl = [_mod8(mod[i, :bsz]) for i in range(2)]
    modc = [_mod8(jnp.broadcast_to(mod[i, bsz:bsz + 1], (bsz, 6 * d))) for i in range(2)]

    n_exp = router_w.shape[2]
    rw = jnp.pad(router_w, ((0, 0), (0, 0), (0, LANES - n_exp)))
    rb = jnp.pad(router_b, ((0, 0), (0, LANES - n_exp)), constant_values=-jnp.inf)

    pw = pool_w[0].astype(BF16)
    args0 = (row(norm1_g[0]), row(norm2_g[0]), pw, row(pool_b[0]), row(pool_scale[0]),
             rw[0], row(rb[0]))
    no_rows = jnp.zeros((SUBLANES, LANES), F32)
    x1, h2, *route_l = _pool_layer(x, modl[0], *args0, no_rows)
    xc1, h2c, *route_c = _pool_layer(ctx, modc[0], *args0, route_l[5])
    x, xc = _moe(0, [(x1, h2, route_l, modl[0]), (xc1, h2c, route_c, modc[0])],
                 w_gu, b_gu, w_down, b_down, row(final_g), False)

    head = QK_NOPE + QK_ROPE
    q_cols = []
    for h in range(N_HEADS):
        q_cols += list(range(h * head, h * head + QK_NOPE)) + _rope_cols(h * head + QK_NOPE)
    kv_cols = list(range(KV_LORA)) + _rope_cols(KV_LORA)
    hk = QK_NOPE + V_HEAD
    ukv_cols = [h * hk + i for h in range(N_HEADS) for i in range(QK_NOPE)] + \
               [h * hk + QK_NOPE + i for h in range(N_HEADS) for i in range(V_HEAD)]
    wuq = _take_cols(w_uq[0], q_cols).astype(BF16)
    wdkv = _take_cols(w_dkv[0], kv_cols).astype(BF16)
    wukv = _take_cols(w_ukv[0], ukv_cols).astype(BF16)
    cos_t, sin_t = _rope_tables(seq)
    q, k, v = _projections(xc, x, modc[1], modl[1], row(norm1_g[1]), cos_t, sin_t,
                           w_dq[0].astype(BF16), row(q_norm_g[0]), wuq, wdkv,
                           row(kv_norm_g[0]), wukv)
    o = _attention(q, k, v)
    x1, h2, *route_l = _out_projection(o, x, modl[1], w_o[0].astype(BF16), row(norm2_g[1]),
                                       rw[1], row(rb[1]), no_rows)
    (out,) = _moe(1, [(x1, h2, route_l, modl[1])], w_gu, b_gu, w_down, b_down, row(final_g), True)
    return out
```

```python
import functools

import jax
import jax.numpy as jnp
from jax import lax
from jax.experimental import pallas as pl
from jax.experimental.pallas import tpu as pltpu

F32 = jnp.float32
BF16 = jnp.bfloat16

N_HEADS = 8
QK_NOPE = 128
QK_ROPE = 64
V_HEAD = 128
KV_LORA = 256
N_EXPERTS = 32
TOP_K = 4
POOL_WINDOWS = (2, 4, 8, 16)
GRID_W = 64
ROPE_BASE = 10000.0
ATTN_SCALE = (QK_NOPE + QK_ROPE) ** -0.5
Q_SCALE = ATTN_SCALE * 1.4426950408889634
SWIGLU_LIMIT = 7.0
SWIGLU_ALPHA = 1.702
EPS = 1e-6

LANES = 128
SUBLANES = 8
HEAD_PAD = 2 * LANES
VMEM_LIMIT = 56 * 1024 * 1024

SEQ_TILE = 256
EXPERT_BLOCK = 512
Q_TILE = 2048
KEY_CHUNK = 256
MOD_COLS = 1536
HALO = 8
SORT_ROWS = -(-(TOP_K * SEQ_TILE + N_EXPERTS * (SUBLANES - 1)) // (2 * LANES)) * (2 * LANES)
CHUNK_LANES = -(-(SORT_ROWS // SUBLANES + 1) // LANES) * LANES
CHUNK_UNROLL = 4


def _cparams(sem, vmem=VMEM_LIMIT):
    return pltpu.CompilerParams(dimension_semantics=sem, vmem_limit_bytes=vmem)


def _dot(a, b):
    return jnp.dot(a, b, preferred_element_type=F32)


def _dot3(a, b):
    ah = a.astype(BF16)
    al = (a - ah.astype(F32)).astype(BF16)
    bh = b.astype(BF16)
    bl = (b - bh.astype(F32)).astype(BF16)
    return _dot(ah, bh) + _dot(ah, bl) + _dot(al, bh)


def _rms(x, g):
    return x * lax.rsqrt(jnp.mean(x * x, axis=-1, keepdims=True) + EPS) * g


def _sigmoid(x):
    return 1.0 / (1.0 + jnp.exp(-x))


HIGH_HALF = -65536


def _pack_rows(x):
    half = x.shape[1] // 2
    lo = lax.bitcast_convert_type(x[:, :half], jnp.int32)
    hi = lax.bitcast_convert_type(x[:, half:], jnp.int32)
    return (hi & HIGH_HALF) | lax.shift_right_logical(lo, 16)


def _unpack_rows(w):
    lo = lax.bitcast_convert_type(lax.shift_left(w, 16), F32)
    hi = lax.bitcast_convert_type(w & HIGH_HALF, F32)
    return jnp.concatenate([lo, hi], axis=1)


def _mod_kernel(c_ref, w_ref, b_ref, o_ref):
    a = c_ref[...]
    s = a * _sigmoid(a)
    o_ref[0] = _dot3(s, w_ref[0]) + b_ref[0]


def _modulation(cvec, ada_w, ada_b):
    depth, d, n = ada_w.shape
    rows = cvec.shape[0]
    return pl.pallas_call(
        _mod_kernel,
        grid=(depth, n // MOD_COLS),
        in_specs=[
            pl.BlockSpec((rows, d), lambda i, j: (0, 0)),
            pl.BlockSpec((1, d, MOD_COLS), lambda i, j: (i, 0, j)),
            pl.BlockSpec((1, 1, MOD_COLS), lambda i, j: (i, 0, j)),
        ],
        out_specs=pl.BlockSpec((1, rows, MOD_COLS), lambda i, j: (i, 0, j)),
        out_shape=jax.ShapeDtypeStruct((depth, rows, n), F32),
        compiler_params=_cparams(("arbitrary", "arbitrary")),
        name="modulation",
    )(cvec, ada_w, ada_b.reshape(depth, 1, n))


def _route_tile(work, first, cin_ref, pos_ref, post_ref, gt_ref, chunk_ref, span_ref, carry_ref):
    rows = work.shape[0]
    lane = lax.broadcasted_iota(jnp.int32, work.shape, 1).astype(F32)
    tops, hots = [], []
    for _ in range(TOP_K):
        m = jnp.max(work, axis=1, keepdims=True)
        idx = jnp.min(jnp.where(work == m, lane, float(LANES)), axis=1, keepdims=True)
        hot = lane == idx
        work = jnp.where(hot, -jnp.inf, work)
        tops.append(m)
        hots.append(hot)
    chosen = jnp.zeros(work.shape, F32)
    for hot in hots:
        chosen = chosen + jnp.where(hot, 1.0, 0.0)
    count = jnp.broadcast_to(jnp.sum(chosen, axis=0, keepdims=True), carry_ref.shape)
    seg_rows = jnp.floor((count + (SUBLANES - 1.0)) * (1.0 / SUBLANES)) * SUBLANES

    def lane_prefix(v):
        shift = 1
        while shift < N_EXPERTS:
            v = v + pltpu.roll(v, shift, axis=1)
            shift *= 2
        return v

    @pl.when(first)
    def _():
        carry_ref[...] = cin_ref[...]

    r = lax.broadcasted_iota(jnp.int32, (rows, rows), 0)
    c = lax.broadcasted_iota(jnp.int32, (rows, rows), 1)
    earlier = jnp.where(c < r, 1.0, 0.0).astype(BF16)
    run_end = lane_prefix(seg_rows)
    run_start = run_end - seg_rows
    before = _dot(earlier, chosen.astype(BF16)) + run_start[0:1]
    exps = [jnp.exp(t - tops[0]) for t in tops]
    den = exps[0] + exps[1] + exps[2] + exps[3]
    col = lax.broadcasted_iota(jnp.int32, (rows, TOP_K), 1)
    wide = lax.broadcasted_iota(jnp.int32, (rows, LANES), 1)
    gt = jnp.zeros((rows, TOP_K), F32)
    ps = jnp.zeros((rows, TOP_K), F32)
    ps_wide = jnp.zeros((rows, LANES), F32)
    for k in range(TOP_K):
        pos_k = jnp.sum(jnp.where(hots[k], before, 0.0), axis=1, keepdims=True)
        gt = jnp.where(col == k, exps[k] / den, gt)
        ps = jnp.where(col == k, pos_k, ps)
        ps_wide = jnp.where(wide == k, pos_k, ps_wide)
    pos_ref[...] = ps.astype(jnp.int32)
    post_ref[...] = jnp.transpose(ps_wide)[0:SUBLANES].astype(jnp.int32)
    gt_ref[...] = gt

    square = (LANES, LANES)
    srow = lax.broadcasted_iota(jnp.int32, square, 0)
    packed = jnp.where(srow == 0, jnp.broadcast_to(run_start[0:1], square),
                       jnp.where(srow == 1, jnp.broadcast_to(run_end[0:1], square),
                                 jnp.broadcast_to((carry_ref[...] - run_start)[0:1], square)))
    cols = jnp.transpose(packed)
    start_c, end_c, shift_c = cols[:, 0:1], cols[:, 1:2], cols[:, 2:3]
    erow = lax.broadcasted_iota(jnp.int32, (LANES, CHUNK_LANES), 0)
    row0 = (lax.broadcasted_iota(jnp.int32, (LANES, CHUNK_LANES), 1) * SUBLANES).astype(F32)
    holds = (erow < N_EXPERTS) & (start_c <= row0) & (row0 < end_c)
    rel = jnp.sum(jnp.where(holds, shift_c, 0.0), axis=0, keepdims=True) + row0[0:1]
    eid = jnp.sum(jnp.where(holds, erow.astype(F32), 0.0), axis=0, keepdims=True)
    n_chunks = jnp.sum(jnp.where(holds, 1.0, 0.0), keepdims=True)
    crow = lax.broadcasted_iota(jnp.int32, (SUBLANES, CHUNK_LANES), 0)
    chunk_ref[0] = jnp.where(crow == 0, rel, jnp.where(crow == 1, eid, n_chunks)).astype(jnp.int32)

    total = carry_ref[...] + seg_rows
    carry_ref[...] = total
    sub = lax.broadcasted_iota(jnp.int32, carry_ref.shape, 0)
    blocks = jnp.floor((total + (EXPERT_BLOCK - 1.0)) * (1.0 / EXPERT_BLOCK))
    ends = lane_prefix(blocks) * EXPERT_BLOCK
    span_ref[...] = jnp.where(sub == 0, ends, jnp.where(sub == 1, ends - blocks * EXPERT_BLOCK,
                                                        0.0)).astype(jnp.int32)


def _route_specs(tiles, per, rows):
    tile = lambda b, j: b * per + j
    tok = pl.BlockSpec((rows, TOP_K), lambda b, j: (tile(b, j), 0))
    keep = pl.BlockSpec((SUBLANES, LANES), lambda b, j: (0, 0))
    specs = [tok, pl.BlockSpec((SUBLANES, rows), lambda b, j: (0, tile(b, j))), tok,
             pl.BlockSpec((1, SUBLANES, CHUNK_LANES), lambda b, j: (tile(b, j), 0, 0)), keep, keep]
    t = tiles * rows
    shapes = [jax.ShapeDtypeStruct((t, TOP_K), jnp.int32),
              jax.ShapeDtypeStruct((SUBLANES, t), jnp.int32),
              jax.ShapeDtypeStruct((t, TOP_K), F32),
              jax.ShapeDtypeStruct((tiles, SUBLANES, CHUNK_LANES), jnp.int32),
              jax.ShapeDtypeStruct((SUBLANES, LANES), jnp.int32),
              jax.ShapeDtypeStruct((SUBLANES, LANES), F32)]
    return specs, shapes


def _tail(x1, mod, first, n2_ref, rw_ref, rb_ref, cin_ref, x1_ref, h2_ref, route_refs):
    sh2, s2 = mod[3:4], mod[4:5]
    h2 = _rms(x1, n2_ref[...]) * (1.0 + s2) + sh2
    x1_ref[0] = x1
    h2_ref[0] = h2.astype(BF16)
    _route_tile(_dot3(h2, rw_ref[...]) + rb_ref[...], first, cin_ref, *route_refs)


def _pool_kernel(x_ref, mod_ref, n1_ref, n2_ref, pw_ref, pb_ref, ps_ref, rw_ref, rb_ref, cin_ref,
                 x1_ref, h2_ref, *route_refs, rows, seq):
    j = pl.program_id(1)
    first = (pl.program_id(0) == 0) & (j == 0)
    start = pl.multiple_of(j * rows, rows)
    prev0 = pl.multiple_of(jnp.maximum(start - HALO, 0), HALO)
    next0 = pl.multiple_of(jnp.minimum(start + rows, seq - HALO), HALO)
    xm = x_ref[0, pl.ds(start, rows), :]
    xe = jnp.concatenate(
        [x_ref[0, pl.ds(prev0, HALO), :], xm, x_ref[0, pl.ds(next0, HALO), :]], axis=0)
    mod = mod_ref[0]
    sh1, s1, g1 = mod[0:1], mod[1:2], mod[2:3]
    h = _rms(xe, n1_ref[...]) * (1.0 + s1) + sh1
    pos = start - HALO + lax.broadcasted_iota(jnp.int32, (rows + 2 * HALO, 1), 0)
    hz = jnp.where((pos >= 0) & (pos < seq), h, 0.0)
    tpos = start + lax.broadcasted_iota(jnp.int32, (rows, 1), 0)
    group = hz.shape[1] // len(POOL_WINDOWS)
    ys = []
    for g, w in enumerate(POOL_WINDOWS):
        half = w // 2
        hg = hz[:, g * group:(g + 1) * group]
        run, span = hg, 1
        while span < w:
            run = run[:run.shape[0] - span] + run[span:]
            span *= 2
        tot = run[HALO - half:HALO - half + rows]
        cnt = (jnp.minimum(tpos + half, seq) - jnp.maximum(tpos - half, 0)).astype(F32)
        dlt = tot / cnt - hg[HALO:HALO + rows]
        ys.append(_dot(dlt.astype(BF16), pw_ref[g]))
    y = (jnp.concatenate(ys, axis=1) + pb_ref[...]) * ps_ref[...]
    x1 = xm + g1 * y
    _tail(x1, mod, first, n2_ref, rw_ref, rb_ref, cin_ref, x1_ref, h2_ref, route_refs)


def _pool_layer(x, mod8, n1, n2, pw, pb, ps, rw, rb, carry_in):
    bsz, seq, d = x.shape
    rows = SEQ_TILE
    per = seq // rows
    grid = (bsz, per)
    const2 = lambda b, j: (0, 0)
    route_specs, route_shapes = _route_specs(bsz * per, per, rows)
    return pl.pallas_call(
        functools.partial(_pool_kernel, rows=rows, seq=seq),
        grid=grid,
        in_specs=[
            pl.BlockSpec((1, seq, d), lambda b, j: (b, 0, 0)),
            pl.BlockSpec((1, SUBLANES, d), lambda b, j: (b, 0, 0)),
            pl.BlockSpec((1, d), const2),
            pl.BlockSpec((1, d), const2),
            pl.BlockSpec(pw.shape, lambda b, j: (0, 0, 0)),
            pl.BlockSpec((1, d), const2),
            pl.BlockSpec((1, d), const2),
            pl.BlockSpec(rw.shape, const2),
            pl.BlockSpec((1, LANES), const2),
            pl.BlockSpec((SUBLANES, LANES), const2),
        ],
        out_specs=[
            pl.BlockSpec((1, rows, d), lambda b, j: (b, j, 0)),
            pl.BlockSpec((1, rows, d), lambda b, j: (b, j, 0)),
        ] + route_specs,
        out_shape=[
            jax.ShapeDtypeStruct((bsz, seq, d), F32),
            jax.ShapeDtypeStruct((bsz, seq, d), BF16),
        ] + route_shapes,
        compiler_params=_cparams(("arbitrary", "arbitrary")),
        name="pool_layer",
    )(x, mod8, n1, n2, pw, pb, ps, rw, rb, carry_in)


def _chunk_copies(go, list_ref, make):
    n = list_ref[CHUNK_LANES - 1]
    groups = n // CHUNK_UNROLL

    def one(c):
        go(make(pl.multiple_of(c * SUBLANES, SUBLANES), pl.multiple_of(list_ref[c], SUBLANES)))

    def group(g, carry):
        for u in range(CHUNK_UNROLL):
            one(g * CHUNK_UNROLL + u)
        return carry

    def rest(c, carry):
        one(c)
        return carry

    lax.fori_loop(0, groups, group, 0)
    lax.fori_loop(groups * CHUNK_UNROLL, n, rest, 0)


def _dispatch_kernel(ends_ref, nu_ref, list_ref, prev_ref, post_ref, *refs, tiles, n_blocks):
    n_src = len(tiles)
    srcs, (xs_ref, sorted_ref, zeros, sem, zsem) = refs[:n_src], refs[n_src:]
    i = pl.program_id(0)
    blk = EXPERT_BLOCK

    @pl.when(i == 0)
    def _():
        zeros[...] = jnp.zeros_like(zeros)

        def fill(b):
            return pltpu.make_async_copy(zeros, xs_ref.at[pl.ds(b * blk, blk)], zsem)

        def expert_tails(go):
            for e in range(N_EXPERTS):
                lo = ends_ref[e - 1] if e else 0

                @pl.when(ends_ref[e] > lo)
                def _(e=e):
                    go(fill(ends_ref[e] // blk - 1))

        def unused(go):
            def body(b, carry):
                go(fill(b))
                return carry
            lax.fori_loop(nu_ref[0], n_blocks, body, 0)

        expert_tails(lambda cp: cp.start())
        unused(lambda cp: cp.start())
        expert_tails(lambda cp: cp.wait())
        unused(lambda cp: cp.wait())

    h = srcs[0][...]
    first = tiles[0]
    for src, n in zip(srcs[1:], tiles[1:]):
        h = jnp.where(i >= first, src[...], h)
        first += n

    pos = post_ref[...]
    j = lax.broadcasted_iota(jnp.int32, (SORT_ROWS, h.shape[0]), 0)
    onehot = jnp.zeros(j.shape, F32)
    for k in range(TOP_K):
        onehot = jnp.where(j == pos[k:k + 1], 1.0, onehot)
    slot = i % 2
    sorted_ref[slot] = _pack_rows(_dot(onehot.astype(BF16), h))

    def make(buf):
        def build(local, glob):
            return pltpu.make_async_copy(sorted_ref.at[buf, pl.ds(local, SUBLANES)],
                                         xs_ref.at[pl.ds(glob, SUBLANES)], sem.at[buf])
        return build

    _chunk_copies(lambda cp: cp.start(), list_ref, make(slot))

    @pl.when(i > 0)
    def _():
        _chunk_copies(lambda cp: cp.wait(), prev_ref, make(1 - slot))

    @pl.when(i == pl.num_programs(0) - 1)
    def _():
        _chunk_copies(lambda cp: cp.wait(), list_ref, make(slot))


def _dispatch(chunks, ends, n_used, pos_t, sources, n_blocks):
    rows = SEQ_TILE
    d = sources[0].shape[1]
    tiles = tuple(s.shape[0] // rows for s in sources)
    firsts = [sum(tiles[:k]) for k in range(len(tiles))]

    def src_spec(first, n):
        return pl.BlockSpec((rows, d), lambda i, *_: (jnp.clip(i - first, 0, n - 1), 0))

    grid_spec = pltpu.PrefetchScalarGridSpec(
        num_scalar_prefetch=2,
        grid=(sum(tiles),),
        in_specs=[pl.BlockSpec((CHUNK_LANES,), lambda i, *_: (i,), memory_space=pltpu.SMEM),
                  pl.BlockSpec((CHUNK_LANES,), lambda i, *_: (jnp.maximum(i - 1, 0),),
                               memory_space=pltpu.SMEM),
                  pl.BlockSpec((SUBLANES, rows), lambda i, *_: (0, i))]
        + [src_spec(f, n) for f, n in zip(firsts, tiles)],
        out_specs=pl.BlockSpec(memory_space=pl.ANY),
        scratch_shapes=[pltpu.VMEM((2, SORT_ROWS, d // 2), jnp.int32),
                        pltpu.VMEM((EXPERT_BLOCK, d // 2), jnp.int32),
                        pltpu.SemaphoreType.DMA((2,)), pltpu.SemaphoreType.DMA],
    )
    return pl.pallas_call(
        functools.partial(_dispatch_kernel, tiles=tiles, n_blocks=n_blocks),
        grid_spec=grid_spec,
        out_shape=jax.ShapeDtypeStruct((n_blocks * EXPERT_BLOCK, d // 2), jnp.int32),
        compiler_params=_cparams(("arbitrary",)),
        name="dispatch",
    )(ends, n_used, chunks, chunks, pos_t, *sources)


def _experts_kernel(be_ref, nu_ref, hf_ref, od_ref, nx_ref, xs_ref, wgu_hbm, bgu_ref, wd_hbm,
                    bd_ref, ys_ref, wgu_bf, wd_bf, wgu_in, wd_in, gsem, dsem, *, rows, dff, layer):
    b = pl.program_id(0)
    prev = be_ref[jnp.maximum(b - 1, 0)]
    fresh = (b == 0) | (be_ref[b] != prev)
    live = b < nu_ref[0]
    half = hf_ref[b] == 1

    def fetch(expert, slot):
        return (pltpu.make_async_copy(wgu_hbm.at[layer, expert], wgu_in.at[slot], gsem.at[slot]),
                pltpu.make_async_copy(wd_hbm.at[layer, expert], wd_in.at[slot], dsem.at[slot]))

    @pl.when(live & fresh)
    def _():
        slot = od_ref[b] % 2

        @pl.when(b == 0)
        def _():
            for cp in fetch(be_ref[b], slot):
                cp.start()

        for cp in fetch(be_ref[b], slot):
            cp.wait()
        wgu_bf[...] = wgu_in[slot].astype(BF16)
        wd_bf[...] = wd_in[slot].astype(BF16)

        @pl.when(nx_ref[b] >= 0)
        def _():
            for cp in fetch(nx_ref[b], 1 - slot):
                cp.start()

    def ffn(packed):
        x = _unpack_rows(packed).astype(BF16)
        gu = _dot(x, wgu_bf[...]) + bgu_ref[0, 0]
        gate = jnp.minimum(gu[:, :dff], SWIGLU_LIMIT)
        up = jnp.clip(gu[:, dff:], -SWIGLU_LIMIT, SWIGLU_LIMIT)
        act = (up + 1.0) * (gate * _sigmoid(SWIGLU_ALPHA * gate))
        y = _dot(act.astype(BF16), wd_bf[...]) + bd_ref[0, 0]
        return _pack_rows(y.astype(BF16).astype(F32))

    @pl.when(live & jnp.logical_not(half))
    def _():
        ys_ref[...] = ffn(xs_ref[...])

    @pl.when(live & half)
    def _():
        ys_ref[:rows // 2] = ffn(xs_ref[:rows // 2])
        ys_ref[rows // 2:] = jnp.zeros((rows // 2, ys_ref.shape[1]), jnp.int32)

    @pl.when(jnp.logical_not(live))
    def _():
        ys_ref[...] = jnp.zeros_like(ys_ref)


def _experts(layer, block_expert, n_used, half, order, following, xs, w_gu, b_gu, w_down, b_down):
    n_rows = xs.shape[0]
    rows = EXPERT_BLOCK
    n_blocks = n_rows // rows
    depth, e, d, dff2 = w_gu.shape
    dff = dff2 // 2
    blk = lambda b, be, nu, *_: (jnp.minimum(b, nu[0] - 1), 0)
    per_expert = lambda b, be, *_: (layer, be[b], 0, 0)
    grid_spec = pltpu.PrefetchScalarGridSpec(
        num_scalar_prefetch=5,
        grid=(n_blocks,),
        in_specs=[
            pl.BlockSpec((rows, d // 2), blk),
            pl.BlockSpec(memory_space=pl.ANY),
            pl.BlockSpec((1, 1, 1, dff2), per_expert),
            pl.BlockSpec(memory_space=pl.ANY),
            pl.BlockSpec((1, 1, 1, d), per_expert),
        ],
        out_specs=pl.BlockSpec((rows, d // 2), lambda b, *_: (b, 0)),
        scratch_shapes=[pltpu.VMEM((d, dff2), BF16), pltpu.VMEM((dff, d), BF16),
                        pltpu.VMEM((2, d, dff2), F32), pltpu.VMEM((2, dff, d), F32),
                        pltpu.SemaphoreType.DMA((2,)), pltpu.SemaphoreType.DMA((2,))],
    )
    return pl.pallas_call(
        functools.partial(_experts_kernel, rows=rows, dff=dff, layer=layer),
        grid_spec=grid_spec,
        out_shape=jax.ShapeDtypeStruct((n_rows, d // 2), jnp.int32),
        compiler_params=_cparams(("arbitrary",)),
        name="experts",
    )(block_expert, n_used, half, order, following, xs, w_gu, b_gu.reshape(depth, e, 1, dff2),
      w_down, b_down.reshape(depth, e, 1, d))


def _combine_kernel(list_ref, next_ref, ys_ref, pos_ref, gt_ref, x1_ref, mod_ref, fg_ref, out_ref,
                    sorted_ref, sem, *, per, final):
    step = pl.program_id(0) * per + pl.program_id(1)
    steps = pl.num_programs(0) * per
    slot = step % 2

    def make(buf):
        def build(local, glob):
            return pltpu.make_async_copy(ys_ref.at[pl.ds(glob, SUBLANES)],
                                         sorted_ref.at[buf, pl.ds(local, SUBLANES)], sem.at[buf])
        return build

    @pl.when(step == 0)
    def _():
        sorted_ref[...] = jnp.zeros_like(sorted_ref)
        _chunk_copies(lambda cp: cp.start(), list_ref, make(slot))

    @pl.when(step + 1 < steps)
    def _():
        _chunk_copies(lambda cp: cp.start(), next_ref, make(1 - slot))

    _chunk_copies(lambda cp: cp.wait(), list_ref, make(slot))

    pos = pos_ref[...]
    gates = gt_ref[...]
    lane = lax.broadcasted_iota(jnp.int32, (pos.shape[0], SORT_ROWS), 1)
    weights = jnp.zeros(lane.shape, F32)
    for k in range(TOP_K):
        weights = jnp.where(lane == pos[:, k:k + 1], gates[:, k:k + 1], weights)
    y = _dot(weights.astype(BF16), _unpack_rows(sorted_ref[slot]).astype(BF16))
    x2 = x1_ref[0] + mod_ref[0][5:6] * y
    if final:
        x2 = _rms(x2, fg_ref[...])
    out_ref[0] = x2


def _combine(chunks, tile0, pos, gates, ys, x1, mod8, final_g, final):
    bsz, seq, d = x1.shape
    rows = SEQ_TILE
    per = seq // rows
    last = tile0 + bsz * per - 1
    tok = pl.BlockSpec((rows, TOP_K), lambda b, j: (tile0 + b * per + j, 0))
    return pl.pallas_call(
        functools.partial(_combine_kernel, per=per, final=final),
        grid=(bsz, per),
        in_specs=[
            pl.BlockSpec((CHUNK_LANES,), lambda b, j: (tile0 + b * per + j,),
                         memory_space=pltpu.SMEM),
            pl.BlockSpec((CHUNK_LANES,), lambda b, j: (jnp.minimum(tile0 + b * per + j + 1, last),),
                         memory_space=pltpu.SMEM),
            pl.BlockSpec(memory_space=pl.ANY),
            tok, tok,
            pl.BlockSpec((1, rows, d), lambda b, j: (b, j, 0)),
            pl.BlockSpec((1, SUBLANES, d), lambda b, j: (b, 0, 0)),
            pl.BlockSpec((1, d), lambda b, j: (0, 0)),
        ],
        out_specs=pl.BlockSpec((1, rows, d), lambda b, j: (b, j, 0)),
        out_shape=jax.ShapeDtypeStruct((bsz, seq, d), F32),
        scratch_shapes=[pltpu.VMEM((2, SORT_ROWS, d // 2), jnp.int32),
                        pltpu.SemaphoreType.DMA((2,))],
        compiler_params=_cparams(("arbitrary", "arbitrary")),
        name="combine",
    )(chunks, chunks, ys, pos, gates, x1, mod8, final_g)


def _moe(layer, streams, w_gu, b_gu, w_down, b_down, final_g, final):
    d = streams[0][0].shape[2]
    routes = [s[2] for s in streams]
    pos = jnp.concatenate([r[0] for r in routes], axis=0)
    pos_t = jnp.concatenate([r[1] for r in routes], axis=1)
    gates = jnp.concatenate([r[2] for r in routes], axis=0)
    chunk = jnp.concatenate([r[3] for r in routes], axis=0)
    span = routes[-1][4]
    t = pos.shape[0]
    tiles = chunk.shape[0]
    ends = span[0, :N_EXPERTS]
    starts = span[1, :N_EXPERTS]
    lane = jnp.arange(CHUNK_LANES, dtype=jnp.int32)
    hot = chunk[:, 1, :, None] == jnp.arange(N_EXPERTS, dtype=jnp.int32)
    base = jnp.sum(jnp.where(hot, starts, 0), axis=-1)
    chunks = jnp.where(lane == CHUNK_LANES - 1, chunk[:, 2], chunk[:, 0] + base).reshape(-1)
    blk = EXPERT_BLOCK
    max_rows = t * TOP_K + tiles * N_EXPERTS * (SUBLANES - 1)
    n_blocks = -(-max_rows // blk) + N_EXPERTS
    n_used = ends[N_EXPERTS - 1] // blk
    block_start = jnp.arange(n_blocks, dtype=jnp.int32) * blk
    bexp = jnp.sum((ends[None, :] <= block_start[:, None]).astype(jnp.int32), axis=1)
    last = jnp.sum((ends <= (n_used - 1) * blk).astype(jnp.int32))
    bexp = jnp.where(jnp.arange(n_blocks) < n_used, bexp, last)
    used = routes[-1][5][0, :N_EXPERTS].astype(jnp.int32)
    mine = bexp[:, None] == jnp.arange(N_EXPERTS, dtype=jnp.int32)
    left = jnp.sum(jnp.where(mine, starts + used, 0), axis=-1) - block_start
    half = (left <= blk // 2).astype(jnp.int32)
    ids = jnp.arange(N_EXPERTS, dtype=jnp.int32)
    owns = ends > starts
    rank = jnp.cumsum(owns.astype(jnp.int32)) - 1
    later = jnp.where(owns[None, :] & (ids[None, :] > ids[:, None]), ids[None, :], N_EXPERTS)
    after = jnp.min(later, axis=1)
    after = jnp.where(after == N_EXPERTS, -1, after)
    order = jnp.sum(jnp.where(mine, rank, 0), axis=-1)
    following = jnp.sum(jnp.where(mine, after, 0), axis=-1)
    sources = [s[1].reshape(-1, d) for s in streams]
    n_used = n_used.reshape(1)
    xs = _dispatch(chunks, ends, n_used, pos_t, sources, n_blocks)
    ys = _experts(layer, bexp, n_used, half, order, following, xs, w_gu, b_gu, w_down, b_down)
    outs, tile0 = [], 0
    for x1, _, _, mod8 in streams:
        outs.append(_combine(chunks, tile0, pos, gates, ys, x1, mod8, final_g, final))
        tile0 += x1.shape[0] * x1.shape[1] // SEQ_TILE
    return outs


def _rope(x, cos, sin):
    return x * cos + pltpu.roll(x, LANES // 2, axis=1) * sin


def _proj_kernel(xc_ref, x_ref, modc_ref, modl_ref, n1_ref, cos_ref, sin_ref,
                 wdq_ref, qg_ref, wuq_ref, wdkv_ref, kvg_ref, wukv_ref,
                 q_ref, k_ref, v_ref, *, ctx_tiles):
    j = pl.program_id(1)
    is_ctx = j < ctx_tiles
    xin = jnp.where(is_ctx, xc_ref[0], x_ref[0])
    mod = jnp.where(is_ctx, modc_ref[0], modl_ref[0])
    sh1, s1 = mod[0:1], mod[1:2]
    hb = (_rms(xin, n1_ref[...]) * (1.0 + s1) + sh1).astype(BF16)
    cos = jnp.where(is_ctx, 1.0, cos_ref[...])
    sin = jnp.where(is_ctx, 0.0, sin_ref[...])

    kva = _dot(hb, wdkv_ref[...])
    kpe = _rope(kva[:, KV_LORA:], cos, sin).astype(BF16)
    kv = _dot(_rms(kva[:, :KV_LORA], kvg_ref[...]).astype(BF16), wukv_ref[...])
    nope_all = N_HEADS * QK_NOPE
    ones = jnp.ones((kv.shape[0], V_HEAD), BF16)
    for h in range(N_HEADS):
        k_ref[0, :, h * HEAD_PAD:h * HEAD_PAD + QK_NOPE] = \
            kv[:, h * QK_NOPE:(h + 1) * QK_NOPE].astype(BF16)
        k_ref[0, :, h * HEAD_PAD + QK_NOPE:(h + 1) * HEAD_PAD] = kpe
        v_ref[0, :, h * HEAD_PAD:h * HEAD_PAD + V_HEAD] = \
            kv[:, nope_all + h * V_HEAD:nope_all + (h + 1) * V_HEAD].astype(BF16)
        v_ref[0, :, h * HEAD_PAD + V_HEAD:(h + 1) * HEAD_PAD] = ones

    @pl.when(jnp.logical_not(is_ctx))
    def _():
        qa = _dot(hb, wdq_ref[...])
        q = _dot(_rms(qa, qg_ref[...]).astype(BF16), wuq_ref[...]) * Q_SCALE
        for h in range(N_HEADS):
            lo = h * HEAD_PAD
            q_ref[0, :, lo:lo + QK_NOPE] = q[:, lo:lo + QK_NOPE].astype(BF16)
            q_ref[0, :, lo + QK_NOPE:lo + HEAD_PAD] = \
                _rope(q[:, lo + QK_NOPE:lo + HEAD_PAD], cos, sin).astype(BF16)


def _projections(xc, x, modc8, modl8, n1, cos_t, sin_t, wdq, qg, wuq, wdkv, kvg, wukv):
    bsz, seq, d = x.shape
    ctx = xc.shape[1]
    rows = SEQ_TILE
    ctx_tiles = ctx // rows
    lat = lambda b, j: (b, jnp.maximum(j - ctx_tiles, 0), 0)
    full = lambda a: pl.BlockSpec(a.shape, lambda b, j: (0,) * a.ndim)
    kw = N_HEADS * HEAD_PAD
    vw = N_HEADS * HEAD_PAD
    return pl.pallas_call(
        functools.partial(_proj_kernel, ctx_tiles=ctx_tiles),
        grid=(bsz, (ctx + seq) // rows),
        in_specs=[
            pl.BlockSpec((1, rows, d), lambda b, j: (b, jnp.minimum(j, ctx_tiles - 1), 0)),
            pl.BlockSpec((1, rows, d), lat),
            pl.BlockSpec((1, SUBLANES, d), lambda b, j: (b, 0, 0)),
            pl.BlockSpec((1, SUBLANES, d), lambda b, j: (b, 0, 0)),
            full(n1),
            pl.BlockSpec((rows, LANES), lambda b, j: (jnp.maximum(j - ctx_tiles, 0), 0)),
            pl.BlockSpec((rows, LANES), lambda b, j: (jnp.maximum(j - ctx_tiles, 0), 0)),
            full(wdq), full(qg), full(wuq), full(wdkv), full(kvg), full(wukv),
        ],
        out_specs=[
            pl.BlockSpec((1, rows, kw), lat),
            pl.BlockSpec((1, rows, kw), lambda b, j: (b, j, 0)),
            pl.BlockSpec((1, rows, vw), lambda b, j: (b, j, 0)),
        ],
        out_shape=[
            jax.ShapeDtypeStruct((bsz, seq, kw), BF16),
            jax.ShapeDtypeStruct((bsz, ctx + seq, kw), BF16),
            jax.ShapeDtypeStruct((bsz, ctx + seq, vw), BF16),
        ],
        compiler_params=_cparams(("arbitrary", "arbitrary")),
        name="projections",
    )(xc, x, modc8, modl8, n1, cos_t, sin_t, wdq, qg, wuq, wdkv, kvg, wukv)


def _attn_kernel(q_ref, k_ref, v_ref, o_ref, *, chunk):
    q = q_ref[0]
    m = acc = None
    for c in range(k_ref.shape[1] // chunk):
        rows = slice(c * chunk, (c + 1) * chunk)
        s = lax.dot_general(q, k_ref[0, rows, :], (((1,), (1,)), ((), ())),
                            preferred_element_type=F32)
        m_c = jnp.max(s, axis=-1, keepdims=True)
        m_new = m_c if c == 0 else jnp.maximum(m, m_c)
        pv = _dot(jnp.exp2(s - m_new).astype(BF16), v_ref[0, rows, :])
        acc = pv if c == 0 else acc * jnp.exp2(m - m_new) + pv
        m = m_new
    o_ref[0] = (acc[:, :V_HEAD] / acc[:, V_HEAD:]).astype(BF16)


def _attention(q, k, v):
    bsz, seq, _ = q.shape
    keys = k.shape[1]
    rows = min(Q_TILE, seq)
    chunk = next(c for c in (KEY_CHUNK, 2 * LANES, keys) if keys % c == 0)
    return pl.pallas_call(
        functools.partial(_attn_kernel, chunk=chunk),
        grid=(bsz, N_HEADS, seq // rows),
        in_specs=[
            pl.BlockSpec((1, rows, HEAD_PAD), lambda b, h, i: (b, i, h)),
            pl.BlockSpec((1, keys, HEAD_PAD), lambda b, h, i: (b, 0, h)),
            pl.BlockSpec((1, keys, HEAD_PAD), lambda b, h, i: (b, 0, h)),
        ],
        out_specs=pl.BlockSpec((1, rows, V_HEAD), lambda b, h, i: (b, i, h)),
        out_shape=jax.ShapeDtypeStruct((bsz, seq, N_HEADS * V_HEAD), BF16),
        compiler_params=_cparams(("arbitrary", "arbitrary", "arbitrary")),
        name="attention",
    )(q, k, v)


def _oproj_kernel(o_ref, x_ref, mod_ref, wo_ref, n2_ref, rw_ref, rb_ref, cin_ref,
                  x1_ref, h2_ref, *route_refs):
    first = (pl.program_id(0) == 0) & (pl.program_id(1) == 0)
    mod = mod_ref[0]
    x1 = x_ref[0] + mod[2:3] * _dot(o_ref[0], wo_ref[...])
    _tail(x1, mod, first, n2_ref, rw_ref, rb_ref, cin_ref, x1_ref, h2_ref, route_refs)


def _out_projection(o, x, mod8, wo, n2, rw, rb, carry_in):
    bsz, seq, d = x.shape
    rows = SEQ_TILE
    per = seq // rows
    const2 = lambda b, j: (0, 0)
    route_specs, route_shapes = _route_specs(bsz * per, per, rows)
    return pl.pallas_call(
        _oproj_kernel,
        grid=(bsz, per),
        in_specs=[
            pl.BlockSpec((1, rows, o.shape[2]), lambda b, j: (b, j, 0)),
            pl.BlockSpec((1, rows, d), lambda b, j: (b, j, 0)),
            pl.BlockSpec((1, SUBLANES, d), lambda b, j: (b, 0, 0)),
            pl.BlockSpec(wo.shape, const2),
            pl.BlockSpec((1, d), const2),
            pl.BlockSpec(rw.shape, const2),
            pl.BlockSpec((1, LANES), const2),
            pl.BlockSpec((SUBLANES, LANES), const2),
        ],
        out_specs=[
            pl.BlockSpec((1, rows, d), lambda b, j: (b, j, 0)),
            pl.BlockSpec((1, rows, d), lambda b, j: (b, j, 0)),
        ] + route_specs,
        out_shape=[
            jax.ShapeDtypeStruct((bsz, seq, d), F32),
            jax.ShapeDtypeStruct((bsz, seq, d), BF16),
        ] + route_shapes,
        compiler_params=_cparams(("arbitrary", "arbitrary")),
        name="out_projection",
    )(o, x, mod8, wo, n2, rw, rb, carry_in)


def _rope_cols(base):
    q = QK_ROPE // 4
    x1 = list(range(base, base + q)) + list(range(base + 2 * q, base + 3 * q))
    x2 = list(range(base + q, base + 2 * q)) + list(range(base + 3 * q, base + 4 * q))
    pad = [-1] * (LANES // 2 - 2 * q)
    return x1 + pad + x2 + pad


def _take_cols(w, cols):
    wz = jnp.concatenate([w, jnp.zeros((w.shape[0], 1), w.dtype)], axis=1)
    idx = jnp.asarray([c if c >= 0 else w.shape[1] for c in cols], jnp.int32)
    return jnp.take(wz, idx, axis=1)


def _rope_tables(seq):
    q = QK_ROPE // 4
    pos = jnp.arange(seq)
    inv = ROPE_BASE ** (-jnp.arange(0, QK_ROPE // 2, 2, dtype=F32) / (QK_ROPE // 2))
    ang = jnp.concatenate([(pos // GRID_W).astype(F32)[:, None] * inv,
                           (pos % GRID_W).astype(F32)[:, None] * inv], axis=1)
    pad1 = jnp.ones((seq, LANES // 2 - 2 * q), F32)
    pad0 = jnp.zeros((seq, LANES // 2 - 2 * q), F32)
    cos = jnp.concatenate([jnp.cos(ang), pad1, jnp.cos(ang), pad1], axis=1)
    sin = jnp.concatenate([-jnp.sin(ang), pad0, jnp.sin(ang), pad0], axis=1)
    return cos, sin


def _mod8(mod_rows):
    bsz, n = mod_rows.shape
    d = n // 6
    m = mod_rows.reshape(bsz, 6, d)
    return jnp.concatenate([m, jnp.zeros((bsz, SUBLANES - 6, d), F32)], axis=1)


def kernel(x, c, ctx, c_ctx, ada_w, ada_b, norm1_g, norm2_g, pool_w, pool_b, pool_scale,
           w_dq, q_norm_g, w_uq, w_dkv, kv_norm_g, w_ukv, w_o,
           router_w, router_b, w_gu, b_gu, w_down, b_down, final_g):
    bsz, seq, d = x.shape
    assert seq % SEQ_TILE == 0 and ctx.shape[1] % SEQ_TILE == 0 and seq % GRID_W == 0
    row = lambda a: a.reshape(1, -1)

    mod_rows = 2 * SUBLANES
    cvec = jnp.concatenate([c, c_ctx[None], jnp.zeros((mod_rows - bsz - 1, d), F32)], axis=0)
    mod = _modulation(cvec, ada_w, ada_b)
    modl = [_mod8(mod[i, :bsz]) for i in range(2)]
    modc = [_mod8(jnp.broadcast_to(mod[i, bsz:bsz + 1], (bsz, 6 * d))) for i in range(2)]

    n_exp = router_w.shape[2]
    rw = jnp.pad(router_w, ((0, 0), (0, 0), (0, LANES - n_exp)))
    rb = jnp.pad(router_b, ((0, 0), (0, LANES - n_exp)), constant_values=-jnp.inf)

    pw = pool_w[0].astype(BF16)
    args0 = (row(norm1_g[0]), row(norm2_g[0]), pw, row(pool_b[0]), row(pool_scale[0]),
             rw[0], row(rb[0]))
    no_rows = jnp.zeros((SUBLANES, LANES), F32)
    x1, h2, *route_l = _pool_layer(x, modl[0], *args0, no_rows)
    xc1, h2c, *route_c = _pool_layer(ctx, modc[0], *args0, route_l[5])
    x, xc = _moe(0, [(x1, h2, route_l, modl[0]), (xc1, h2c, route_c, modc[0])],
                 w_gu, b_gu, w_down, b_down, row(final_g), False)

    head = QK_NOPE + QK_ROPE
    q_cols = []
    for h in range(N_HEADS):
        q_cols += list(range(h * head, h * head + QK_NOPE)) + _rope_cols(h * head + QK_NOPE)
    kv_cols = list(range(KV_LORA)) + _rope_cols(KV_LORA)
    hk = QK_NOPE + V_HEAD
    ukv_cols = [h * hk + i for h in range(N_HEADS) for i in range(QK_NOPE)] + \
               [h * hk + QK_NOPE + i for h in range(N_HEADS) for i in range(V_HEAD)]
    wuq = _take_cols(w_uq[0], q_cols).astype(BF16)
    wdkv = _take_cols(w_dkv[0], kv_cols).astype(BF16)
    wukv = _take_cols(w_ukv[0], ukv_cols).astype(BF16)
    cos_t, sin_t = _rope_tables(seq)
    q, k, v = _projections(xc, x, modc[1], modl[1], row(norm1_g[1]), cos_t, sin_t,
                           w_dq[0].astype(BF16), row(q_norm_g[0]), wuq, wdkv,
                           row(kv_norm_g[0]), wukv)
    o = _attention(q, k, v)
    x1, h2, *route_l = _out_projection(o, x, modl[1], w_o[0].astype(BF16), row(norm2_g[1]),
                                       rw[1], row(rb[1]), no_rows)
    (out,) = _moe(1, [(x1, h2, route_l, modl[1])], w_gu, b_gu, w_down, b_down, row(final_g), True)
    return out
```

```python
import functools

import jax
import jax.numpy as jnp
from jax import lax
from jax.experimental import pallas as pl
from jax.experimental.pallas import tpu as pltpu

F32 = jnp.float32
BF16 = jnp.bfloat16

N_HEADS = 8
QK_NOPE = 128
QK_ROPE = 64
V_HEAD = 128
KV_LORA = 256
N_EXPERTS = 32
TOP_K = 4
POOL_WINDOWS = (2, 4, 8, 16)
GRID_W = 64
ROPE_BASE = 10000.0
ATTN_SCALE = (QK_NOPE + QK_ROPE) ** -0.5
Q_SCALE = ATTN_SCALE * 1.4426950408889634
SWIGLU_LIMIT = 7.0
SWIGLU_ALPHA = 1.702
EPS = 1e-6

LANES = 128
SUBLANES = 8
HEAD_PAD = 2 * LANES
VMEM_LIMIT = 56 * 1024 * 1024

SEQ_TILE = 256
EXPERT_BLOCK = 512
Q_TILE = 2048
KEY_CHUNK = 256
HEADS_PER_STEP = 2
MOD_COLS = 1536
HALO = 8
SORT_ROWS = -(-(TOP_K * SEQ_TILE + N_EXPERTS * (SUBLANES - 1)) // (2 * LANES)) * (2 * LANES)
CHUNK_LANES = -(-(SORT_ROWS // SUBLANES + 1) // LANES) * LANES
CHUNK_UNROLL = 4


def _cparams(sem, vmem=VMEM_LIMIT):
    return pltpu.CompilerParams(dimension_semantics=sem, vmem_limit_bytes=vmem)


def _dot(a, b):
    return jnp.dot(a, b, preferred_element_type=F32)


def _dot3(a, b):
    ah = a.astype(BF16)
    al = (a - ah.astype(F32)).astype(BF16)
    bh = b.astype(BF16)
    bl = (b - bh.astype(F32)).astype(BF16)
    return _dot(ah, bh) + _dot(ah, bl) + _dot(al, bh)


def _rms(x, g):
    return x * lax.rsqrt(jnp.mean(x * x, axis=-1, keepdims=True) + EPS) * g


def _sigmoid(x):
    return 1.0 / (1.0 + jnp.exp(-x))


HIGH_HALF = -65536


def _pack_rows(x):
    half = x.shape[1] // 2
    lo = lax.bitcast_convert_type(x[:, :half], jnp.int32)
    hi = lax.bitcast_convert_type(x[:, half:], jnp.int32)
    return (hi & HIGH_HALF) | lax.shift_right_logical(lo, 16)


def _unpack_rows(w):
    lo = lax.bitcast_convert_type(lax.shift_left(w, 16), F32)
    hi = lax.bitcast_convert_type(w & HIGH_HALF, F32)
    return jnp.concatenate([lo, hi], axis=1)


def _mod_kernel(c_ref, w_ref, b_ref, o_ref):
    a = c_ref[...]
    s = a * _sigmoid(a)
    o_ref[0] = _dot3(s, w_ref[0]) + b_ref[0]


def _modulation(cvec, ada_w, ada_b):
    depth, d, n = ada_w.shape
    rows = cvec.shape[0]
    return pl.pallas_call(
        _mod_kernel,
        grid=(depth, n // MOD_COLS),
        in_specs=[
            pl.BlockSpec((rows, d), lambda i, j: (0, 0)),
            pl.BlockSpec((1, d, MOD_COLS), lambda i, j: (i, 0, j)),
            pl.BlockSpec((1, 1, MOD_COLS), lambda i, j: (i, 0, j)),
        ],
        out_specs=pl.BlockSpec((1, rows, MOD_COLS), lambda i, j: (i, 0, j)),
        out_shape=jax.ShapeDtypeStruct((depth, rows, n), F32),
        compiler_params=_cparams(("arbitrary", "arbitrary")),
        name="modulation",
    )(cvec, ada_w, ada_b.reshape(depth, 1, n))


def _route_tile(work, first, cin_ref, pos_ref, post_ref, gt_ref, chunk_ref, span_ref, carry_ref):
    rows = work.shape[0]
    lane = lax.broadcasted_iota(jnp.int32, work.shape, 1).astype(F32)
    tops, hots = [], []
    for _ in range(TOP_K):
        m = jnp.max(work, axis=1, keepdims=True)
        idx = jnp.min(jnp.where(work == m, lane, float(LANES)), axis=1, keepdims=True)
        hot = lane == idx
        work = jnp.where(hot, -jnp.inf, work)
        tops.append(m)
        hots.append(hot)
    chosen = jnp.zeros(work.shape, F32)
    for hot in hots:
        chosen = chosen + jnp.where(hot, 1.0, 0.0)
    count = jnp.broadcast_to(jnp.sum(chosen, axis=0, keepdims=True), carry_ref.shape)
    seg_rows = jnp.floor((count + (SUBLANES - 1.0)) * (1.0 / SUBLANES)) * SUBLANES

    def lane_prefix(v):
        shift = 1
        while shift < N_EXPERTS:
            v = v + pltpu.roll(v, shift, axis=1)
            shift *= 2
        return v

    @pl.when(first)
    def _():
        carry_ref[...] = cin_ref[...]

    r = lax.broadcasted_iota(jnp.int32, (rows, rows), 0)
    c = lax.broadcasted_iota(jnp.int32, (rows, rows), 1)
    earlier = jnp.where(c < r, 1.0, 0.0).astype(BF16)
    run_end = lane_prefix(seg_rows)
    run_start = run_end - seg_rows
    before = _dot(earlier, chosen.astype(BF16)) + run_start[0:1]
    exps = [jnp.exp(t - tops[0]) for t in tops]
    den = exps[0] + exps[1] + exps[2] + exps[3]
    col = lax.broadcasted_iota(jnp.int32, (rows, TOP_K), 1)
    wide = lax.broadcasted_iota(jnp.int32, (rows, LANES), 1)
    gt = jnp.zeros((rows, TOP_K), F32)
    ps = jnp.zeros((rows, TOP_K), F32)
    ps_wide = jnp.zeros((rows, LANES), F32)
    for k in range(TOP_K):
        pos_k = jnp.sum(jnp.where(hots[k], before, 0.0), axis=1, keepdims=True)
        gt = jnp.where(col == k, exps[k] / den, gt)
        ps = jnp.where(col == k, pos_k, ps)
        ps_wide = jnp.where(wide == k, pos_k, ps_wide)
    pos_ref[...] = ps.astype(jnp.int32)
    post_ref[...] = jnp.transpose(ps_wide)[0:SUBLANES].astype(jnp.int32)
    gt_ref[...] = gt

    square = (LANES, LANES)
    srow = lax.broadcasted_iota(jnp.int32, square, 0)
    packed = jnp.where(srow == 0, jnp.broadcast_to(run_start[0:1], square),
                       jnp.where(srow == 1, jnp.broadcast_to(run_end[0:1], square),
                                 jnp.broadcast_to((carry_ref[...] - run_start)[0:1], square)))
    cols = jnp.transpose(packed)
    start_c, end_c, shift_c = cols[:, 0:1], cols[:, 1:2], cols[:, 2:3]
    erow = lax.broadcasted_iota(jnp.int32, (LANES, CHUNK_LANES), 0)
    row0 = (lax.broadcasted_iota(jnp.int32, (LANES, CHUNK_LANES), 1) * SUBLANES).astype(F32)
    holds = (erow < N_EXPERTS) & (start_c <= row0) & (row0 < end_c)
    rel = jnp.sum(jnp.where(holds, shift_c, 0.0), axis=0, keepdims=True) + row0[0:1]
    eid = jnp.sum(jnp.where(holds, erow.astype(F32), 0.0), axis=0, keepdims=True)
    n_chunks = jnp.sum(jnp.where(holds, 1.0, 0.0), keepdims=True)
    crow = lax.broadcasted_iota(jnp.int32, (SUBLANES, CHUNK_LANES), 0)
    chunk_ref[0] = jnp.where(crow == 0, rel, jnp.where(crow == 1, eid, n_chunks)).astype(jnp.int32)

    total = carry_ref[...] + seg_rows
    carry_ref[...] = total
    sub = lax.broadcasted_iota(jnp.int32, carry_ref.shape, 0)
    blocks = jnp.floor((total + (EXPERT_BLOCK - 1.0)) * (1.0 / EXPERT_BLOCK))
    ends = lane_prefix(blocks) * EXPERT_BLOCK
    span_ref[...] = jnp.where(sub == 0, ends, jnp.where(sub == 1, ends - blocks * EXPERT_BLOCK,
                                                        0.0)).astype(jnp.int32)


def _route_specs(tiles, per, rows):
    tile = lambda b, j: b * per + j
    tok = pl.BlockSpec((rows, TOP_K), lambda b, j: (tile(b, j), 0))
    keep = pl.BlockSpec((SUBLANES, LANES), lambda b, j: (0, 0))
    specs = [tok, pl.BlockSpec((SUBLANES, rows), lambda b, j: (0, tile(b, j))), tok,
             pl.BlockSpec((1, SUBLANES, CHUNK_LANES), lambda b, j: (tile(b, j), 0, 0)), keep, keep]
    t = tiles * rows
    shapes = [jax.ShapeDtypeStruct((t, TOP_K), jnp.int32),
              jax.ShapeDtypeStruct((SUBLANES, t), jnp.int32),
              jax.ShapeDtypeStruct((t, TOP_K), F32),
              jax.ShapeDtypeStruct((tiles, SUBLANES, CHUNK_LANES), jnp.int32),
              jax.ShapeDtypeStruct((SUBLANES, LANES), jnp.int32),
              jax.ShapeDtypeStruct((SUBLANES, LANES), F32)]
    return specs, shapes


def _tail(x1, mod, first, n2_ref, rw_ref, rb_ref, cin_ref, x1_ref, h2_ref, route_refs):
    sh2, s2 = mod[3:4], mod[4:5]
    h2 = _rms(x1, n2_ref[...]) * (1.0 + s2) + sh2
    x1_ref[0] = x1
    h2_ref[0] = h2.astype(BF16)
    _route_tile(_dot3(h2, rw_ref[...]) + rb_ref[...], first, cin_ref, *route_refs)


def _pool_kernel(x_ref, mod_ref, n1_ref, n2_ref, pw_ref, pb_ref, ps_ref, rw_ref, rb_ref, cin_ref,
                 x1_ref, h2_ref, *route_refs, rows, seq):
    j = pl.program_id(1)
    first = (pl.program_id(0) == 0) & (j == 0)
    start = pl.multiple_of(j * rows, rows)
    prev0 = pl.multiple_of(jnp.maximum(start - HALO, 0), HALO)
    next0 = pl.multiple_of(jnp.minimum(start + rows, seq - HALO), HALO)
    xm = x_ref[0, pl.ds(start, rows), :]
    xe = jnp.concatenate(
        [x_ref[0, pl.ds(prev0, HALO), :], xm, x_ref[0, pl.ds(next0, HALO), :]], axis=0)
    mod = mod_ref[0]
    sh1, s1, g1 = mod[0:1], mod[1:2], mod[2:3]
    h = _rms(xe, n1_ref[...]) * (1.0 + s1) + sh1
    pos = start - HALO + lax.broadcasted_iota(jnp.int32, (rows + 2 * HALO, 1), 0)
    hz = jnp.where((pos >= 0) & (pos < seq), h, 0.0)
    tpos = start + lax.broadcasted_iota(jnp.int32, (rows, 1), 0)
    group = hz.shape[1] // len(POOL_WINDOWS)
    ys = []
    for g, w in enumerate(POOL_WINDOWS):
        half = w // 2
        hg = hz[:, g * group:(g + 1) * group]
        run, span = hg, 1
        while span < w:
            run = run[:run.shape[0] - span] + run[span:]
            span *= 2
        tot = run[HALO - half:HALO - half + rows]
        cnt = (jnp.minimum(tpos + half, seq) - jnp.maximum(tpos - half, 0)).astype(F32)
        dlt = tot / cnt - hg[HALO:HALO + rows]
        ys.append(_dot(dlt.astype(BF16), pw_ref[g]))
    y = (jnp.concatenate(ys, axis=1) + pb_ref[...]) * ps_ref[...]
    x1 = xm + g1 * y
    _tail(x1, mod, first, n2_ref, rw_ref, rb_ref, cin_ref, x1_ref, h2_ref, route_refs)


def _pool_layer(x, mod8, n1, n2, pw, pb, ps, rw, rb, carry_in):
    bsz, seq, d = x.shape
    rows = SEQ_TILE
    per = seq // rows
    grid = (bsz, per)
    const2 = lambda b, j: (0, 0)
    route_specs, route_shapes = _route_specs(bsz * per, per, rows)
    return pl.pallas_call(
        functools.partial(_pool_kernel, rows=rows, seq=seq),
        grid=grid,
        in_specs=[
            pl.BlockSpec((1, seq, d), lambda b, j: (b, 0, 0)),
            pl.BlockSpec((1, SUBLANES, d), lambda b, j: (b, 0, 0)),
            pl.BlockSpec((1, d), const2),
            pl.BlockSpec((1, d), const2),
            pl.BlockSpec(pw.shape, lambda b, j: (0, 0, 0)),
            pl.BlockSpec((1, d), const2),
            pl.BlockSpec((1, d), const2),
            pl.BlockSpec(rw.shape, const2),
            pl.BlockSpec((1, LANES), const2),
            pl.BlockSpec((SUBLANES, LANES), const2),
        ],
        out_specs=[
            pl.BlockSpec((1, rows, d), lambda b, j: (b, j, 0)),
            pl.BlockSpec((1, rows, d), lambda b, j: (b, j, 0)),
        ] + route_specs,
        out_shape=[
            jax.ShapeDtypeStruct((bsz, seq, d), F32),
            jax.ShapeDtypeStruct((bsz, seq, d), BF16),
        ] + route_shapes,
        compiler_params=_cparams(("arbitrary", "arbitrary")),
        name="pool_layer",
    )(x, mod8, n1, n2, pw, pb, ps, rw, rb, carry_in)


def _chunk_copies(go, list_ref, make):
    n = list_ref[CHUNK_LANES - 1]
    groups = n // CHUNK_UNROLL

    def one(c):
        go(make(pl.multiple_of(c * SUBLANES, SUBLANES), pl.multiple_of(list_ref[c], SUBLANES)))

    def group(g, carry):
        for u in range(CHUNK_UNROLL):
            one(g * CHUNK_UNROLL + u)
        return carry

    def rest(c, carry):
        one(c)
        return carry

    lax.fori_loop(0, groups, group, 0)
    lax.fori_loop(groups * CHUNK_UNROLL, n, rest, 0)


def _dispatch_kernel(ends_ref, nu_ref, list_ref, prev_ref, post_ref, *refs, tiles, n_blocks):
    n_src = len(tiles)
    srcs, (xs_ref, sorted_ref, zeros, sem, zsem) = refs[:n_src], refs[n_src:]
    i = pl.program_id(0)
    blk = EXPERT_BLOCK

    @pl.when(i == 0)
    def _():
        zeros[...] = jnp.zeros_like(zeros)

        def fill(b):
            return pltpu.make_async_copy(zeros, xs_ref.at[pl.ds(b * blk, blk)], zsem)

        def expert_tails(go):
            for e in range(N_EXPERTS):
                lo = ends_ref[e - 1] if e else 0

                @pl.when(ends_ref[e] > lo)
                def _(e=e):
                    go(fill(ends_ref[e] // blk - 1))

        def unused(go):
            def body(b, carry):
                go(fill(b))
                return carry
            lax.fori_loop(nu_ref[0], n_blocks, body, 0)

        expert_tails(lambda cp: cp.start())
        unused(lambda cp: cp.start())
        expert_tails(lambda cp: cp.wait())
        unused(lambda cp: cp.wait())

    h = srcs[0][...]
    first = tiles[0]
    for src, n in zip(srcs[1:], tiles[1:]):
        h = jnp.where(i >= first, src[...], h)
        first += n

    pos = post_ref[...]
    j = lax.broadcasted_iota(jnp.int32, (SORT_ROWS, h.shape[0]), 0)
    onehot = jnp.zeros(j.shape, F32)
    for k in range(TOP_K):
        onehot = jnp.where(j == pos[k:k + 1], 1.0, onehot)
    slot = i % 2
    sorted_ref[slot] = _pack_rows(_dot(onehot.astype(BF16), h))

    def make(buf):
        def build(local, glob):
            return pltpu.make_async_copy(sorted_ref.at[buf, pl.ds(local, SUBLANES)],
                                         xs_ref.at[pl.ds(glob, SUBLANES)], sem.at[buf])
        return build

    _chunk_copies(lambda cp: cp.start(), list_ref, make(slot))

    @pl.when(i > 0)
    def _():
        _chunk_copies(lambda cp: cp.wait(), prev_ref, make(1 - slot))

    @pl.when(i == pl.num_programs(0) - 1)
    def _():
        _chunk_copies(lambda cp: cp.wait(), list_ref, make(slot))


def _dispatch(chunks, ends, n_used, pos_t, sources, n_blocks):
    rows = SEQ_TILE
    d = sources[0].shape[1]
    tiles = tuple(s.shape[0] // rows for s in sources)
    firsts = [sum(tiles[:k]) for k in range(len(tiles))]

    def src_spec(first, n):
        return pl.BlockSpec((rows, d), lambda i, *_: (jnp.clip(i - first, 0, n - 1), 0))

    grid_spec = pltpu.PrefetchScalarGridSpec(
        num_scalar_prefetch=2,
        grid=(sum(tiles),),
        in_specs=[pl.BlockSpec((CHUNK_LANES,), lambda i, *_: (i,), memory_space=pltpu.SMEM),
                  pl.BlockSpec((CHUNK_LANES,), lambda i, *_: (jnp.maximum(i - 1, 0),),
                               memory_space=pltpu.SMEM),
                  pl.BlockSpec((SUBLANES, rows), lambda i, *_: (0, i))]
        + [src_spec(f, n) for f, n in zip(firsts, tiles)],
        out_specs=pl.BlockSpec(memory_space=pl.ANY),
        scratch_shapes=[pltpu.VMEM((2, SORT_ROWS, d // 2), jnp.int32),
                        pltpu.VMEM((EXPERT_BLOCK, d // 2), jnp.int32),
                        pltpu.SemaphoreType.DMA((2,)), pltpu.SemaphoreType.DMA],
    )
    return pl.pallas_call(
        functools.partial(_dispatch_kernel, tiles=tiles, n_blocks=n_blocks),
        grid_spec=grid_spec,
        out_shape=jax.ShapeDtypeStruct((n_blocks * EXPERT_BLOCK, d // 2), jnp.int32),
        compiler_params=_cparams(("arbitrary",)),
        name="dispatch",
    )(ends, n_used, chunks, chunks, pos_t, *sources)


def _experts_kernel(be_ref, nu_ref, hf_ref, od_ref, nx_ref, xs_ref, wgu_hbm, bgu_ref, wd_hbm,
                    bd_ref, ys_ref, wgu_bf, wd_bf, wgu_in, wd_in, gsem, dsem, *, rows, dff, layer):
    b = pl.program_id(0)
    prev = be_ref[jnp.maximum(b - 1, 0)]
    fresh = (b == 0) | (be_ref[b] != prev)
    live = b < nu_ref[0]
    half = hf_ref[b] == 1

    def fetch(expert, slot):
        return (pltpu.make_async_copy(wgu_hbm.at[layer, expert], wgu_in.at[slot], gsem.at[slot]),
                pltpu.make_async_copy(wd_hbm.at[layer, expert], wd_in.at[slot], dsem.at[slot]))

    @pl.when(live & fresh)
    def _():
        slot = od_ref[b] % 2

        @pl.when(b == 0)
        def _():
            for cp in fetch(be_ref[b], slot):
                cp.start()

        for cp in fetch(be_ref[b], slot):
            cp.wait()
        wgu_bf[...] = wgu_in[slot].astype(BF16)
        wd_bf[...] = wd_in[slot].astype(BF16)

        @pl.when(nx_ref[b] >= 0)
        def _():
            for cp in fetch(nx_ref[b], 1 - slot):
                cp.start()

    def ffn(packed):
        x = _unpack_rows(packed).astype(BF16)
        gu = _dot(x, wgu_bf[...]) + bgu_ref[0, 0]
        gate = jnp.minimum(gu[:, :dff], SWIGLU_LIMIT)
        up = jnp.clip(gu[:, dff:], -SWIGLU_LIMIT, SWIGLU_LIMIT)
        act = (up + 1.0) * (gate * _sigmoid(SWIGLU_ALPHA * gate))
        y = _dot(act.astype(BF16), wd_bf[...]) + bd_ref[0, 0]
        return _pack_rows(y.astype(BF16).astype(F32))

    @pl.when(live & jnp.logical_not(half))
    def _():
        ys_ref[...] = ffn(xs_ref[...])

    @pl.when(live & half)
    def _():
        ys_ref[:rows // 2] = ffn(xs_ref[:rows // 2])
        ys_ref[rows // 2:] = jnp.zeros((rows // 2, ys_ref.shape[1]), jnp.int32)

    @pl.when(jnp.logical_not(live))
    def _():
        ys_ref[...] = jnp.zeros_like(ys_ref)


def _experts(layer, block_expert, n_used, half, order, following, xs, w_gu, b_gu, w_down, b_down):
    n_rows = xs.shape[0]
    rows = EXPERT_BLOCK
    n_blocks = n_rows // rows
    depth, e, d, dff2 = w_gu.shape
    dff = dff2 // 2
    blk = lambda b, be, nu, *_: (jnp.minimum(b, nu[0] - 1), 0)
    per_expert = lambda b, be, *_: (layer, be[b], 0, 0)
    grid_spec = pltpu.PrefetchScalarGridSpec(
        num_scalar_prefetch=5,
        grid=(n_blocks,),
        in_specs=[
            pl.BlockSpec((rows, d // 2), blk),
            pl.BlockSpec(memory_space=pl.ANY),
            pl.BlockSpec((1, 1, 1, dff2), per_expert),
            pl.BlockSpec(memory_space=pl.ANY),
            pl.BlockSpec((1, 1, 1, d), per_expert),
        ],
        out_specs=pl.BlockSpec((rows, d // 2), lambda b, *_: (b, 0)),
        scratch_shapes=[pltpu.VMEM((d, dff2), BF16), pltpu.VMEM((dff, d), BF16),
                        pltpu.VMEM((2, d, dff2), F32), pltpu.VMEM((2, dff, d), F32),
                        pltpu.SemaphoreType.DMA((2,)), pltpu.SemaphoreType.DMA((2,))],
    )
    return pl.pallas_call(
        functools.partial(_experts_kernel, rows=rows, dff=dff, layer=layer),
        grid_spec=grid_spec,
        out_shape=jax.ShapeDtypeStruct((n_rows, d // 2), jnp.int32),
        compiler_params=_cparams(("arbitrary",)),
        name="experts",
    )(block_expert, n_used, half, order, following, xs, w_gu, b_gu.reshape(depth, e, 1, dff2),
      w_down, b_down.reshape(depth, e, 1, d))


def _combine_kernel(list_ref, next_ref, ys_ref, pos_ref, gt_ref, x1_ref, mod_ref, fg_ref, out_ref,
                    sorted_ref, sem, *, per, final):
    step = pl.program_id(0) * per + pl.program_id(1)
    steps = pl.num_programs(0) * per
    slot = step % 2

    def make(buf):
        def build(local, glob):
            return pltpu.make_async_copy(ys_ref.at[pl.ds(glob, SUBLANES)],
                                         sorted_ref.at[buf, pl.ds(local, SUBLANES)], sem.at[buf])
        return build

    @pl.when(step == 0)
    def _():
        sorted_ref[...] = jnp.zeros_like(sorted_ref)
        _chunk_copies(lambda cp: cp.start(), list_ref, make(slot))

    @pl.when(step + 1 < steps)
    def _():
        _chunk_copies(lambda cp: cp.start(), next_ref, make(1 - slot))

    _chunk_copies(lambda cp: cp.wait(), list_ref, make(slot))

    pos = pos_ref[...]
    gates = gt_ref[...]
    lane = lax.broadcasted_iota(jnp.int32, (pos.shape[0], SORT_ROWS), 1)
    weights = jnp.zeros(lane.shape, F32)
    for k in range(TOP_K):
        weights = jnp.where(lane == pos[:, k:k + 1], gates[:, k:k + 1], weights)
    y = _dot(weights.astype(BF16), _unpack_rows(sorted_ref[slot]).astype(BF16))
    x2 = x1_ref[0] + mod_ref[0][5:6] * y
    if final:
        x2 = _rms(x2, fg_ref[...])
    out_ref[0] = x2


def _combine(chunks, tile0, pos, gates, ys, x1, mod8, final_g, final):
    bsz, seq, d = x1.shape
    rows = SEQ_TILE
    per = seq // rows
    last = tile0 + bsz * per - 1
    tok = pl.BlockSpec((rows, TOP_K), lambda b, j: (tile0 + b * per + j, 0))
    return pl.pallas_call(
        functools.partial(_combine_kernel, per=per, final=final),
        grid=(bsz, per),
        in_specs=[
            pl.BlockSpec((CHUNK_LANES,), lambda b, j: (tile0 + b * per + j,),
                         memory_space=pltpu.SMEM),
            pl.BlockSpec((CHUNK_LANES,), lambda b, j: (jnp.minimum(tile0 + b * per + j + 1, last),),
                         memory_space=pltpu.SMEM),
            pl.BlockSpec(memory_space=pl.ANY),
            tok, tok,
            pl.BlockSpec((1, rows, d), lambda b, j: (b, j, 0)),
            pl.BlockSpec((1, SUBLANES, d), lambda b, j: (b, 0, 0)),
            pl.BlockSpec((1, d), lambda b, j: (0, 0)),
        ],
        out_specs=pl.BlockSpec((1, rows, d), lambda b, j: (b, j, 0)),
        out_shape=jax.ShapeDtypeStruct((bsz, seq, d), F32),
        scratch_shapes=[pltpu.VMEM((2, SORT_ROWS, d // 2), jnp.int32),
                        pltpu.SemaphoreType.DMA((2,))],
        compiler_params=_cparams(("arbitrary", "arbitrary")),
        name="combine",
    )(chunks, chunks, ys, pos, gates, x1, mod8, final_g)


def _moe(layer, streams, w_gu, b_gu, w_down, b_down, final_g, final):
    d = streams[0][0].shape[2]
    routes = [s[2] for s in streams]
    pos = jnp.concatenate([r[0] for r in routes], axis=0)
    pos_t = jnp.concatenate([r[1] for r in routes], axis=1)
    gates = jnp.concatenate([r[2] for r in routes], axis=0)
    chunk = jnp.concatenate([r[3] for r in routes], axis=0)
    span = routes[-1][4]
    t = pos.shape[0]
    tiles = chunk.shape[0]
    ends = span[0, :N_EXPERTS]
    starts = span[1, :N_EXPERTS]
    lane = jnp.arange(CHUNK_LANES, dtype=jnp.int32)
    hot = chunk[:, 1, :, None] == jnp.arange(N_EXPERTS, dtype=jnp.int32)
    base = jnp.sum(jnp.where(hot, starts, 0), axis=-1)
    chunks = jnp.where(lane == CHUNK_LANES - 1, chunk[:, 2], chunk[:, 0] + base).reshape(-1)
    blk = EXPERT_BLOCK
    max_rows = t * TOP_K + tiles * N_EXPERTS * (SUBLANES - 1)
    n_blocks = -(-max_rows // blk) + N_EXPERTS
    n_used = ends[N_EXPERTS - 1] // blk
    block_start = jnp.arange(n_blocks, dtype=jnp.int32) * blk
    bexp = jnp.sum((ends[None, :] <= block_start[:, None]).astype(jnp.int32), axis=1)
    last = jnp.sum((ends <= (n_used - 1) * blk).astype(jnp.int32))
    bexp = jnp.where(jnp.arange(n_blocks) < n_used, bexp, last)
    used = routes[-1][5][0, :N_EXPERTS].astype(jnp.int32)
    mine = bexp[:, None] == jnp.arange(N_EXPERTS, dtype=jnp.int32)
    left = jnp.sum(jnp.where(mine, starts + used, 0), axis=-1) - block_start
    half = (left <= blk // 2).astype(jnp.int32)
    ids = jnp.arange(N_EXPERTS, dtype=jnp.int32)
    owns = ends > starts
    rank = jnp.cumsum(owns.astype(jnp.int32)) - 1
    later = jnp.where(owns[None, :] & (ids[None, :] > ids[:, None]), ids[None, :], N_EXPERTS)
    after = jnp.min(later, axis=1)
    after = jnp.where(after == N_EXPERTS, -1, after)
    order = jnp.sum(jnp.where(mine, rank, 0), axis=-1)
    following = jnp.sum(jnp.where(mine, after, 0), axis=-1)
    sources = [s[1].reshape(-1, d) for s in streams]
    n_used = n_used.reshape(1)
    xs = _dispatch(chunks, ends, n_used, pos_t, sources, n_blocks)
    ys = _experts(layer, bexp, n_used, half, order, following, xs, w_gu, b_gu, w_down, b_down)
    outs, tile0 = [], 0
    for x1, _, _, mod8 in streams:
        outs.append(_combine(chunks, tile0, pos, gates, ys, x1, mod8, final_g, final))
        tile0 += x1.shape[0] * x1.shape[1] // SEQ_TILE
    return outs


def _rope(x, cos, sin):
    return x * cos + pltpu.roll(x, LANES // 2, axis=1) * sin


def _proj_kernel(xc_ref, x_ref, modc_ref, modl_ref, n1_ref, cos_ref, sin_ref,
                 wdq_ref, qg_ref, wuq_ref, wdkv_ref, kvg_ref, wukv_ref,
                 q_ref, k_ref, v_ref, *, ctx_tiles):
    j = pl.program_id(1)
    is_ctx = j < ctx_tiles
    xin = jnp.where(is_ctx, xc_ref[0], x_ref[0])
    mod = jnp.where(is_ctx, modc_ref[0], modl_ref[0])
    sh1, s1 = mod[0:1], mod[1:2]
    hb = (_rms(xin, n1_ref[...]) * (1.0 + s1) + sh1).astype(BF16)
    cos = jnp.where(is_ctx, 1.0, cos_ref[...])
    sin = jnp.where(is_ctx, 0.0, sin_ref[...])

    kva = _dot(hb, wdkv_ref[...])
    kpe = _rope(kva[:, KV_LORA:], cos, sin).astype(BF16)
    kv = _dot(_rms(kva[:, :KV_LORA], kvg_ref[...]).astype(BF16), wukv_ref[...])
    nope_all = N_HEADS * QK_NOPE
    ones = jnp.ones((kv.shape[0], V_HEAD), BF16)
    for h in range(N_HEADS):
        k_ref[0, :, h * HEAD_PAD:h * HEAD_PAD + QK_NOPE] = \
            kv[:, h * QK_NOPE:(h + 1) * QK_NOPE].astype(BF16)
        k_ref[0, :, h * HEAD_PAD + QK_NOPE:(h + 1) * HEAD_PAD] = kpe
        v_ref[0, :, h * HEAD_PAD:h * HEAD_PAD + V_HEAD] = \
            kv[:, nope_all + h * V_HEAD:nope_all + (h + 1) * V_HEAD].astype(BF16)
        v_ref[0, :, h * HEAD_PAD + V_HEAD:(h + 1) * HEAD_PAD] = ones

    @pl.when(jnp.logical_not(is_ctx))
    def _():
        qa = _dot(hb, wdq_ref[...])
        q = _dot(_rms(qa, qg_ref[...]).astype(BF16), wuq_ref[...]) * Q_SCALE
        for h in range(N_HEADS):
            lo = h * HEAD_PAD
            q_ref[0, :, lo:lo + QK_NOPE] = q[:, lo:lo + QK_NOPE].astype(BF16)
            q_ref[0, :, lo + QK_NOPE:lo + HEAD_PAD] = \
                _rope(q[:, lo + QK_NOPE:lo + HEAD_PAD], cos, sin).astype(BF16)


def _projections(xc, x, modc8, modl8, n1, cos_t, sin_t, wdq, qg, wuq, wdkv, kvg, wukv):
    bsz, seq, d = x.shape
    ctx = xc.shape[1]
    rows = SEQ_TILE
    ctx_tiles = ctx // rows
    lat = lambda b, j: (b, jnp.maximum(j - ctx_tiles, 0), 0)
    full = lambda a: pl.BlockSpec(a.shape, lambda b, j: (0,) * a.ndim)
    kw = N_HEADS * HEAD_PAD
    vw = N_HEADS * HEAD_PAD
    return pl.pallas_call(
        functools.partial(_proj_kernel, ctx_tiles=ctx_tiles),
        grid=(bsz, (ctx + seq) // rows),
        in_specs=[
            pl.BlockSpec((1, rows, d), lambda b, j: (b, jnp.minimum(j, ctx_tiles - 1), 0)),
            pl.BlockSpec((1, rows, d), lat),
            pl.BlockSpec((1, SUBLANES, d), lambda b, j: (b, 0, 0)),
            pl.BlockSpec((1, SUBLANES, d), lambda b, j: (b, 0, 0)),
            full(n1),
            pl.BlockSpec((rows, LANES), lambda b, j: (jnp.maximum(j - ctx_tiles, 0), 0)),
            pl.BlockSpec((rows, LANES), lambda b, j: (jnp.maximum(j - ctx_tiles, 0), 0)),
            full(wdq), full(qg), full(wuq), full(wdkv), full(kvg), full(wukv),
        ],
        out_specs=[
            pl.BlockSpec((1, rows, kw), lat),
            pl.BlockSpec((1, rows, kw), lambda b, j: (b, j, 0)),
            pl.BlockSpec((1, rows, vw), lambda b, j: (b, j, 0)),
        ],
        out_shape=[
            jax.ShapeDtypeStruct((bsz, seq, kw), BF16),
            jax.ShapeDtypeStruct((bsz, ctx + seq, kw), BF16),
            jax.ShapeDtypeStruct((bsz, ctx + seq, vw), BF16),
        ],
        compiler_params=_cparams(("arbitrary", "arbitrary")),
        name="projections",
    )(xc, x, modc8, modl8, n1, cos_t, sin_t, wdq, qg, wuq, wdkv, kvg, wukv)


def _attn_kernel(q_ref, k_ref, v_ref, o_ref, *, chunk):
    for h in range(HEADS_PER_STEP):
        cols = slice(h * HEAD_PAD, (h + 1) * HEAD_PAD)
        q = q_ref[0, :, cols]
        m = acc = None
        for c in range(k_ref.shape[1] // chunk):
            rows = slice(c * chunk, (c + 1) * chunk)
            s = lax.dot_general(q, k_ref[0, rows, cols], (((1,), (1,)), ((), ())),
                                preferred_element_type=F32)
            m_c = jnp.max(s, axis=-1, keepdims=True)
            m_new = m_c if c == 0 else jnp.maximum(m, m_c)
            pv = _dot(jnp.exp2(s - m_new).astype(BF16), v_ref[0, rows, cols])
            acc = pv if c == 0 else acc * jnp.exp2(m - m_new) + pv
            m = m_new
        o_ref[0, :, h * V_HEAD:(h + 1) * V_HEAD] = \
            (acc[:, :V_HEAD] / acc[:, V_HEAD:]).astype(BF16)


def _attention(q, k, v):
    bsz, seq, _ = q.shape
    keys = k.shape[1]
    rows = min(Q_TILE, seq)
    chunk = next(c for c in (KEY_CHUNK, 2 * LANES, keys) if keys % c == 0)
    return pl.pallas_call(
        functools.partial(_attn_kernel, chunk=chunk),
        grid=(bsz, N_HEADS // HEADS_PER_STEP, seq // rows),
        in_specs=[
            pl.BlockSpec((1, rows, HEADS_PER_STEP * HEAD_PAD), lambda b, h, i: (b, i, h)),
            pl.BlockSpec((1, keys, HEADS_PER_STEP * HEAD_PAD), lambda b, h, i: (b, 0, h)),
            pl.BlockSpec((1, keys, HEADS_PER_STEP * HEAD_PAD), lambda b, h, i: (b, 0, h)),
        ],
        out_specs=pl.BlockSpec((1, rows, HEADS_PER_STEP * V_HEAD), lambda b, h, i: (b, i, h)),
        out_shape=jax.ShapeDtypeStruct((bsz, seq, N_HEADS * V_HEAD), BF16),
        compiler_params=_cparams(("arbitrary", "arbitrary", "arbitrary")),
        name="attention",
    )(q, k, v)


def _oproj_kernel(o_ref, x_ref, mod_ref, wo_ref, n2_ref, rw_ref, rb_ref, cin_ref,
                  x1_ref, h2_ref, *route_refs):
    first = (pl.program_id(0) == 0) & (pl.program_id(1) == 0)
    mod = mod_ref[0]
    x1 = x_ref[0] + mod[2:3] * _dot(o_ref[0], wo_ref[...])
    _tail(x1, mod, first, n2_ref, rw_ref, rb_ref, cin_ref, x1_ref, h2_ref, route_refs)


def _out_projection(o, x, mod8, wo, n2, rw, rb, carry_in):
    bsz, seq, d = x.shape
    rows = SEQ_TILE
    per = seq // rows
    const2 = lambda b, j: (0, 0)
    route_specs, route_shapes = _route_specs(bsz * per, per, rows)
    return pl.pallas_call(
        _oproj_kernel,
        grid=(bsz, per),
        in_specs=[
            pl.BlockSpec((1, rows, o.shape[2]), lambda b, j: (b, j, 0)),
            pl.BlockSpec((1, rows, d), lambda b, j: (b, j, 0)),
            pl.BlockSpec((1, SUBLANES, d), lambda b, j: (b, 0, 0)),
            pl.BlockSpec(wo.shape, const2),
            pl.BlockSpec((1, d), const2),
            pl.BlockSpec(rw.shape, const2),
            pl.BlockSpec((1, LANES), const2),
            pl.BlockSpec((SUBLANES, LANES), const2),
        ],
        out_specs=[
            pl.BlockSpec((1, rows, d), lambda b, j: (b, j, 0)),
            pl.BlockSpec((1, rows, d), lambda b, j: (b, j, 0)),
        ] + route_specs,
        out_shape=[
            jax.ShapeDtypeStruct((bsz, seq, d), F32),
            jax.ShapeDtypeStruct((bsz, seq, d), BF16),
        ] + route_shapes,
        compiler_params=_cparams(("arbitrary", "arbitrary")),
        name="out_projection",
    )(o, x, mod8, wo, n2, rw, rb, carry_in)


def _rope_cols(base):
    q = QK_ROPE // 4
    x1 = list(range(base, base + q)) + list(range(base + 2 * q, base + 3 * q))
    x2 = list(range(base + q, base + 2 * q)) + list(range(base + 3 * q, base + 4 * q))
    pad = [-1] * (LANES // 2 - 2 * q)
    return x1 + pad + x2 + pad


def _take_cols(w, cols):
    wz = jnp.concatenate([w, jnp.zeros((w.shape[0], 1), w.dtype)], axis=1)
    idx = jnp.asarray([c if c >= 0 else w.shape[1] for c in cols], jnp.int32)
    return jnp.take(wz, idx, axis=1)


def _rope_tables(seq):
    q = QK_ROPE // 4
    pos = jnp.arange(seq)
    inv = ROPE_BASE ** (-jnp.arange(0, QK_ROPE // 2, 2, dtype=F32) / (QK_ROPE // 2))
    ang = jnp.concatenate([(pos // GRID_W).astype(F32)[:, None] * inv,
                           (pos % GRID_W).astype(F32)[:, None] * inv], axis=1)
    pad1 = jnp.ones((seq, LANES // 2 - 2 * q), F32)
    pad0 = jnp.zeros((seq, LANES // 2 - 2 * q), F32)
    cos = jnp.concatenate([jnp.cos(ang), pad1, jnp.cos(ang), pad1], axis=1)
    sin = jnp.concatenate([-jnp.sin(ang), pad0, jnp.sin(ang), pad0], axis=1)
    return cos, sin


def _mod8(mod_rows):
    bsz, n = mod_rows.shape
    d = n // 6
    m = mod_rows.reshape(bsz, 6, d)
    return jnp.concatenate([m, jnp.zeros((bsz, SUBLANES - 6, d), F32)], axis=1)


def kernel(x, c, ctx, c_ctx, ada_w, ada_b, norm1_g, norm2_g, pool_w, pool_b, pool_scale,
           w_dq, q_norm_g, w_uq, w_dkv, kv_norm_g, w_ukv, w_o,
           router_w, router_b, w_gu, b_gu, w_down, b_down, final_g):
    bsz, seq, d = x.shape
    assert seq % SEQ_TILE == 0 and ctx.shape[1] % SEQ_TILE == 0 and seq % GRID_W == 0
    row = lambda a: a.reshape(1, -1)

    mod_rows = 2 * SUBLANES
    cvec = jnp.concatenate([c, c_ctx[None], jnp.zeros((mod_rows - bsz - 1, d), F32)], axis=0)
    mod = _modulation(cvec, ada_w, ada_b)
    modl = [_mod8(mod[i, :bsz]) for i in range(2)]
    modc = [_mod8(jnp.broadcast_to(mod[i, bsz:bsz + 1], (bsz, 6 * d))) for i in range(2)]

    n_exp = router_w.shape[2]
    rw = jnp.pad(router_w, ((0, 0), (0, 0), (0, LANES - n_exp)))
    rb = jnp.pad(router_b, ((0, 0), (0, LANES - n_exp)), constant_values=-jnp.inf)

    pw = pool_w[0].astype(BF16)
    args0 = (row(norm1_g[0]), row(norm2_g[0]), pw, row(pool_b[0]), row(pool_scale[0]),
             rw[0], row(rb[0]))
    no_rows = jnp.zeros((SUBLANES, LANES), F32)
    x1, h2, *route_l = _pool_layer(x, modl[0], *args0, no_rows)
    xc1, h2c, *route_c = _pool_layer(ctx, modc[0], *args0, route_l[5])
    x, xc = _moe(0, [(x1, h2, route_l, modl[0]), (xc1, h2c, route_c, modc[0])],
                 w_gu, b_gu, w_down, b_down, row(final_g), False)

    head = QK_NOPE + QK_ROPE
    q_cols = []
    for h in range(N_HEADS):
        q_cols += list(range(h * head, h * head + QK_NOPE)) + _rope_cols(h * head + QK_NOPE)
    kv_cols = list(range(KV_LORA)) + _rope_cols(KV_LORA)
    hk = QK_NOPE + V_HEAD
    ukv_cols = [h * hk + i for h in range(N_HEADS) for i in range(QK_NOPE)] + \
               [h * hk + QK_NOPE + i for h in range(N_HEADS) for i in range(V_HEAD)]
    wuq = _take_cols(w_uq[0], q_cols).astype(BF16)
    wdkv = _take_cols(w_dkv[0], kv_cols).astype(BF16)
    wukv = _take_cols(w_ukv[0], ukv_cols).astype(BF16)
    cos_t, sin_t = _rope_tables(seq)
    q, k, v = _projections(xc, x, modc[1], modl[1], row(norm1_g[1]), cos_t, sin_t,
                           w_dq[0].astype(BF16), row(q_norm_g[0]), wuq, wdkv,
                           row(kv_norm_g[0]), wukv)
    o = _attention(q, k, v)
    x1, h2, *route_l = _out_projection(o, x, modl[1], w_o[0].astype(BF16), row(norm2_g[1]),
                                       rw[1], row(rb[1]), no_rows)
    (out,) = _moe(1, [(x1, h2, route_l, modl[1])], w_gu, b_gu, w_down, b_down, row(final_g), True)
    return out
```
